```python
import math
import jax
import jax.numpy as jnp
from jax import lax
import numpy as np

D_MODEL = 1024
BATCH = 16
SEQ = 4096
DEPTH = 1
DEC_BATCH = 128
DEC_SEQ = 1
PAST_LEN = 8192
PAGE_SIZE = 128

NSA_HEADS = 8
NSA_KV_HEADS = 2
NSA_GROUP = NSA_HEADS // NSA_KV_HEADS
NSA_HEAD_DIM = 64
NSA_WIDTH = NSA_HEADS * NSA_HEAD_DIM
KV_WIDTH = NSA_KV_HEADS * NSA_HEAD_DIM
CMP_BLOCK = 32
CMP_STRIDE = 16
CMP_HIDDEN = 128
SEL_BLOCK = 64
SEL_TOP = 16
N_LOCAL_SEL = 2
FORCE_BONUS = 1.0e4
WINDOW = 512
NSA_Q_BLOCK = 64
HGRN_HEADS = 4
HGRN_HEAD_DIM = 128
HGRN_WIDTH = HGRN_HEADS * HGRN_HEAD_DIM
HGRN_CHUNK = 16
N_EXPERTS = 32
TOP_K = 4
D_EXPERT = 1024
SWIGLU_LIMIT = 7.0
SWIGLU_ALPHA = 1.702
MOE_BLOCK = 256
DN_ALPHA = (2 * DEPTH) ** 0.25
DN_BETA = (8 * DEPTH) ** -0.25
LN_EPS = 1e-5
IN_WIDTH = NSA_WIDTH + 6 * KV_WIDTH + 3 * NSA_HEADS + 4 * HGRN_WIDTH + 2 * D_MODEL

kernel_name = 'nsa_hgrn2_moe_deepnorm_step'


def layer_norm(x, g, b):
    xf = x.astype(jnp.float32)
    mu = jnp.mean(xf, axis=-1, keepdims=True)
    var = jnp.mean(jnp.square(xf - mu), axis=-1, keepdims=True)
    return ((xf - mu) * lax.rsqrt(var + LN_EPS) * g + b).astype(x.dtype)


def masked_softmax(s, mask):
    s = jnp.where(mask, s.astype(jnp.float32), -jnp.inf)
    m = jnp.max(s, axis=-1, keepdims=True)
    m = jnp.where(jnp.isfinite(m), m, 0.0)
    e = jnp.exp(s - m)
    return e / jnp.maximum(jnp.sum(e, axis=-1, keepdims=True), 1e-30)


def project(x, w_in):
    B, T, _ = x.shape
    p = jnp.einsum('btd,de->bte', x, w_in)
    sizes = (NSA_WIDTH,) + (KV_WIDTH,) * 6 + (3 * NSA_HEADS,) + (HGRN_WIDTH,) * 4 + (D_MODEL, D_MODEL)
    parts = jnp.split(p, [int(o) for o in np.cumsum(sizes)[:-1]], axis=-1)
    q = parts[0].reshape(B, T, NSA_KV_HEADS, NSA_GROUP, NSA_HEAD_DIM)
    kv = [a.reshape(B, T, NSA_KV_HEADS, NSA_HEAD_DIM) for a in parts[1:7]]
    g = jax.nn.sigmoid(parts[7]).reshape(B, T, 3, NSA_KV_HEADS, NSA_GROUP)
    return (q, g, *kv, *parts[8:])


def compress(rows, pe, w1, b1, w2, b2):
    B, T, G, hd = rows.shape
    c = rows.reshape(B, T // CMP_STRIDE, CMP_STRIDE, G, hd)
    first = jnp.einsum('bjpgd,pdh->bjgh', c, w1[:CMP_STRIDE])
    second = jnp.einsum('bjpgd,pdh->bjgh', c, w1[CMP_STRIDE:])
    bias = jnp.einsum('pd,pdh->h', pe, w1) + b1
    h = jax.nn.gelu(first[:, :-1] + second[:, 1:] + bias)
    return jnp.einsum('bngh,hd->bngd', h, w2) + b2


def sel_blocks(rows):
    B, T, G, hd = rows.shape
    return rows.reshape(B, T // SEL_BLOCK, SEL_BLOCK, G, hd).transpose(0, 3, 1, 2, 4)


def nsa_core(q, g, q_pos, kc, vc, ks_blk, vs_blk, kw, vw, w_pos):
    scale = NSA_HEAD_DIM ** -0.5
    B, Tq = q.shape[:2]
    nc = kc.shape[1]
    ns = ks_blk.shape[2]
    s = jnp.einsum('btgrd,bngd->btgrn', q, kc) * scale
    c_end = jnp.arange(nc) * CMP_STRIDE + (CMP_BLOCK - 1)
    c_ok = c_end[None, :] <= q_pos[:, None]
    p_cmp = masked_softmax(s, c_ok[None, :, None, None, :])
    o_cmp = jnp.einsum('btgrn,bngd->btgrd', p_cmp.astype(vc.dtype), vc)
    r = SEL_BLOCK // CMP_STRIDE
    imp = jnp.pad(jnp.sum(p_cmp, axis=3), ((0, 0), (0, 0), (0, 0), (1, 1)))
    p_slc = jnp.sum(imp[..., :r * ns].reshape(B, Tq, NSA_KV_HEADS, ns, r), axis=-1) + imp[..., r::r]
    blk = jnp.arange(ns)[None, :]
    cur = (q_pos // SEL_BLOCK)[:, None]
    s_ok = blk <= cur
    forced = (blk == 0) | (blk > cur - N_LOCAL_SEL)
    score = jnp.where(s_ok[None, :, None, :],
                      p_slc + jnp.where(forced, FORCE_BONUS, 0.0)[None, :, None, :], -FORCE_BONUS)
    _, idx = lax.top_k(score, min(SEL_TOP, ns))
    n_sel = idx.shape[-1]
    idx = idx.transpose(0, 2, 1, 3)
    take = jax.vmap(jax.vmap(lambda blocks, ix: blocks[ix]))
    flat = idx.reshape(B, NSA_KV_HEADS, Tq * n_sel)
    ks = take(ks_blk, flat).reshape(B, NSA_KV_HEADS, Tq, n_sel * SEL_BLOCK, NSA_HEAD_DIM)
    vs = take(vs_blk, flat).reshape(B, NSA_KV_HEADS, Tq, n_sel * SEL_BLOCK, NSA_HEAD_DIM)
    k_pos = (idx[..., None] * SEL_BLOCK + jnp.arange(SEL_BLOCK)).reshape(B, NSA_KV_HEADS, Tq, n_sel * SEL_BLOCK)
    sel_ok = (k_pos <= q_pos[None, None, :, None]).transpose(0, 2, 1, 3)[:, :, :, None, :]
    s = jnp.einsum('btgrd,bgtkd->btgrk', q, ks) * scale
    p = masked_softmax(s, sel_ok)
    o_sel = jnp.einsum('btgrk,bgtkd->btgrd', p.astype(vs.dtype), vs)
    w_ok = (w_pos[None, :] <= q_pos[:, None]) & (w_pos[None, :] > q_pos[:, None] - WINDOW) & (w_pos[None, :] >= 0)
    s = jnp.einsum('btgrd,bkgd->btgrk', q, kw) * scale
    p = masked_softmax(s, w_ok[None, :, None, None, :])
    o_win = jnp.einsum('btgrk,bkgd->btgrd', p.astype(vw.dtype), vw)
    return g[:, :, 0, ..., None] * o_cmp + g[:, :, 1, ..., None] * o_sel + g[:, :, 2, ..., None] * o_win


def hgrn2(hq, hf, hi, hg, s0, lb, norm_g):
    B, T, _ = hq.shape
    H, hd = HGRN_HEADS, HGRN_HEAD_DIM
    C = math.gcd(T, HGRN_CHUNK)
    n = T // C
    f = lb + (1.0 - lb) * jax.nn.sigmoid(hf.astype(jnp.float32))

    def chunks(a):
        return a.astype(jnp.float32).reshape(B, n, C, H, hd).transpose(0, 3, 1, 2, 4)

    q = chunks(jax.nn.silu(hq))
    k = chunks(1.0 - f)
    v = chunks(hi)
    b = jnp.cumsum(chunks(jnp.log(f)), axis=3)
    b_end = b[:, :, :, -1:, :]
    q_dec = q * jnp.exp(b)
    k_inv = k * jnp.exp(-b)
    k_end = k * jnp.exp(b_end - b)
    tri = jnp.tril(jnp.ones((C, C), bool))
    a = jnp.where(tri, jnp.einsum('bhncd,bhnsd->bhncs', q_dec, k_inv), 0.0)
    o_intra = jnp.einsum('bhncs,bhnsv->bhncv', a, v)

    def step(S, xs):
        q_c, k_c, v_c, d_c = xs
        o_c = jnp.einsum('bhcd,bhdv->bhcv', q_c, S)
        S = d_c[..., None] * S + jnp.einsum('bhcd,bhcv->bhdv', k_c, v_c)
        return S, o_c

    xs = (jnp.moveaxis(q_dec, 2, 0), jnp.moveaxis(k_end, 2, 0), jnp.moveaxis(v, 2, 0),
          jnp.moveaxis(jnp.exp(b_end[:, :, :, 0]), 2, 0))
    s_fin, o_inter = lax.scan(step, s0.astype(jnp.float32), xs)
    o = o_intra + jnp.moveaxis(o_inter, 0, 2)
    o = o * lax.rsqrt(jnp.mean(jnp.square(o), axis=-1, keepdims=True) + LN_EPS) * norm_g
    o = o.transpose(0, 2, 3, 1, 4).reshape(B, T, H * hd)
    return (o * jax.nn.silu(hg.astype(jnp.float32))).astype(hq.dtype), s_fin


def moe(x, w_router, b_router, w_gu, b_gu, w_down, b_down):
    B, T, D = x.shape
    N = B * T
    M = N * TOP_K
    xt = x.reshape(N, D)
    logits = jnp.einsum('nd,de->ne', xt, w_router).astype(jnp.float32) + b_router.astype(jnp.float32)
    top_logit, top_e = lax.top_k(logits, TOP_K)
    top_w = jax.nn.softmax(top_logit, axis=-1)
    e_flat = top_e.reshape(M)
    order = jnp.argsort(e_flat)
    e_sorted = e_flat[order]
    tok_sorted = (order // TOP_K).astype(jnp.int32)
    w_sorted = top_w.reshape(M)[order]
    counts = jnp.zeros((N_EXPERTS,), jnp.int32).at[e_flat].add(1)
    padded = (counts + MOE_BLOCK - 1) // MOE_BLOCK * MOE_BLOCK
    pad_end = jnp.cumsum(padded)
    pad_start = pad_end - padded
    start = jnp.cumsum(counts) - counts
    dest = pad_start[e_sorted] + (jnp.arange(M) - start[e_sorted])
    n_blocks = -(-(M + N_EXPERTS * (MOE_BLOCK - 1)) // MOE_BLOCK)
    slot_tok = jnp.full((n_blocks * MOE_BLOCK,), N, jnp.int32).at[dest].set(tok_sorted)
    x_pad = jnp.concatenate([xt, jnp.zeros((1, D), xt.dtype)], axis=0)
    xb = x_pad[slot_tok].reshape(n_blocks, MOE_BLOCK, D)
    blk_e = jnp.minimum(jnp.searchsorted(pad_end, jnp.arange(n_blocks) * MOE_BLOCK, side='right'), N_EXPERTS - 1)

    def expert_block(args):
        xb_i, e = args
        gu = xb_i @ w_gu[e] + b_gu[e]
        gate = jnp.minimum(gu[:, :D_EXPERT], SWIGLU_LIMIT)
        up = jnp.clip(gu[:, D_EXPERT:], -SWIGLU_LIMIT, SWIGLU_LIMIT)
        h = gate * jax.nn.sigmoid(SWIGLU_ALPHA * gate) * (up + 1.0)
        return h @ w_down[e] + b_down[e]

    yb = lax.map(expert_block, (xb, blk_e)).reshape(n_blocks * MOE_BLOCK, D)
    y = jnp.zeros((N, D), jnp.float32).at[tok_sorted].add(yb[dest].astype(jnp.float32) * w_sorted[:, None])
    return y.reshape(B, T, D).astype(x.dtype)


def mix_prompt(x, w_in, cmp_l, lb, norm_g):
    B, T, _ = x.shape
    q, g, kc, vc, ks, vs, kw, vw, hq, hf, hi, hg, ga, gb = project(x, w_in)
    pe, w1, b1, w2, b2 = cmp_l
    kc_sum = compress(kc, pe[0], w1[0], b1[0], w2[0], b2[0])
    vc_sum = compress(vc, pe[1], w1[1], b1[1], w2[1], b2[1])
    ks_blk, vs_blk = sel_blocks(ks), sel_blocks(vs)
    front = ((0, 0), (WINDOW, 0), (0, 0), (0, 0))
    kw_pad, vw_pad = jnp.pad(kw, front), jnp.pad(vw, front)

    def one_block(i):
        t0 = i * NSA_Q_BLOCK
        q_b = lax.dynamic_slice_in_dim(q, t0, NSA_Q_BLOCK, axis=1)
        g_b = lax.dynamic_slice_in_dim(g, t0, NSA_Q_BLOCK, axis=1)
        kw_b = lax.dynamic_slice_in_dim(kw_pad, t0, WINDOW + NSA_Q_BLOCK, axis=1)
        vw_b = lax.dynamic_slice_in_dim(vw_pad, t0, WINDOW + NSA_Q_BLOCK, axis=1)
        q_pos = t0 + jnp.arange(NSA_Q_BLOCK)
        w_pos = t0 - WINDOW + jnp.arange(WINDOW + NSA_Q_BLOCK)
        return nsa_core(q_b, g_b, q_pos, kc_sum, vc_sum, ks_blk, vs_blk, kw_b, vw_b, w_pos)

    o_nsa = lax.map(one_block, jnp.arange(T // NSA_Q_BLOCK))
    o_nsa = jnp.moveaxis(o_nsa, 0, 1).reshape(B, T, NSA_WIDTH)
    s0 = jnp.zeros((B, HGRN_HEADS, HGRN_HEAD_DIM, HGRN_HEAD_DIM), jnp.float32)
    o_hgrn, s_fin = hgrn2(hq, hf, hi, hg, s0, lb, norm_g)
    win = min(WINDOW, T)
    return o_nsa, o_hgrn, ga, gb, (kc, vc, ks, vs, kw[:, T - win:], vw[:, T - win:], s_fin.astype(x.dtype))


def mix_sample(x, ck_cmp, cv_cmp, ck_sel, cv_sel, ck_win, cv_win, s0, page_table, w_in, cmp_l, lb, norm_g):
    B, S, _ = x.shape
    past = page_table.shape[1] * PAGE_SIZE
    win_keep = ck_win.shape[1]
    q, g, kc, vc, ks, vs, kw, vw, hq, hf, hi, hg, ga, gb = project(x, w_in)
    pad = (-(past + S)) % SEL_BLOCK

    def full_rows(pool, new):
        old = pool[page_table].reshape(B, past, NSA_KV_HEADS, NSA_HEAD_DIM)
        zeros = jnp.zeros((B, pad, NSA_KV_HEADS, NSA_HEAD_DIM), new.dtype)
        return jnp.concatenate([old.astype(new.dtype), new, zeros], axis=1)

    pe, w1, b1, w2, b2 = cmp_l
    kc_sum = compress(full_rows(ck_cmp, kc), pe[0], w1[0], b1[0], w2[0], b2[0])
    vc_sum = compress(full_rows(cv_cmp, vc), pe[1], w1[1], b1[1], w2[1], b2[1])
    ks_blk = sel_blocks(full_rows(ck_sel, ks))
    vs_blk = sel_blocks(full_rows(cv_sel, vs))
    kw_band = jnp.concatenate([ck_win.astype(kw.dtype), kw], axis=1)
    vw_band = jnp.concatenate([cv_win.astype(vw.dtype), vw], axis=1)
    q_pos = past + jnp.arange(S)
    w_pos = past - win_keep + jnp.arange(win_keep + S)
    o_nsa = nsa_core(q, g, q_pos, kc_sum, vc_sum, ks_blk, vs_blk, kw_band, vw_band, w_pos).reshape(B, S, NSA_WIDTH)
    o_hgrn, s_fin = hgrn2(hq, hf, hi, hg, s0, lb, norm_g)
    return o_nsa, o_hgrn, ga, gb, (kc, vc, ks, vs, kw_band[:, -win_keep:], vw_band[:, -win_keep:], s_fin.astype(s0.dtype))


def finish(x, o_nsa, o_hgrn, ga, gb, w_ba, w_bb, w_out, ln1_g, ln1_b,
           w_router, b_router, w_gu, b_gu, w_down, b_down, ln2_g, ln2_b):
    mix = jax.nn.sigmoid(ga) * (o_nsa @ w_ba) + jax.nn.sigmoid(gb) * (o_hgrn @ w_bb)
    x1 = layer_norm(DN_ALPHA * x + mix @ w_out, ln1_g, ln1_b)
    return layer_norm(DN_ALPHA * x1 + moe(x1, w_router, b_router, w_gu, b_gu, w_down, b_down), ln2_g, ln2_b)


def stack_layers(states, i):
    return jnp.stack([st[i] for st in states], axis=0)


def setup_inputs(seed: int = 0) -> dict:
    key = jax.random.key(seed)
    ks = jax.random.split(key, 40)
    f32 = jnp.float32
    n_pages = PAST_LEN // PAGE_SIZE
    n_pool = (DEC_BATCH * n_pages * 5) // 4
    win_keep = min(WINDOW, PAST_LEN)
    hd = NSA_HEAD_DIM

    def nrm(k, shape, scale=1.0):
        return jax.random.normal(k, shape, f32) * scale

    page_shape = (DEPTH, n_pool, PAGE_SIZE, NSA_KV_HEADS, hd)
    win_shape = (DEPTH, DEC_BATCH, win_keep, NSA_KV_HEADS, hd)
    page_table = jax.random.permutation(ks[9], n_pool)[:DEC_BATCH * n_pages].reshape(DEC_BATCH, n_pages).astype(jnp.int32)
    return {
        'x_prompt': nrm(ks[0], (BATCH, SEQ, D_MODEL)),
        'x_sample': nrm(ks[1], (DEC_BATCH, DEC_SEQ, D_MODEL)),
        'cache_k_cmp': nrm(ks[2], page_shape),
        'cache_v_cmp': nrm(ks[3], page_shape),
        'cache_k_sel': nrm(ks[4], page_shape),
        'cache_v_sel': nrm(ks[5], page_shape),
        'cache_k_win': nrm(ks[6], win_shape),
        'cache_v_win': nrm(ks[7], win_shape),
        'state_hgrn': nrm(ks[8], (DEPTH, DEC_BATCH, HGRN_HEADS, HGRN_HEAD_DIM, HGRN_HEAD_DIM), 0.5),
        'page_table': page_table,
        'w_in': nrm(ks[10], (DEPTH, D_MODEL, IN_WIDTH), D_MODEL ** -0.5),
        'cmp_pe': nrm(ks[11], (DEPTH, 2, CMP_BLOCK, hd), 0.1),
        'cmp_w1': nrm(ks[12], (DEPTH, 2, CMP_BLOCK, hd, CMP_HIDDEN), (CMP_BLOCK * hd) ** -0.5),
        'cmp_b1': nrm(ks[13], (DEPTH, 2, CMP_HIDDEN), 0.01),
        'cmp_w2': nrm(ks[14], (DEPTH, 2, CMP_HIDDEN, hd), CMP_HIDDEN ** -0.5),
        'cmp_b2': nrm(ks[15], (DEPTH, 2, hd), 0.01),
        'hgrn_gamma': nrm(ks[16], (DEPTH + 1, HGRN_WIDTH), 0.1),
        'hgrn_norm': 1.0 + nrm(ks[17], (DEPTH, HGRN_HEAD_DIM), 0.01),
        'w_branch_a': nrm(ks[18], (DEPTH, NSA_WIDTH, D_MODEL), NSA_WIDTH ** -0.5 * DN_BETA),
        'w_branch_b': nrm(ks[19], (DEPTH, HGRN_WIDTH, D_MODEL), HGRN_WIDTH ** -0.5 * DN_BETA),
        'w_out': nrm(ks[20], (DEPTH, D_MODEL, D_MODEL), D_MODEL ** -0.5 * DN_BETA),
        'ln1_g': 1.0 + nrm(ks[21], (DEPTH, D_MODEL), 0.01),
        'ln1_b': nrm(ks[22], (DEPTH, D_MODEL), 0.01),
        'w_router': nrm(ks[23], (DEPTH, D_MODEL, N_EXPERTS), D_MODEL ** -0.5),
        'b_router': nrm(ks[24], (DEPTH, N_EXPERTS), 0.01),
        'w_gate_up': nrm(ks[25], (DEPTH, N_EXPERTS, D_MODEL, 2 * D_EXPERT), D_MODEL ** -0.5 * DN_BETA),
        'b_gate_up': nrm(ks[26], (DEPTH, N_EXPERTS, 2 * D_EXPERT), 0.01),
        'w_down': nrm(ks[27], (DEPTH, N_EXPERTS, D_EXPERT, D_MODEL), D_EXPERT ** -0.5 * DN_BETA),
        'b_down': nrm(ks[28], (DEPTH, N_EXPERTS, D_MODEL), 0.01),
        'ln2_g': 1.0 + nrm(ks[29], (DEPTH, D_MODEL), 0.01),
        'ln2_b': nrm(ks[30], (DEPTH, D_MODEL), 0.01),
    }


def reference(x_prompt, x_sample, cache_k_cmp, cache_v_cmp, cache_k_sel, cache_v_sel, cache_k_win, cache_v_win,
              state_hgrn, page_table, w_in, cmp_pe, cmp_w1, cmp_b1, cmp_w2, cmp_b2, hgrn_gamma, hgrn_norm,
              w_branch_a, w_branch_b, w_out, ln1_g, ln1_b, w_router, b_router, w_gate_up, b_gate_up,
              w_down, b_down, ln2_g, ln2_b):
    lower_bounds = jnp.cumsum(jax.nn.softmax(hgrn_gamma.astype(jnp.float32), axis=0), axis=0)
    xp, xs = x_prompt, x_sample
    new_p, new_s = [], []
    for l in range(DEPTH):
        cmp_l = (cmp_pe[l], cmp_w1[l], cmp_b1[l], cmp_w2[l], cmp_b2[l])
        ffn_l = (w_branch_a[l], w_branch_b[l], w_out[l], ln1_g[l], ln1_b[l], w_router[l], b_router[l],
                 w_gate_up[l], b_gate_up[l], w_down[l], b_down[l], ln2_g[l], ln2_b[l])
        o_a, o_b, ga, gb, st = mix_prompt(xp, w_in[l], cmp_l, lower_bounds[l], hgrn_norm[l])
        xp = finish(xp, o_a, o_b, ga, gb, *ffn_l)
        new_p.append(st)
        o_a, o_b, ga, gb, st = mix_sample(xs, cache_k_cmp[l], cache_v_cmp[l], cache_k_sel[l], cache_v_sel[l],
                                          cache_k_win[l], cache_v_win[l], state_hgrn[l], page_table, w_in[l],
                                          cmp_l, lower_bounds[l], hgrn_norm[l])
        xs = finish(xs, o_a, o_b, ga, gb, *ffn_l)
        new_s.append(st)
    return (xp, xs,
            stack_layers(new_p, 0), stack_layers(new_p, 1), stack_layers(new_p, 2), stack_layers(new_p, 3),
            stack_layers(new_p, 4), stack_layers(new_p, 5), stack_layers(new_p, 6),
            stack_layers(new_s, 0), stack_layers(new_s, 1), stack_layers(new_s, 2), stack_layers(new_s, 3),
            stack_layers(new_s, 4), stack_layers(new_s, 5), stack_layers(new_s, 6))
```

```python
import functools
import math

import jax
import jax.numpy as jnp
from jax import lax
from jax.experimental import pallas as pl
from jax.experimental.pallas import tpu as pltpu

F32 = jnp.float32
BF16 = jnp.bfloat16

D_MODEL = 1024
PAGE_SIZE = 128
NSA_HEADS = 8
NSA_KV_HEADS = 2
NSA_GROUP = NSA_HEADS // NSA_KV_HEADS
NSA_HEAD_DIM = 64
NSA_WIDTH = NSA_HEADS * NSA_HEAD_DIM
KV_WIDTH = NSA_KV_HEADS * NSA_HEAD_DIM
CMP_BLOCK = 32
CMP_STRIDE = 16
CMP_HIDDEN = 128
SEL_BLOCK = 64
SEL_TOP = 16
N_LOCAL_SEL = 2
FORCE_BONUS = 1.0e4
WINDOW = 512
HGRN_HEADS = 4
HGRN_HEAD_DIM = 128
HGRN_WIDTH = HGRN_HEADS * HGRN_HEAD_DIM
HGRN_CHUNK = 16
N_EXPERTS = 32
TOP_K = 4
D_EXPERT = 1024
SWIGLU_LIMIT = 7.0
SWIGLU_ALPHA = 1.702
DEPTH = 1
DN_ALPHA = (2 * DEPTH) ** 0.25
LN_EPS = 1e-5

LANES = 128
SUBLANES = 8
VMEM_BYTES_V7X = 64 * 1024 * 1024
VMEM_LIMIT = VMEM_BYTES_V7X * 3 // 4

Q_TILE = 128
K_TILE = 128
HGRN_TILE = 128
MOE_ROWS = 512
ROUTE_TILE = 256
NEG_INF = float("-inf")

_C_Q = 0
_C_KV = _C_Q + NSA_WIDTH
_C_H = _C_KV + 6 * KV_WIDTH
_C_GAB = _C_H + 4 * HGRN_WIDTH
_C_END = _C_GAB + 2 * D_MODEL
GATE_ROWS = 32


def _params(*sem):
    return pltpu.CompilerParams(dimension_semantics=sem, vmem_limit_bytes=VMEM_LIMIT)


def _nt_dot(a, b):
    return lax.dot_general(a, b, (((1,), (1,)), ((), ())), preferred_element_type=F32)


def _dot(a, b):
    return jnp.dot(a, b, preferred_element_type=F32)


def _proj_kernel(x_ref, w_ref, wg_ref, q_ref, kc_ref, vc_ref, ks_ref, vs_ref, kw_ref, vw_ref,
                 h_ref, gab_ref, gt_ref):
    x = x_ref[...].astype(BF16)
    q_ref[...] = _dot(x, w_ref[:, _C_Q:_C_KV])
    kv = _dot(x, w_ref[:, _C_KV:_C_H])
    for n, ref in enumerate((kc_ref, vc_ref, ks_ref, vs_ref, kw_ref, vw_ref)):
        ref[...] = kv[:, n * KV_WIDTH:(n + 1) * KV_WIDTH]
    h_ref[...] = _dot(x, w_ref[:, _C_H:_C_GAB])
    gab_ref[...] = _dot(x, w_ref[:, _C_GAB:_C_END])
    gt_ref[...] = jax.nn.sigmoid(_nt_dot(wg_ref[...], x))


def _project(x, w_main, w_gate_t, tm):
    n = x.shape[0]
    row = lambda w: pl.BlockSpec((tm, w), lambda i: (i, 0))
    full = lambda a: pl.BlockSpec(a.shape, lambda i: (0,) * a.ndim)
    out_shape = ([jax.ShapeDtypeStruct((n, NSA_WIDTH), F32)]
                 + [jax.ShapeDtypeStruct((n, KV_WIDTH), F32)] * 6
                 + [jax.ShapeDtypeStruct((n, 4 * HGRN_WIDTH), F32),
                    jax.ShapeDtypeStruct((n, 2 * D_MODEL), F32),
                    jax.ShapeDtypeStruct((GATE_ROWS, n), F32)])
    out_specs = ([row(NSA_WIDTH)] + [row(KV_WIDTH)] * 6 + [row(4 * HGRN_WIDTH), row(2 * D_MODEL),
                 pl.BlockSpec((GATE_ROWS, tm), lambda i: (0, i))])
    return pl.pallas_call(
        _proj_kernel, grid=(n // tm,),
        in_specs=[row(D_MODEL), full(w_main), full(w_gate_t)],
        out_specs=out_specs, out_shape=out_shape,
        compiler_params=_params("parallel"), name="in_proj")(x, w_main, w_gate_t)


def _gelu_tanh(x):
    return 0.5 * x * (1.0 + jnp.tanh(math.sqrt(2.0 / math.pi) * (x + 0.044715 * (x * x * x))))


def _compress_kernel(c_ref, w1_ref, bias_ref, w2_ref, b2_ref, o_ref):
    c = c_ref[0].astype(BF16)
    j = c.shape[0]
    f = _dot(c, w1_ref[...])
    outs = []
    for g in range(NSA_KV_HEADS):
        base = g * 2 * CMP_HIDDEN
        first = f[:, base:base + CMP_HIDDEN]
        second = f[:, base + CMP_HIDDEN:base + 2 * CMP_HIDDEN]
        nxt = pltpu.roll(second, j - 1, axis=0)
        h = _gelu_tanh(first + nxt + bias_ref[g:g + 1, :])
        outs.append(_dot(h.astype(BF16), w2_ref[...]) + b2_ref[...])
    o_ref[0] = jnp.concatenate(outs, axis=1)


def _compress(rows, w1full, bias, w2, b2):
    b, t, _ = rows.shape
    j = t // CMP_STRIDE
    c = rows.reshape(b, j, CMP_STRIDE * KV_WIDTH)
    full = lambda a: pl.BlockSpec(a.shape, lambda i: (0,) * a.ndim)
    return pl.pallas_call(
        _compress_kernel, grid=(b,),
        in_specs=[pl.BlockSpec((1, j, CMP_STRIDE * KV_WIDTH), lambda i: (i, 0, 0)),
                  full(w1full), full(bias), full(w2), full(b2)],
        out_specs=pl.BlockSpec((1, j, KV_WIDTH), lambda i: (i, 0, 0)),
        out_shape=jax.ShapeDtypeStruct((b, j, KV_WIDTH), F32),
        compiler_params=_params("parallel"), name="compress")(c, w1full, bias, w2, b2)


def _compress_weights(pe, w1, b1, w2, b2):
    hd, hid = NSA_HEAD_DIM, CMP_HIDDEN
    halves = w1.reshape(2, CMP_STRIDE, hd, hid)
    eye = jnp.eye(NSA_KV_HEADS, dtype=w1.dtype)
    w1full = jnp.einsum('apdh,kg->pkdgah', halves, eye).reshape(
        CMP_STRIDE * KV_WIDTH, NSA_KV_HEADS * 2 * hid).astype(BF16)
    bias = jnp.einsum('pd,pdh->h', pe, w1, precision=lax.Precision.HIGHEST) + b1
    bias = jnp.broadcast_to(bias[None, :], (SUBLANES, hid))
    return w1full, bias, w2.astype(BF16), b2.reshape(1, hd)


def _nsa_prompt_kernel(q_ref, gt_ref, kc_ref, vc_ref, ks_ref, vs_ref, kw_ref, vw_ref, o_ref,
                       imp_ref, score_ref, sel_ref, m_ref, l_ref, acc_ref):
    g = pl.program_id(1)
    i = pl.program_id(2)
    t0 = i * Q_TILE
    rows = NSA_GROUP * Q_TILE
    nc = kc_ref.shape[1]
    ns = score_ref.shape[0]
    scale = NSA_HEAD_DIM ** -0.5

    q = q_ref[0] * scale
    lane_head = lax.broadcasted_iota(jnp.int32, (Q_TILE, KV_WIDTH), 1) // NSA_HEAD_DIM
    parts = []
    for r in range(NSA_GROUP):
        qr = q[:, r * NSA_HEAD_DIM:(r + 1) * NSA_HEAD_DIM]
        parts.append(jnp.where(lane_head == g, jnp.concatenate([qr, qr], axis=1), 0.0))
    qs = jnp.concatenate(parts, axis=0).astype(BF16)

    tok = t0 + lax.broadcasted_iota(jnp.int32, (1, Q_TILE), 1)
    tok4 = jnp.concatenate([tok] * NSA_GROUP, axis=1)

    s = _nt_dot(kc_ref[0].astype(BF16), qs)
    n_idx = lax.broadcasted_iota(jnp.int32, (nc, rows), 0)
    ok = (n_idx * CMP_STRIDE + (CMP_BLOCK - 1) <= tok4) & (n_idx < nc - 1)
    s = jnp.where(ok, s, NEG_INF)
    m = jnp.max(s, axis=0, keepdims=True)
    m = jnp.where(m > NEG_INF, m, 0.0)
    e = jnp.exp(s - m)
    p = e / jnp.maximum(jnp.sum(e, axis=0, keepdims=True), 1e-30)
    o_cmp = _dot(vc_ref[0].T.astype(BF16), p.astype(BF16))

    imp = p[:, 0:Q_TILE]
    for r in range(1, NSA_GROUP):
        imp = imp + p[:, r * Q_TILE:(r + 1) * Q_TILE]
    imp_ref[0:SUBLANES, :] = jnp.zeros((SUBLANES, Q_TILE), F32)
    imp_ref[SUBLANES:SUBLANES + nc, :] = imp
    ratio = SEL_BLOCK // CMP_STRIDE
    p_slc = imp_ref[pl.ds(SUBLANES - 1, ns, stride=ratio), :]
    for d in range(ratio):
        p_slc = p_slc + imp_ref[pl.ds(SUBLANES + d, ns, stride=ratio), :]
    blk = lax.broadcasted_iota(jnp.int32, (ns, Q_TILE), 0)
    cur = tok // SEL_BLOCK
    forced = (blk == 0) | (blk > cur - N_LOCAL_SEL)
    score = jnp.where(blk <= cur, p_slc + jnp.where(forced, FORCE_BONUS, 0.0), -FORCE_BONUS)
    score_ref[...] = score

    def rank_step(jp, rank):
        row = score_ref[pl.ds(jp, 1), :]
        beats = (row > score) | ((row == score) & (jp < blk))
        return rank + beats.astype(F32)
    rank = lax.fori_loop(0, ns, rank_step, jnp.zeros((ns, Q_TILE), F32), unroll=8)
    sel_ref[...] = (rank < float(min(SEL_TOP, ns))).astype(F32)

    key_sub = lax.broadcasted_iota(jnp.int32, (K_TILE, rows), 0)

    def attend(k_ref, v_ref, kt, ok_fn):
        k0 = pl.multiple_of(kt * K_TILE, K_TILE)
        sc = _nt_dot(k_ref[0, pl.ds(k0, K_TILE), :].astype(BF16), qs)
        sc = jnp.where(ok_fn(kt, k0 + key_sub), sc, NEG_INF)
        m_new = jnp.maximum(m_ref[...], jnp.max(sc, axis=0, keepdims=True))
        alpha = jnp.exp(m_ref[...] - m_new)
        pp = jnp.exp(sc - m_new)
        l_ref[...] = alpha * l_ref[...] + jnp.sum(pp, axis=0, keepdims=True)
        vt = v_ref[0, pl.ds(k0, K_TILE), :].T.astype(BF16)
        acc_ref[...] = alpha * acc_ref[...] + _dot(vt, pp.astype(BF16))
        m_ref[...] = m_new

    def reset():
        m_ref[...] = jnp.full(m_ref.shape, NEG_INF, F32)
        l_ref[...] = jnp.zeros(l_ref.shape, F32)
        acc_ref[...] = jnp.zeros(acc_ref.shape, F32)

    def sel_ok(kt, key_pos):
        two = K_TILE // SEL_BLOCK
        lo = sel_ref[pl.ds(kt * two, 1), :]
        hi = sel_ref[pl.ds(kt * two + 1, 1), :]
        lo4 = jnp.concatenate([lo] * NSA_GROUP, axis=1)
        hi4 = jnp.concatenate([hi] * NSA_GROUP, axis=1)
        chosen = jnp.where(key_sub < SEL_BLOCK, lo4, hi4) > 0.5
        return chosen & (key_pos <= tok4)

    reset()

    def sel_step(n, carry):
        attend(ks_ref, vs_ref, i - n, sel_ok)
        return carry
    lax.fori_loop(0, i + 1, sel_step, 0)
    o_sel = acc_ref[...] / l_ref[...]

    def win_ok(kt, key_pos):
        return (key_pos <= tok4) & (key_pos > tok4 - WINDOW)

    reset()

    def win_step(n, carry):
        attend(kw_ref, vw_ref, i - n, win_ok)
        return carry
    lax.fori_loop(0, jnp.minimum(i, WINDOW // K_TILE) + 1, win_step, 0)
    o_win = acc_ref[...] / l_ref[...]

    def gate(branch):
        gr = gt_ref[pl.ds(branch * NSA_HEADS + g * NSA_GROUP, NSA_GROUP), :]
        return jnp.concatenate([gr[r:r + 1, :] for r in range(NSA_GROUP)], axis=1)
    o_t = gate(0) * o_cmp + gate(1) * o_sel + gate(2) * o_win
    outs = []
    for r in range(NSA_GROUP):
        blk_t = o_t[:, r * Q_TILE:(r + 1) * Q_TILE].T
        outs.append(jnp.where(g == 0, blk_t[:, :NSA_HEAD_DIM], blk_t[:, NSA_HEAD_DIM:]))
    o_ref[0] = jnp.concatenate(outs, axis=1)


def _nsa_prompt(q, gt, kc_sum, vc_sum, ks, vs, kw, vw):
    b, t, _ = q.shape
    nt = t // Q_TILE
    nc = kc_sum.shape[1]
    ns = t // SEL_BLOCK
    rows = NSA_GROUP * Q_TILE
    per_b = lambda a: pl.BlockSpec((1,) + a.shape[1:], lambda bi, g, i: (bi, 0, 0))
    return pl.pallas_call(
        _nsa_prompt_kernel, grid=(b, NSA_KV_HEADS, nt),
        in_specs=[pl.BlockSpec((1, Q_TILE, NSA_WIDTH // NSA_KV_HEADS), lambda bi, g, i: (bi, i, g)),
                  pl.BlockSpec((GATE_ROWS, Q_TILE), lambda bi, g, i: (0, bi * nt + i)),
                  per_b(kc_sum), per_b(vc_sum), per_b(ks), per_b(vs), per_b(kw), per_b(vw)],
        out_specs=pl.BlockSpec((1, Q_TILE, NSA_WIDTH // NSA_KV_HEADS), lambda bi, g, i: (bi, i, g)),
        out_shape=jax.ShapeDtypeStruct((b, t, NSA_WIDTH), F32),
        scratch_shapes=[pltpu.VMEM((SUBLANES + nc, Q_TILE), F32),
                        pltpu.VMEM((ns, Q_TILE), F32),
                        pltpu.VMEM((ns, Q_TILE), F32),
                        pltpu.VMEM((1, rows), F32),
                        pltpu.VMEM((1, rows), F32),
                        pltpu.VMEM((KV_WIDTH, rows), F32)],
        compiler_params=_params("parallel", "arbitrary", "arbitrary"),
        name="nsa_prompt")(q, gt, kc_sum, vc_sum, ks, vs, kw, vw)


def _hgrn_prompt_kernel(hq_ref, hf_ref, hi_ref, hg_ref, lb_ref, ng_ref, o_ref, st_out_ref, st_ref):
    c = pl.program_id(2)
    n = HGRN_TILE
    sub = HGRN_CHUNK

    @pl.when(c == 0)
    def _():
        st_ref[...] = jnp.zeros(st_ref.shape, F32)

    lb = lb_ref[...]
    f = lb + (1.0 - lb) * jax.nn.sigmoid(hf_ref[0])
    logf = jnp.log(f)
    pos = lax.broadcasted_iota(jnp.int32, (n, HGRN_HEAD_DIM), 0) % sub
    b = logf
    suf = logf
    sh = 1
    while sh < sub:
        b = b + jnp.where(pos >= sh, pltpu.roll(b, sh, axis=0), 0.0)
        suf = suf + jnp.where(pos + sh < sub, pltpu.roll(suf, n - sh, axis=0), 0.0)
        sh *= 2
    hq = hq_ref[0]
    k = 1.0 - f
    q_dec = (hq * jax.nn.sigmoid(hq) * jnp.exp(b)).astype(BF16)
    k_inv = (k * jnp.exp(-b)).astype(BF16)
    k_end = (k * jnp.exp(suf - logf)).astype(BF16)
    v = hi_ref[0]
    v16 = v.astype(BF16)

    rc = lax.broadcasted_iota(jnp.int32, (n, n), 0)
    cc = lax.broadcasted_iota(jnp.int32, (n, n), 1)
    a = jnp.where((rc // sub == cc // sub) & (cc <= rc), _nt_dot(q_dec, k_inv), 0.0)
    o = _dot(a.astype(BF16), v16)

    vt = v.T
    tok_chunk = lax.broadcasted_iota(jnp.int32, (HGRN_HEAD_DIM, n), 1) // sub
    st = st_ref[...]
    inter = []
    for ci in range(n // sub):
        inter.append(_nt_dot(q_dec[ci * sub:(ci + 1) * sub, :], st.astype(BF16)))
        kv_t = _dot(jnp.where(tok_chunk == ci, vt, 0.0).astype(BF16), k_end)
        st = st * jnp.exp(suf[ci * sub:ci * sub + 1, :]) + kv_t
    st_ref[...] = st
    o = o + jnp.concatenate(inter, axis=0)
    o = o * lax.rsqrt(jnp.mean(o * o, axis=-1, keepdims=True) + LN_EPS) * ng_ref[...]
    hg = hg_ref[0]
    o_ref[0] = o * (hg * jax.nn.sigmoid(hg))

    @pl.when(c == pl.num_programs(2) - 1)
    def _():
        st_out_ref[0, 0] = st.T


def _hgrn_prompt(h4, lb, norm_g, b, t):
    nchunk = t // HGRN_TILE
    part = lambda p: pl.BlockSpec((1, HGRN_TILE, HGRN_HEAD_DIM),
                                  lambda bi, h, c: (bi, c, p * HGRN_HEADS + h))
    return pl.pallas_call(
        _hgrn_prompt_kernel, grid=(b, HGRN_HEADS, nchunk),
        in_specs=[part(0), part(1), part(2), part(3),
                  pl.BlockSpec((1, HGRN_HEAD_DIM), lambda bi, h, c: (0, h)),
                  pl.BlockSpec((1, HGRN_HEAD_DIM), lambda bi, h, c: (0, 0))],
        out_specs=[pl.BlockSpec((1, HGRN_TILE, HGRN_HEAD_DIM), lambda bi, h, c: (bi, c, h)),
                   pl.BlockSpec((1, 1, HGRN_HEAD_DIM, HGRN_HEAD_DIM), lambda bi, h, c: (bi, h, 0, 0))],
        out_shape=[jax.ShapeDtypeStruct((b, t, HGRN_WIDTH), F32),
                   jax.ShapeDtypeStruct((b, HGRN_HEADS, HGRN_HEAD_DIM, HGRN_HEAD_DIM), F32)],
        scratch_shapes=[pltpu.VMEM((HGRN_HEAD_DIM, HGRN_HEAD_DIM), F32)],
        compiler_params=_params("parallel", "parallel", "arbitrary"),
        name="hgrn_prompt")(h4, h4, h4, h4, lb, norm_g)


def _layer_norm(y, g, b):
    mu = jnp.mean(y, axis=-1, keepdims=True)
    d = y - mu
    var = jnp.mean(d * d, axis=-1, keepdims=True)
    return d * lax.rsqrt(var + LN_EPS) * g + b


def _finish_kernel(x_ref, oa_ref, ob_ref, gab_ref, wba_ref, wbb_ref, wout_ref, g_ref, b_ref,
                   wr_hi_ref, wr_lo_ref, br_ref, x1_ref, x1b_ref, lt_ref):
    a = _dot(oa_ref[...].astype(BF16), wba_ref[...])
    bb = _dot(ob_ref[...].astype(BF16), wbb_ref[...])
    mix = jax.nn.sigmoid(gab_ref[:, :D_MODEL]) * a + jax.nn.sigmoid(gab_ref[:, D_MODEL:]) * bb
    y = DN_ALPHA * x_ref[...] + _dot(mix.astype(BF16), wout_ref[...])
    x1 = _layer_norm(y, g_ref[...], b_ref[...])
    x1_ref[...] = x1
    hi = x1.astype(BF16)
    x1b_ref[...] = hi
    lo = (x1 - hi.astype(F32)).astype(BF16)
    lt_ref[...] = (_nt_dot(wr_hi_ref[...], hi) + _nt_dot(wr_hi_ref[...], lo)
                   + _nt_dot(wr_lo_ref[...], hi) + br_ref[...])


def _finish(x, oa, ob, gab, wba, wbb, wout, g1, b1, wr_hi, wr_lo, br, tm):
    n = x.shape[0]
    row = lambda w: pl.BlockSpec((tm, w), lambda i: (i, 0))
    full = lambda a: pl.BlockSpec(a.shape, lambda i: (0,) * a.ndim)
    return pl.pallas_call(
        _finish_kernel, grid=(n // tm,),
        in_specs=[row(D_MODEL), row(NSA_WIDTH), row(HGRN_WIDTH), row(2 * D_MODEL),
                  full(wba), full(wbb), full(wout), full(g1), full(b1), full(wr_hi), full(wr_lo), full(br)],
        out_specs=[row(D_MODEL), row(D_MODEL), pl.BlockSpec((N_EXPERTS, tm), lambda i: (0, i))],
        out_shape=[jax.ShapeDtypeStruct((n, D_MODEL), F32), jax.ShapeDtypeStruct((n, D_MODEL), BF16),
                   jax.ShapeDtypeStruct((N_EXPERTS, n), F32)],
        compiler_params=_params("parallel"), name="merge_ln_router")(
            x, oa, ob, gab, wba, wbb, wout, g1, b1, wr_hi, wr_lo, br)


def _route_kernel(lt_ref, tri_ref, e_ref, w_ref, pos_ref, cnt_ref, carry_ref):
    i = pl.program_id(0)

    @pl.when(i == 0)
    def _():
        carry_ref[...] = jnp.zeros(carry_ref.shape, F32)

    logit = lt_ref[...]
    tn = logit.shape[1]
    eid = lax.broadcasted_iota(jnp.int32, (N_EXPERTS, tn), 0)
    rank = jnp.zeros((N_EXPERTS, tn), F32)
    for ep in range(N_EXPERTS):
        row = logit[ep:ep + 1, :]
        rank = rank + ((row > logit) | ((row == logit) & (ep < eid))).astype(F32)
    sel = rank < float(TOP_K)
    top = jnp.max(logit, axis=0, keepdims=True)
    ex = jnp.where(sel, jnp.exp(logit - top), 0.0)
    wgt = ex / jnp.sum(ex, axis=0, keepdims=True)
    self = sel.astype(F32)
    incl = _dot(self.astype(BF16), tri_ref[...])
    pos = carry_ref[:, 0:1] + incl - self
    carry_ref[...] = carry_ref[...] + jnp.sum(self, axis=1, keepdims=True)
    eid_f = eid.astype(F32)
    for kk in range(TOP_K):
        pick = sel & (rank == float(kk))
        e_ref[kk:kk + 1, :] = jnp.sum(jnp.where(pick, eid_f, 0.0), axis=0, keepdims=True).astype(jnp.int32)
        w_ref[kk:kk + 1, :] = jnp.sum(jnp.where(pick, wgt, 0.0), axis=0, keepdims=True)
        pos_ref[kk:kk + 1, :] = jnp.sum(jnp.where(pick, pos, 0.0), axis=0, keepdims=True).astype(jnp.int32)
    cnt_ref[...] = carry_ref[...]


def _route(logit_t):
    n = logit_t.shape[1]
    tn = ROUTE_TILE if n % ROUTE_TILE == 0 else n
    tri = (lax.broadcasted_iota(jnp.int32, (tn, tn), 0) <= lax.broadcasted_iota(jnp.int32, (tn, tn), 1)).astype(BF16)
    col = lambda r: pl.BlockSpec((r, tn), lambda i: (0, i))
    return pl.pallas_call(
        _route_kernel, grid=(n // tn,),
        in_specs=[col(N_EXPERTS), pl.BlockSpec((tn, tn), lambda i: (0, 0))],
        out_specs=[col(TOP_K), col(TOP_K), col(TOP_K), pl.BlockSpec((N_EXPERTS, LANES), lambda i: (0, 0))],
        out_shape=[jax.ShapeDtypeStruct((TOP_K, n), jnp.int32), jax.ShapeDtypeStruct((TOP_K, n), F32),
                   jax.ShapeDtypeStruct((TOP_K, n), jnp.int32), jax.ShapeDtypeStruct((N_EXPERTS, LANES), F32)],
        scratch_shapes=[pltpu.VMEM((N_EXPERTS, LANES), F32)],
        compiler_params=_params("arbitrary"), name="route")(logit_t, tri)


def _expert_kernel(blk_e_ref, nblk_ref, xb_ref, wgu_ref, bgu_ref, wd_ref, bd_ref, y_ref, wgu_b, wd_b):
    i = pl.program_id(0)
    live = i < nblk_ref[0]

    @pl.when(live & ((i == 0) | (blk_e_ref[i] != blk_e_ref[jnp.maximum(i - 1, 0)])))
    def _():
        wgu_b[...] = wgu_ref[0].astype(BF16)
        wd_b[...] = wd_ref[0].astype(BF16)

    @pl.when(live)
    def _():
        gu = _dot(xb_ref[...], wgu_b[...]) + bgu_ref[0]
        gate = jnp.minimum(gu[:, :D_EXPERT], SWIGLU_LIMIT)
        up = jnp.clip(gu[:, D_EXPERT:], -SWIGLU_LIMIT, SWIGLU_LIMIT)
        h = gate * jax.nn.sigmoid(SWIGLU_ALPHA * gate) * (up + 1.0)
        y_ref[...] = _dot(h.astype(BF16), wd_b[...]) + bd_ref[0]

    @pl.when(i >= nblk_ref[0])
    def _():
        y_ref[...] = jnp.zeros(y_ref.shape, F32)


def _experts(xb, blk_e, nblk, w_gu, b_gu, w_down, b_down):
    m = xb.shape[0]
    grid_spec = pltpu.PrefetchScalarGridSpec(
        num_scalar_prefetch=2, grid=(m // MOE_ROWS,),
        in_specs=[pl.BlockSpec((MOE_ROWS, D_MODEL), lambda i, be, nb: (i, 0)),
                  pl.BlockSpec((1, D_MODEL, 2 * D_EXPERT), lambda i, be, nb: (be[i], 0, 0)),
                  pl.BlockSpec((1, 1, 2 * D_EXPERT), lambda i, be, nb: (be[i], 0, 0)),
                  pl.BlockSpec((1, D_EXPERT, D_MODEL), lambda i, be, nb: (be[i], 0, 0)),
                  pl.BlockSpec((1, 1, D_MODEL), lambda i, be, nb: (be[i], 0, 0))],
        out_specs=pl.BlockSpec((MOE_ROWS, D_MODEL), lambda i, be, nb: (i, 0)),
        scratch_shapes=[pltpu.VMEM((D_MODEL, 2 * D_EXPERT), BF16), pltpu.VMEM((D_EXPERT, D_MODEL), BF16)])
    return pl.pallas_call(
        _expert_kernel, grid_spec=grid_spec,
        out_shape=jax.ShapeDtypeStruct((m, D_MODEL), F32),
        compiler_params=_params("arbitrary"), name="experts")(
            blk_e, nblk, xb, w_gu, b_gu.reshape(N_EXPERTS, 1, 2 * D_EXPERT),
            w_down, b_down.reshape(N_EXPERTS, 1, D_MODEL))


def _final_ln_kernel(x1_ref, y_ref, g_ref, b_ref, o_ref):
    o_ref[...] = _layer_norm(DN_ALPHA * x1_ref[...] + y_ref[...], g_ref[...], b_ref[...])


def _final_ln(x1, y, g2, b2, tm):
    n = x1.shape[0]
    row = pl.BlockSpec((tm, D_MODEL), lambda i: (i, 0))
    vec = pl.BlockSpec((1, D_MODEL), lambda i: (0, 0))
    return pl.pallas_call(
        _final_ln_kernel, grid=(n // tm,), in_specs=[row, row, vec, vec], out_specs=row,
        out_shape=jax.ShapeDtypeStruct((n, D_MODEL), F32),
        compiler_params=_params("parallel"), name="final_ln")(x1, y, g2, b2)


def _moe_and_norm(x1, x1b, logit_t, w_gu, b_gu, w_down, b_down, g2, b2, tm):
    n = x1.shape[0]
    top_e, top_w, top_pos, counts = _route(logit_t)
    counts = counts[:, 0].astype(jnp.int32)
    padded = (counts + MOE_ROWS - 1) // MOE_ROWS * MOE_ROWS
    pad_end = jnp.cumsum(padded)
    pad_start = pad_end - padded
    n_blocks = -(-(n * TOP_K + N_EXPERTS * (MOE_ROWS - 1)) // MOE_ROWS)
    dest = pad_start[top_e] + top_pos
    tok = jnp.broadcast_to(jnp.arange(n, dtype=jnp.int32)[None, :], dest.shape)
    slot_tok = jnp.zeros((n_blocks * MOE_ROWS,), jnp.int32).at[dest.reshape(-1)].set(tok.reshape(-1))
    xb = x1b[slot_tok]
    blk_e = jnp.minimum(jnp.searchsorted(pad_end, jnp.arange(n_blocks, dtype=jnp.int32) * MOE_ROWS, side='right'),
                        N_EXPERTS - 1).astype(jnp.int32)
    nblk = (pad_end[-1:] // MOE_ROWS).astype(jnp.int32)
    yb = _experts(xb, blk_e, nblk, w_gu, b_gu, w_down, b_down)
    y = jnp.zeros((n, D_MODEL), F32)
    for kk in range(TOP_K):
        y = y + yb[dest[kk]] * top_w[kk][:, None]
    return _final_ln(x1, y, g2, b2, tm)


def _masked_softmax(s, mask):
    s = jnp.where(mask, s.astype(F32), -jnp.inf)
    m = jnp.max(s, axis=-1, keepdims=True)
    m = jnp.where(jnp.isfinite(m), m, 0.0)
    e = jnp.exp(s - m)
    return e / jnp.maximum(jnp.sum(e, axis=-1, keepdims=True), 1e-30)


def _nsa_sample(q, g, kc_sum, vc_sum, ks_rows, vs_rows, kw, vw, past):
    scale = NSA_HEAD_DIM ** -0.5
    b = q.shape[0]
    nc = kc_sum.shape[1]
    ns = ks_rows.shape[1] // SEL_BLOCK
    q_pos = past
    s = jnp.einsum('bgrd,bngd->bgrn', q, kc_sum) * scale
    c_ok = (jnp.arange(nc) * CMP_STRIDE + (CMP_BLOCK - 1)) <= q_pos
    p_cmp = _masked_softmax(s, c_ok[None, None, None, :])
    o_cmp = jnp.einsum('bgrn,bngd->bgrd', p_cmp, vc_sum)
    r = SEL_BLOCK // CMP_STRIDE
    imp = jnp.pad(jnp.sum(p_cmp, axis=2), ((0, 0), (0, 0), (1, 1)))
    p_slc = jnp.sum(imp[..., :r * ns].reshape(b, NSA_KV_HEADS, ns, r), axis=-1) + imp[..., r::r]
    blk = jnp.arange(ns)
    cur = q_pos // SEL_BLOCK
    forced = (blk == 0) | (blk > cur - N_LOCAL_SEL)
    score = jnp.where(blk <= cur, p_slc + jnp.where(forced, FORCE_BONUS, 0.0), -FORCE_BONUS)
    _, idx = lax.top_k(score, min(SEL_TOP, ns))
    ks_blk = ks_rows.reshape(b, ns, SEL_BLOCK, NSA_KV_HEADS, NSA_HEAD_DIM).transpose(0, 3, 1, 2, 4)
    vs_blk = vs_rows.reshape(b, ns, SEL_BLOCK, NSA_KV_HEADS, NSA_HEAD_DIM).transpose(0, 3, 1, 2, 4)
    take = jax.vmap(jax.vmap(lambda blocks, ix: blocks[ix]))
    ks = take(ks_blk, idx).reshape(b, NSA_KV_HEADS, -1, NSA_HEAD_DIM)
    vs = take(vs_blk, idx).reshape(b, NSA_KV_HEADS, -1, NSA_HEAD_DIM)
    k_pos = (idx[..., None] * SEL_BLOCK + jnp.arange(SEL_BLOCK)).reshape(b, NSA_KV_HEADS, 1, -1)
    s = jnp.einsum('bgrd,bgkd->bgrk', q, ks) * scale
    p = _masked_softmax(s, k_pos <= q_pos)
    o_sel = jnp.einsum('bgrk,bgkd->bgrd', p, vs)
    kwn = kw.shape[1]
    w_pos = past - (kwn - 1) + jnp.arange(kwn)
    w_ok = (w_pos <= q_pos) & (w_pos > q_pos - WINDOW) & (w_pos >= 0)
    s = jnp.einsum('bgrd,bkgd->bgrk', q, kw) * scale
    p = _masked_softmax(s, w_ok[None, None, None, :])
    o_win = jnp.einsum('bgrk,bkgd->bgrd', p, vw)
    return g[:, 0, ..., None] * o_cmp + g[:, 1, ..., None] * o_sel + g[:, 2, ..., None] * o_win


def _hgrn_sample(hq, hf, hi, hg, s0, lb, norm_g):
    b = hq.shape[0]
    shp = (b, HGRN_HEADS, HGRN_HEAD_DIM)
    f = (lb + (1.0 - lb) * jax.nn.sigmoid(hf)).reshape(shp)
    q = jax.nn.silu(hq).reshape(shp)
    k = 1.0 - f
    v = hi.reshape(shp)
    s_new = f[..., None] * s0 + k[..., None] * v[:, :, None, :]
    o = jnp.einsum('bhd,bhdv->bhv', q, s_new, precision=lax.Precision.HIGHEST)
    o = o * lax.rsqrt(jnp.mean(jnp.square(o), axis=-1, keepdims=True) + LN_EPS) * norm_g
    return o.reshape(b, HGRN_WIDTH) * jax.nn.silu(hg), s_new


def kernel(x_prompt, x_sample, cache_k_cmp, cache_v_cmp, cache_k_sel, cache_v_sel, cache_k_win, cache_v_win, state_hgrn, page_table, w_in, cmp_pe, cmp_w1, cmp_b1, cmp_w2, cmp_b2, hgrn_gamma, hgrn_norm, w_branch_a, w_branch_b, w_out, ln1_g, ln1_b, w_router, b_router, w_gate_up, b_gate_up, w_down, b_down, ln2_g, ln2_b):
    l = 0
    bp, t, _ = x_prompt.shape
    bs = x_sample.shape[0]
    past = page_table.shape[1] * PAGE_SIZE
    win_keep = cache_k_win.shape[2]

    lower = jnp.cumsum(jax.nn.softmax(hgrn_gamma.astype(F32), axis=0), axis=0)[l][None, :]
    norm_g = hgrn_norm[l][None, :]
    w = w_in[l]
    c_g = NSA_WIDTH + 6 * KV_WIDTH
    n_g = 3 * NSA_HEADS
    w_main = jnp.concatenate([w[:, :c_g], w[:, c_g + n_g:]], axis=1).astype(BF16)
    w_gate_t = jnp.pad(w[:, c_g:c_g + n_g].T, ((0, GATE_ROWS - n_g), (0, 0))).astype(BF16)
    cw = [_compress_weights(cmp_pe[l, n], cmp_w1[l, n], cmp_b1[l, n], cmp_w2[l, n], cmp_b2[l, n]) for n in range(2)]
    wba = w_branch_a[l].astype(BF16)
    wbb = w_branch_b[l].astype(BF16)
    wout = w_out[l].astype(BF16)
    wr_t = w_router[l].T
    wr_hi = wr_t.astype(BF16)
    wr_lo = (wr_t - wr_hi.astype(F32)).astype(BF16)
    br = jnp.broadcast_to(b_router[l].astype(F32)[:, None], (N_EXPERTS, 1))
    g1, b1 = ln1_g[l][None, :], ln1_b[l][None, :]
    g2, b2 = ln2_g[l][None, :], ln2_b[l][None, :]

    def finish(x2d, o_a, o_b, gab, tm):
        x1, x1b, logit_t = _finish(x2d, o_a, o_b, gab, wba, wbb, wout, g1, b1, wr_hi, wr_lo, br, tm)
        return _moe_and_norm(x1, x1b, logit_t, w_gate_up[l], b_gate_up[l], w_down[l], b_down[l], g2, b2, tm)

    n = bp * t
    xp = x_prompt.reshape(n, D_MODEL)
    q, kc, vc, ks, vs, kw, vw, h4, gab, gt = _project(xp, w_main, w_gate_t, 256)
    r3 = lambda a: a.reshape(bp, t, a.shape[-1])
    kc_sum = _compress(r3(kc), *cw[0])
    vc_sum = _compress(r3(vc), *cw[1])
    o_nsa = _nsa_prompt(r3(q), gt, kc_sum, vc_sum, r3(ks), r3(vs), r3(kw), r3(vw))
    o_hgrn, p_state = _hgrn_prompt(r3(h4), lower, norm_g, bp, t)
    y_prompt = finish(xp, o_nsa.reshape(n, NSA_WIDTH), o_hgrn.reshape(n, HGRN_WIDTH), gab, 256).reshape(bp, t, D_MODEL)
    r5 = lambda a: a.reshape(1, bp, t, NSA_KV_HEADS, NSA_HEAD_DIM)
    win = min(WINDOW, t)
    new_p = (r5(kc), r5(vc), r5(ks), r5(vs), r5(kw)[:, :, t - win:], r5(vw)[:, :, t - win:], p_state[None])

    xs = x_sample.reshape(bs, D_MODEL)
    q, kc, vc, ks, vs, kw, vw, h4, gab, gt = _project(xs, w_main, w_gate_t, bs)
    pad = (-(past + 1)) % (SUBLANES * CMP_STRIDE)

    def full_rows(pool, new):
        old = pool[l][page_table].reshape(bs, past, KV_WIDTH)
        return jnp.concatenate([old, new[:, None, :], jnp.zeros((bs, pad, KV_WIDTH), F32)], axis=1)

    sel_len = past + 1 + (-(past + 1)) % SEL_BLOCK
    nc = sel_len // CMP_STRIDE - 1
    kc_sum = _compress(full_rows(cache_k_cmp, kc), *cw[0])[:, :nc]
    vc_sum = _compress(full_rows(cache_v_cmp, vc), *cw[1])[:, :nc]
    ks_rows = full_rows(cache_k_sel, ks)[:, :sel_len]
    vs_rows = full_rows(cache_v_sel, vs)[:, :sel_len]
    kw_band = jnp.concatenate([cache_k_win[l].reshape(bs, win_keep, KV_WIDTH), kw[:, None, :]], axis=1)
    vw_band = jnp.concatenate([cache_v_win[l].reshape(bs, win_keep, KV_WIDTH), vw[:, None, :]], axis=1)
    h4d = lambda a: a.reshape(a.shape[0], a.shape[1], NSA_KV_HEADS, NSA_HEAD_DIM)
    g_s = gt[:3 * NSA_HEADS].T.reshape(bs, 3, NSA_KV_HEADS, NSA_GROUP)
    o_nsa = _nsa_sample(q.reshape(bs, NSA_KV_HEADS, NSA_GROUP, NSA_HEAD_DIM), g_s, h4d(kc_sum), h4d(vc_sum),
                        h4d(ks_rows), h4d(vs_rows), h4d(kw_band), h4d(vw_band), past).reshape(bs, NSA_WIDTH)
    hw = HGRN_WIDTH
    o_hgrn, s_state = _hgrn_sample(h4[:, :hw], h4[:, hw:2 * hw], h4[:, 2 * hw:3 * hw], h4[:, 3 * hw:],
                                   state_hgrn[l], lower, norm_g)
    y_sample = finish(xs, o_nsa, o_hgrn, gab, bs).reshape(bs, 1, D_MODEL)
    s5 = lambda a: a.reshape(1, bs, 1, NSA_KV_HEADS, NSA_HEAD_DIM)
    w5 = lambda a: a[:, -win_keep:].reshape(1, bs, win_keep, NSA_KV_HEADS, NSA_HEAD_DIM)
    new_s = (s5(kc), s5(vc), s5(ks), s5(vs), w5(kw_band), w5(vw_band), s_state[None])

    return (y_prompt, y_sample) + new_p + new_s
```

```python
import functools
import math

import jax
import jax.numpy as jnp
from jax import lax
from jax.experimental import pallas as pl
from jax.experimental.pallas import tpu as pltpu

F32 = jnp.float32
BF16 = jnp.bfloat16

D_MODEL = 1024
PAGE_SIZE = 128
NSA_HEADS = 8
NSA_KV_HEADS = 2
NSA_GROUP = NSA_HEADS // NSA_KV_HEADS
NSA_HEAD_DIM = 64
NSA_WIDTH = NSA_HEADS * NSA_HEAD_DIM
KV_WIDTH = NSA_KV_HEADS * NSA_HEAD_DIM
CMP_BLOCK = 32
CMP_STRIDE = 16
CMP_HIDDEN = 128
SEL_BLOCK = 64
SEL_TOP = 16
N_LOCAL_SEL = 2
FORCE_BONUS = 1.0e4
WINDOW = 512
HGRN_HEADS = 4
HGRN_HEAD_DIM = 128
HGRN_WIDTH = HGRN_HEADS * HGRN_HEAD_DIM
HGRN_CHUNK = 16
N_EXPERTS = 32
TOP_K = 4
D_EXPERT = 1024
SWIGLU_LIMIT = 7.0
SWIGLU_ALPHA = 1.702
DEPTH = 1
DN_ALPHA = (2 * DEPTH) ** 0.25
LN_EPS = 1e-5

LANES = 128
SUBLANES = 8
VMEM_BYTES_V7X = 64 * 1024 * 1024
VMEM_LIMIT = VMEM_BYTES_V7X * 3 // 4

Q_TILE = 128
K_TILE = 128
SEL_SWEEP = 512
HGRN_TILE = 128
MOE_ROWS = 512
ROUTE_TILE = 256
NEG_INF = float("-inf")

_C_Q = 0
_C_KV = _C_Q + NSA_WIDTH
_C_H = _C_KV + 6 * KV_WIDTH
_C_GAB = _C_H + 4 * HGRN_WIDTH
_C_END = _C_GAB + 2 * D_MODEL
GATE_ROWS = 32


def _params(*sem):
    return pltpu.CompilerParams(dimension_semantics=sem, vmem_limit_bytes=VMEM_LIMIT)


def _nt_dot(a, b):
    return lax.dot_general(a, b, (((1,), (1,)), ((), ())), preferred_element_type=F32)


def _dot(a, b):
    return jnp.dot(a, b, preferred_element_type=F32)


def _proj_kernel(x_ref, w_ref, wg_ref, q_ref, kc_ref, vc_ref, ks_ref, vs_ref, kw_ref, vw_ref,
                 h_ref, gab_ref, gt_ref):
    x = x_ref[...].astype(BF16)
    q_ref[...] = _dot(x, w_ref[:, _C_Q:_C_KV])
    kv = _dot(x, w_ref[:, _C_KV:_C_H])
    for n, ref in enumerate((kc_ref, vc_ref, ks_ref, vs_ref, kw_ref, vw_ref)):
        ref[...] = kv[:, n * KV_WIDTH:(n + 1) * KV_WIDTH]
    h_ref[...] = _dot(x, w_ref[:, _C_H:_C_GAB])
    gab_ref[...] = _dot(x, w_ref[:, _C_GAB:_C_END])
    gt_ref[...] = jax.nn.sigmoid(_nt_dot(wg_ref[...], x))


def _project(x, w_main, w_gate_t, tm):
    n = x.shape[0]
    row = lambda w: pl.BlockSpec((tm, w), lambda i: (i, 0))
    full = lambda a: pl.BlockSpec(a.shape, lambda i: (0,) * a.ndim)
    out_shape = ([jax.ShapeDtypeStruct((n, NSA_WIDTH), F32)]
                 + [jax.ShapeDtypeStruct((n, KV_WIDTH), F32)] * 6
                 + [jax.ShapeDtypeStruct((n, 4 * HGRN_WIDTH), F32),
                    jax.ShapeDtypeStruct((n, 2 * D_MODEL), F32),
                    jax.ShapeDtypeStruct((GATE_ROWS, n), F32)])
    out_specs = ([row(NSA_WIDTH)] + [row(KV_WIDTH)] * 6 + [row(4 * HGRN_WIDTH), row(2 * D_MODEL),
                 pl.BlockSpec((GATE_ROWS, tm), lambda i: (0, i))])
    return pl.pallas_call(
        _proj_kernel, grid=(n // tm,),
        in_specs=[row(D_MODEL), full(w_main), full(w_gate_t)],
        out_specs=out_specs, out_shape=out_shape,
        compiler_params=_params("parallel"), name="in_proj")(x, w_main, w_gate_t)


def _gelu_tanh(x):
    return 0.5 * x * (1.0 + jnp.tanh(math.sqrt(2.0 / math.pi) * (x + 0.044715 * (x * x * x))))


def _compress_kernel(c_ref, w1_ref, bias_ref, w2_ref, b2_ref, o_ref):
    c = c_ref[0].astype(BF16)
    j = c.shape[0]
    f = _dot(c, w1_ref[...])
    outs = []
    for g in range(NSA_KV_HEADS):
        base = g * 2 * CMP_HIDDEN
        first = f[:, base:base + CMP_HIDDEN]
        second = f[:, base + CMP_HIDDEN:base + 2 * CMP_HIDDEN]
        nxt = pltpu.roll(second, j - 1, axis=0)
        h = _gelu_tanh(first + nxt + bias_ref[g:g + 1, :])
        outs.append(_dot(h.astype(BF16), w2_ref[...]) + b2_ref[...])
    o_ref[0] = jnp.concatenate(outs, axis=1)


def _compress(rows, w1full, bias, w2, b2):
    b, t, _ = rows.shape
    j = t // CMP_STRIDE
    c = rows.reshape(b, j, CMP_STRIDE * KV_WIDTH)
    full = lambda a: pl.BlockSpec(a.shape, lambda i: (0,) * a.ndim)
    return pl.pallas_call(
        _compress_kernel, grid=(b,),
        in_specs=[pl.BlockSpec((1, j, CMP_STRIDE * KV_WIDTH), lambda i: (i, 0, 0)),
                  full(w1full), full(bias), full(w2), full(b2)],
        out_specs=pl.BlockSpec((1, j, KV_WIDTH), lambda i: (i, 0, 0)),
        out_shape=jax.ShapeDtypeStruct((b, j, KV_WIDTH), F32),
        compiler_params=_params("parallel"), name="compress")(c, w1full, bias, w2, b2)


def _compress_weights(pe, w1, b1, w2, b2):
    hd, hid = NSA_HEAD_DIM, CMP_HIDDEN
    halves = w1.reshape(2, CMP_STRIDE, hd, hid)
    eye = jnp.eye(NSA_KV_HEADS, dtype=w1.dtype)
    w1full = jnp.einsum('apdh,kg->pkdgah', halves, eye).reshape(
        CMP_STRIDE * KV_WIDTH, NSA_KV_HEADS * 2 * hid).astype(BF16)
    bias = jnp.einsum('pd,pdh->h', pe, w1, precision=lax.Precision.HIGHEST) + b1
    bias = jnp.broadcast_to(bias[None, :], (SUBLANES, hid))
    return w1full, bias, w2.astype(BF16), b2.reshape(1, hd)


def _nsa_prompt_kernel(q_ref, gt_ref, kc_ref, vc_ref, ks_ref, vs_ref, kw_ref, vw_ref, o_ref,
                       ks16_ref, vst16_ref, kw16_ref, vwt16_ref, imp_ref, score_ref, selb_ref, seloff_ref,
                       m_ref, l_ref, acc_ref, sc_a_ref, sc_b_ref, mt_a_ref, mt_b_ref):
    g = pl.program_id(1)
    i = pl.program_id(2)
    t0 = pl.multiple_of(i * Q_TILE, Q_TILE)
    rows = NSA_GROUP * Q_TILE
    nc = kc_ref.shape[1]
    ns = score_ref.shape[0]
    t_len = ks_ref.shape[1]
    hd = NSA_HEAD_DIM

    @pl.when((g == 0) & (i == 0))
    def _():
        def cast_step(c, carry):
            r0 = pl.multiple_of(c * K_TILE, K_TILE)
            ks16_ref[pl.ds(r0, K_TILE), :] = ks_ref[0, pl.ds(r0, K_TILE), :].astype(BF16)
            kw16_ref[pl.ds(r0, K_TILE), :] = kw_ref[0, pl.ds(r0, K_TILE), :].astype(BF16)
            vst16_ref[:, pl.ds(r0, K_TILE)] = vs_ref[0, pl.ds(r0, K_TILE), :].T.astype(BF16)
            vwt16_ref[:, pl.ds(r0, K_TILE)] = vw_ref[0, pl.ds(r0, K_TILE), :].T.astype(BF16)
            return carry
        lax.fori_loop(0, t_len // K_TILE, cast_step, 0)

    q = q_ref[0] * (hd ** -0.5 * math.log2(math.e))
    lane_head = lax.broadcasted_iota(jnp.int32, (Q_TILE, KV_WIDTH), 1) // NSA_HEAD_DIM
    parts = []
    for r in range(NSA_GROUP):
        qr = q[:, r * NSA_HEAD_DIM:(r + 1) * NSA_HEAD_DIM]
        parts.append(jnp.where(lane_head == g, jnp.concatenate([qr, qr], axis=1), 0.0))
    qs = jnp.concatenate(parts, axis=0).astype(BF16)

    tok = t0 + lax.broadcasted_iota(jnp.int32, (1, Q_TILE), 1)
    tok4 = jnp.concatenate([tok] * NSA_GROUP, axis=1)

    s = _nt_dot(kc_ref[0].astype(BF16), qs)
    n_idx = lax.broadcasted_iota(jnp.int32, (nc, rows), 0)
    ok = (n_idx * CMP_STRIDE + (CMP_BLOCK - 1) <= tok4) & (n_idx < nc - 1)
    s = jnp.where(ok, s, NEG_INF)
    m = jnp.max(s, axis=0, keepdims=True)
    m = jnp.where(m > NEG_INF, m, 0.0)
    e = jnp.exp2(s - m)
    p = e / jnp.maximum(jnp.sum(e, axis=0, keepdims=True), 1e-30)
    o_cmp = _dot(vc_ref[0].T.astype(BF16), p.astype(BF16))
    o_cmp = jnp.where(g == 0, o_cmp[:hd], o_cmp[hd:])

    def tile4(bias):
        return jnp.concatenate([bias] * NSA_GROUP, axis=1)

    head_rows = pl.ds(pl.multiple_of(g * hd, hd), hd)

    wk = WINDOW + Q_TILE
    w0 = pl.multiple_of(jnp.maximum(t0 - WINDOW, 0), Q_TILE)
    key_w = w0 + lax.broadcasted_iota(jnp.int32, (wk, Q_TILE), 0)
    band = jnp.where((key_w <= tok) & (key_w > tok - WINDOW), 0.0, NEG_INF)
    sw = _nt_dot(kw16_ref[pl.ds(w0, wk), :], qs) + tile4(band)
    pw = jnp.exp2(sw - jnp.max(sw, axis=0, keepdims=True))
    o_win = (_dot(vwt16_ref[head_rows, pl.ds(w0, wk)], pw.astype(BF16))
             / jnp.sum(pw, axis=0, keepdims=True))

    key_d = t0 + lax.broadcasted_iota(jnp.int32, (Q_TILE, Q_TILE), 0)
    causal = jnp.where(key_d <= tok, 0.0, NEG_INF)
    sd = _nt_dot(ks16_ref[pl.ds(t0, Q_TILE), :], qs) + tile4(causal)
    md = jnp.max(sd, axis=0, keepdims=True)
    pd = jnp.exp2(sd - md)
    m_ref[...] = md
    l_ref[...] = jnp.sum(pd, axis=0, keepdims=True)
    acc_ref[...] = _dot(vst16_ref[head_rows, pl.ds(t0, Q_TILE)], pd.astype(BF16))

    imp = p[:, 0:Q_TILE]
    for r in range(1, NSA_GROUP):
        imp = imp + p[:, r * Q_TILE:(r + 1) * Q_TILE]
    imp_ref[0:SUBLANES, :] = jnp.zeros((SUBLANES, Q_TILE), F32)
    imp_ref[SUBLANES:SUBLANES + nc, :] = imp
    ratio = SEL_BLOCK // CMP_STRIDE
    p_slc = imp_ref[pl.ds(SUBLANES - 1, ns, stride=ratio), :]
    for d in range(ratio):
        p_slc = p_slc + imp_ref[pl.ds(SUBLANES + d, ns, stride=ratio), :]
    blk = lax.broadcasted_iota(jnp.int32, (ns, Q_TILE), 0)
    cur = tok // SEL_BLOCK
    forced = (blk == 0) | (blk > cur - N_LOCAL_SEL)
    score = jnp.where(blk <= cur, p_slc + jnp.where(forced, FORCE_BONUS, 0.0), -FORCE_BONUS)
    score_ref[...] = score

    def rank_step(jp, rank):
        row = score_ref[pl.ds(jp, 1), :]
        beats = (row > score) | ((row == score) & (jp < blk))
        return rank + beats.astype(F32)
    n_top = min(SEL_TOP, ns)
    visible = blk <= cur

    @pl.when(t0 + Q_TILE <= n_top * SEL_BLOCK)
    def _():
        selb_ref[...] = jnp.where(visible, 0.0, NEG_INF)

    @pl.when(t0 + Q_TILE > n_top * SEL_BLOCK)
    def _():
        rank = lax.fori_loop(0, ns, rank_step, jnp.zeros((ns, Q_TILE), F32), unroll=8)
        selb_ref[...] = jnp.where((rank < float(n_top)) & visible, 0.0, NEG_INF)

    first_blk = i * (Q_TILE // SEL_BLOCK)
    seloff_ref[...] = jnp.where(blk < first_blk, selb_ref[...], NEG_INF)
    per_tile = SEL_SWEEP // SEL_BLOCK
    n_sweep = (t0 + SEL_SWEEP - 1) // SEL_SWEEP

    last_tile = t_len // SEL_SWEEP - 1

    def score_tile(kt, s_ref, mx_ref):
        k0 = pl.multiple_of(kt * SEL_SWEEP, SEL_SWEEP)
        bias = jnp.concatenate(
            [jnp.broadcast_to(seloff_ref[pl.ds(kt * per_tile + j, 1), :], (SEL_BLOCK, Q_TILE))
             for j in range(per_tile)], axis=0)
        sc = _nt_dot(ks16_ref[pl.ds(k0, SEL_SWEEP), :], qs) + tile4(bias)
        s_ref[...] = sc
        mx_ref[...] = jnp.max(sc, axis=0, keepdims=True)

    def consume_tile(kt, s_ref, mx_ref):
        k0 = pl.multiple_of(kt * SEL_SWEEP, SEL_SWEEP)
        m_new = jnp.maximum(m_ref[...], mx_ref[...])
        alpha = jnp.exp2(m_ref[...] - m_new)
        pp = jnp.exp2(s_ref[...] - m_new)
        l_ref[...] = alpha * l_ref[...] + jnp.sum(pp, axis=0, keepdims=True)
        acc_ref[...] = alpha * acc_ref[...] + _dot(vst16_ref[head_rows, pl.ds(k0, SEL_SWEEP)], pp.astype(BF16))
        m_ref[...] = m_new

    @pl.when(n_sweep > 0)
    def _():
        score_tile(0, sc_a_ref, mt_a_ref)

    def sel_step(kp, carry):
        kt = kp * 2
        score_tile(jnp.minimum(kt + 1, last_tile), sc_b_ref, mt_b_ref)
        consume_tile(kt, sc_a_ref, mt_a_ref)
        score_tile(jnp.minimum(kt + 2, last_tile), sc_a_ref, mt_a_ref)
        consume_tile(jnp.minimum(kt + 1, last_tile), sc_b_ref, mt_b_ref)
        return carry
    lax.fori_loop(0, (n_sweep + 1) // 2, sel_step, 0)
    o_sel = acc_ref[...] / l_ref[...]

    def gate(branch):
        gr = gt_ref[pl.ds(branch * NSA_HEADS + g * NSA_GROUP, NSA_GROUP), :]
        return jnp.concatenate([gr[r:r + 1, :] for r in range(NSA_GROUP)], axis=1)
    o_t = gate(0) * o_cmp + gate(1) * o_sel + gate(2) * o_win
    outs = []
    for r in range(0, NSA_GROUP, 2):
        pair = jnp.concatenate([o_t[:, r * Q_TILE:(r + 1) * Q_TILE],
                                o_t[:, (r + 1) * Q_TILE:(r + 2) * Q_TILE]], axis=0)
        outs.append(pair.T)
    o_ref[0] = jnp.concatenate(outs, axis=1)


def _nsa_prompt(q, gt, kc_sum, vc_sum, ks, vs, kw, vw):
    b, t, _ = q.shape
    assert t % (2 * SEL_SWEEP) == 0 and t >= WINDOW + Q_TILE, t
    nt = t // Q_TILE
    nc = kc_sum.shape[1]
    ns = t // SEL_BLOCK
    rows = NSA_GROUP * Q_TILE
    per_b = lambda a: pl.BlockSpec((1,) + a.shape[1:], lambda bi, g, i: (bi, 0, 0))
    return pl.pallas_call(
        _nsa_prompt_kernel, grid=(b, NSA_KV_HEADS, nt),
        in_specs=[pl.BlockSpec((1, Q_TILE, NSA_WIDTH // NSA_KV_HEADS), lambda bi, g, i: (bi, i, g)),
                  pl.BlockSpec((GATE_ROWS, Q_TILE), lambda bi, g, i: (0, bi * nt + i)),
                  per_b(kc_sum), per_b(vc_sum), per_b(ks), per_b(vs), per_b(kw), per_b(vw)],
        out_specs=pl.BlockSpec((1, Q_TILE, NSA_WIDTH // NSA_KV_HEADS), lambda bi, g, i: (bi, i, g)),
        out_shape=jax.ShapeDtypeStruct((b, t, NSA_WIDTH), F32),
        scratch_shapes=[pltpu.VMEM((t, KV_WIDTH), BF16),
                        pltpu.VMEM((KV_WIDTH, t), BF16),
                        pltpu.VMEM((t, KV_WIDTH), BF16),
                        pltpu.VMEM((KV_WIDTH, t), BF16),
                        pltpu.VMEM((SUBLANES + nc, Q_TILE), F32),
                        pltpu.VMEM((ns, Q_TILE), F32),
                        pltpu.VMEM((ns, Q_TILE), F32),
                        pltpu.VMEM((ns, Q_TILE), F32),
                        pltpu.VMEM((1, rows), F32),
                        pltpu.VMEM((1, rows), F32),
                        pltpu.VMEM((NSA_HEAD_DIM, rows), F32),
                        pltpu.VMEM((SEL_SWEEP, rows), F32),
                        pltpu.VMEM((SEL_SWEEP, rows), F32),
                        pltpu.VMEM((1, rows), F32),
                        pltpu.VMEM((1, rows), F32)],
        compiler_params=_params("arbitrary", "arbitrary", "arbitrary"),
        name="nsa_prompt")(q, gt, kc_sum, vc_sum, ks, vs, kw, vw)


def _hgrn_prompt_kernel(hq_ref, hf_ref, hi_ref, hg_ref, lb_ref, ng_ref, o_ref, st_out_ref, st_ref):
    c = pl.program_id(1)
    n = HGRN_TILE
    sub = HGRN_CHUNK
    hd = HGRN_HEAD_DIM

    @pl.when(c == 0)
    def _():
        st_ref[...] = jnp.zeros(st_ref.shape, F32)

    pos = lax.broadcasted_iota(jnp.int32, (n, hd), 0) % sub
    rc = lax.broadcasted_iota(jnp.int32, (n, n), 0)
    cc = lax.broadcasted_iota(jnp.int32, (n, n), 1)
    intra = (rc // sub == cc // sub) & (cc <= rc)
    tok_chunk = lax.broadcasted_iota(jnp.int32, (hd, n), 1) // sub

    for h in range(HGRN_HEADS):
        cols = slice(h * hd, (h + 1) * hd)
        lb = lb_ref[:, cols]
        f = lb + (1.0 - lb) * jax.nn.sigmoid(hf_ref[0, :, cols])
        logf = jnp.log(f)
        b = logf
        suf = logf
        sh = 1
        while sh < sub:
            b = b + jnp.where(pos >= sh, pltpu.roll(b, sh, axis=0), 0.0)
            suf = suf + jnp.where(pos + sh < sub, pltpu.roll(suf, n - sh, axis=0), 0.0)
            sh *= 2
        hq = hq_ref[0, :, cols]
        k = 1.0 - f
        q_dec = (hq * jax.nn.sigmoid(hq) * jnp.exp(b)).astype(BF16)
        k_inv = (k * jnp.exp(-b)).astype(BF16)
        k_end = (k * jnp.exp(suf - logf)).astype(BF16)
        v = hi_ref[0, :, cols]

        a = jnp.where(intra, _nt_dot(q_dec, k_inv), 0.0)
        o = _dot(a.astype(BF16), v.astype(BF16))

        vt = v.T
        st = st_ref[h]
        inter = []
        for ci in range(n // sub):
            inter.append(_nt_dot(q_dec[ci * sub:(ci + 1) * sub, :], st.astype(BF16)))
            kv_t = _dot(jnp.where(tok_chunk == ci, vt, 0.0).astype(BF16), k_end)
            st = st * jnp.exp(suf[ci * sub:ci * sub + 1, :]) + kv_t
        st_ref[h] = st
        o = o + jnp.concatenate(inter, axis=0)
        o = o * lax.rsqrt(jnp.mean(o * o, axis=-1, keepdims=True) + LN_EPS) * ng_ref[...]
        hg = hg_ref[0, :, cols]
        o_ref[0, :, cols] = o * (hg * jax.nn.sigmoid(hg))

    @pl.when(c == pl.num_programs(1) - 1)
    def _():
        for h in range(HGRN_HEADS):
            st_out_ref[0, h] = st_ref[h].T


def _hgrn_prompt(h4, lb, norm_g, b, t):
    nchunk = t // HGRN_TILE
    part = lambda p: pl.BlockSpec((1, HGRN_TILE, HGRN_WIDTH), lambda bi, c: (bi, c, p))
    return pl.pallas_call(
        _hgrn_prompt_kernel, grid=(b, nchunk),
        in_specs=[part(0), part(1), part(2), part(3),
                  pl.BlockSpec((1, HGRN_WIDTH), lambda bi, c: (0, 0)),
                  pl.BlockSpec((1, HGRN_HEAD_DIM), lambda bi, c: (0, 0))],
        out_specs=[pl.BlockSpec((1, HGRN_TILE, HGRN_WIDTH), lambda bi, c: (bi, c, 0)),
                   pl.BlockSpec((1, HGRN_HEADS, HGRN_HEAD_DIM, HGRN_HEAD_DIM), lambda bi, c: (bi, 0, 0, 0))],
        out_shape=[jax.ShapeDtypeStruct((b, t, HGRN_WIDTH), F32),
                   jax.ShapeDtypeStruct((b, HGRN_HEADS, HGRN_HEAD_DIM, HGRN_HEAD_DIM), F32)],
        scratch_shapes=[pltpu.VMEM((HGRN_HEADS, HGRN_HEAD_DIM, HGRN_HEAD_DIM), F32)],
        compiler_params=_params("parallel", "arbitrary"),
        name="hgrn_prompt")(h4, h4, h4, h4, lb, norm_g)


def _layer_norm(y, g, b):
    mu = jnp.mean(y, axis=-1, keepdims=True)
    d = y - mu
    var = jnp.mean(d * d, axis=-1, keepdims=True)
    return d * lax.rsqrt(var + LN_EPS) * g + b


def _finish_kernel(x_ref, oa_ref, ob_ref, gab_ref, wba_ref, wbb_ref, wout_ref, g_ref, b_ref,
                   wr_hi_ref, wr_lo_ref, br_ref, x1_ref, x1b_ref, lt_ref):
    a = _dot(oa_ref[...].astype(BF16), wba_ref[...])
    bb = _dot(ob_ref[...].astype(BF16), wbb_ref[...])
    mix = jax.nn.sigmoid(gab_ref[:, :D_MODEL]) * a + jax.nn.sigmoid(gab_ref[:, D_MODEL:]) * bb
    y = DN_ALPHA * x_ref[...] + _dot(mix.astype(BF16), wout_ref[...])
    x1 = _layer_norm(y, g_ref[...], b_ref[...])
    x1_ref[...] = x1
    hi = x1.astype(BF16)
    x1b_ref[...] = hi
    lo = (x1 - hi.astype(F32)).astype(BF16)
    lt_ref[...] = (_nt_dot(wr_hi_ref[...], hi) + _nt_dot(wr_hi_ref[...], lo)
                   + _nt_dot(wr_lo_ref[...], hi) + br_ref[...])


def _finish(x, oa, ob, gab, wba, wbb, wout, g1, b1, wr_hi, wr_lo, br, tm):
    n = x.shape[0]
    row = lambda w: pl.BlockSpec((tm, w), lambda i: (i, 0))
    full = lambda a: pl.BlockSpec(a.shape, lambda i: (0,) * a.ndim)
    return pl.pallas_call(
        _finish_kernel, grid=(n // tm,),
        in_specs=[row(D_MODEL), row(NSA_WIDTH), row(HGRN_WIDTH), row(2 * D_MODEL),
                  full(wba), full(wbb), full(wout), full(g1), full(b1), full(wr_hi), full(wr_lo), full(br)],
        out_specs=[row(D_MODEL), row(D_MODEL), pl.BlockSpec((N_EXPERTS, tm), lambda i: (0, i))],
        out_shape=[jax.ShapeDtypeStruct((n, D_MODEL), F32), jax.ShapeDtypeStruct((n, D_MODEL), BF16),
                   jax.ShapeDtypeStruct((N_EXPERTS, n), F32)],
        compiler_params=_params("parallel"), name="merge_ln_router")(
            x, oa, ob, gab, wba, wbb, wout, g1, b1, wr_hi, wr_lo, br)


def _route_kernel(lt_ref, tri_ref, e_ref, w_ref, pos_ref, cnt_ref, carry_ref):
    i = pl.program_id(0)

    @pl.when(i == 0)
    def _():
        carry_ref[...] = jnp.zeros(carry_ref.shape, F32)

    logit = lt_ref[...]
    tn = logit.shape[1]
    eid = lax.broadcasted_iota(jnp.int32, (N_EXPERTS, tn), 0)
    rank = jnp.zeros((N_EXPERTS, tn), F32)
    for ep in range(N_EXPERTS):
        row = logit[ep:ep + 1, :]
        rank = rank + ((row > logit) | ((row == logit) & (ep < eid))).astype(F32)
    sel = rank < float(TOP_K)
    top = jnp.max(logit, axis=0, keepdims=True)
    ex = jnp.where(sel, jnp.exp(logit - top), 0.0)
    wgt = ex / jnp.sum(ex, axis=0, keepdims=True)
    self = sel.astype(F32)
    incl = _dot(self.astype(BF16), tri_ref[...])
    pos = carry_ref[:, 0:1] + incl - self
    carry_ref[...] = carry_ref[...] + jnp.sum(self, axis=1, keepdims=True)
    eid_f = eid.astype(F32)
    for kk in range(TOP_K):
        pick = sel & (rank == float(kk))
        e_ref[kk:kk + 1, :] = jnp.sum(jnp.where(pick, eid_f, 0.0), axis=0, keepdims=True).astype(jnp.int32)
        w_ref[kk:kk + 1, :] = jnp.sum(jnp.where(pick, wgt, 0.0), axis=0, keepdims=True)
        pos_ref[kk:kk + 1, :] = jnp.sum(jnp.where(pick, pos, 0.0), axis=0, keepdims=True).astype(jnp.int32)
    cnt_ref[...] = carry_ref[...]


def _route(logit_t):
    n = logit_t.shape[1]
    tn = ROUTE_TILE if n % ROUTE_TILE == 0 else n
    tri = (lax.broadcasted_iota(jnp.int32, (tn, tn), 0) <= lax.broadcasted_iota(jnp.int32, (tn, tn), 1)).astype(BF16)
    col = lambda r: pl.BlockSpec((r, tn), lambda i: (0, i))
    return pl.pallas_call(
        _route_kernel, grid=(n // tn,),
        in_specs=[col(N_EXPERTS), pl.BlockSpec((tn, tn), lambda i: (0, 0))],
        out_specs=[col(TOP_K), col(TOP_K), col(TOP_K), pl.BlockSpec((N_EXPERTS, LANES), lambda i: (0, 0))],
        out_shape=[jax.ShapeDtypeStruct((TOP_K, n), jnp.int32), jax.ShapeDtypeStruct((TOP_K, n), F32),
                   jax.ShapeDtypeStruct((TOP_K, n), jnp.int32), jax.ShapeDtypeStruct((N_EXPERTS, LANES), F32)],
        scratch_shapes=[pltpu.VMEM((N_EXPERTS, LANES), F32)],
        compiler_params=_params("arbitrary"), name="route")(logit_t, tri)


def _expert_kernel(blk_e_ref, nblk_ref, xb_ref, wgu_ref, bgu_ref, wd_ref, bd_ref, y_ref, wgu_b, wd_b):
    i = pl.program_id(0)
    live = i < nblk_ref[0]

    @pl.when(live & ((i == 0) | (blk_e_ref[i] != blk_e_ref[jnp.maximum(i - 1, 0)])))
    def _():
        wgu_b[...] = wgu_ref[0].astype(BF16)
        wd_b[...] = wd_ref[0].astype(BF16)

    @pl.when(live)
    def _():
        gu = _dot(xb_ref[...], wgu_b[...]) + bgu_ref[0]
        gate = jnp.minimum(gu[:, :D_EXPERT], SWIGLU_LIMIT)
        up = jnp.clip(gu[:, D_EXPERT:], -SWIGLU_LIMIT, SWIGLU_LIMIT)
        h = gate * jax.nn.sigmoid(SWIGLU_ALPHA * gate) * (up + 1.0)
        y_ref[...] = _dot(h.astype(BF16), wd_b[...]) + bd_ref[0]

    @pl.when(i >= nblk_ref[0])
    def _():
        y_ref[...] = jnp.zeros(y_ref.shape, F32)


def _experts(xb, blk_e, nblk, w_gu, b_gu, w_down, b_down):
    m = xb.shape[0]
    grid_spec = pltpu.PrefetchScalarGridSpec(
        num_scalar_prefetch=2, grid=(m // MOE_ROWS,),
        in_specs=[pl.BlockSpec((MOE_ROWS, D_MODEL), lambda i, be, nb: (i, 0)),
                  pl.BlockSpec((1, D_MODEL, 2 * D_EXPERT), lambda i, be, nb: (be[i], 0, 0)),
                  pl.BlockSpec((1, 1, 2 * D_EXPERT), lambda i, be, nb: (be[i], 0, 0)),
                  pl.BlockSpec((1, D_EXPERT, D_MODEL), lambda i, be, nb: (be[i], 0, 0)),
                  pl.BlockSpec((1, 1, D_MODEL), lambda i, be, nb: (be[i], 0, 0))],
        out_specs=pl.BlockSpec((MOE_ROWS, D_MODEL), lambda i, be, nb: (i, 0)),
        scratch_shapes=[pltpu.VMEM((D_MODEL, 2 * D_EXPERT), BF16), pltpu.VMEM((D_EXPERT, D_MODEL), BF16)])
    return pl.pallas_call(
        _expert_kernel, grid_spec=grid_spec,
        out_shape=jax.ShapeDtypeStruct((m, D_MODEL), F32),
        compiler_params=_params("arbitrary"), name="experts")(
            blk_e, nblk, xb, w_gu, b_gu.reshape(N_EXPERTS, 1, 2 * D_EXPERT),
            w_down, b_down.reshape(N_EXPERTS, 1, D_MODEL))


def _final_ln_kernel(x1_ref, y_ref, g_ref, b_ref, o_ref):
    o_ref[...] = _layer_norm(DN_ALPHA * x1_ref[...] + y_ref[...], g_ref[...], b_ref[...])


def _final_ln(x1, y, g2, b2, tm):
    n = x1.shape[0]
    row = pl.BlockSpec((tm, D_MODEL), lambda i: (i, 0))
    vec = pl.BlockSpec((1, D_MODEL), lambda i: (0, 0))
    return pl.pallas_call(
        _final_ln_kernel, grid=(n // tm,), in_specs=[row, row, vec, vec], out_specs=row,
        out_shape=jax.ShapeDtypeStruct((n, D_MODEL), F32),
        compiler_params=_params("parallel"), name="final_ln")(x1, y, g2, b2)


def _moe_and_norm(x1, x1b, logit_t, w_gu, b_gu, w_down, b_down, g2, b2, tm):
    n = x1.shape[0]
    top_e, top_w, top_pos, counts = _route(logit_t)
    counts = counts[:, 0].astype(jnp.int32)
    padded = (counts + MOE_ROWS - 1) // MOE_ROWS * MOE_ROWS
    pad_end = jnp.cumsum(padded)
    pad_start = pad_end - padded
    n_blocks = -(-(n * TOP_K + N_EXPERTS * (MOE_ROWS - 1)) // MOE_ROWS)
    dest = pad_start[top_e] + top_pos
    tok = jnp.broadcast_to(jnp.arange(n, dtype=jnp.int32)[None, :], dest.shape)
    slot_tok = jnp.zeros((n_blocks * MOE_ROWS,), jnp.int32).at[dest.reshape(-1)].set(tok.reshape(-1))
    xb = x1b[slot_tok]
    blk_e = jnp.minimum(jnp.searchsorted(pad_end, jnp.arange(n_blocks, dtype=jnp.int32) * MOE_ROWS, side='right'),
                        N_EXPERTS - 1).astype(jnp.int32)
    nblk = (pad_end[-1:] // MOE_ROWS).astype(jnp.int32)
    yb = _experts(xb, blk_e, nblk, w_gu, b_gu, w_down, b_down)
    y = jnp.zeros((n, D_MODEL), F32)
    for kk in range(TOP_K):
        y = y + yb[dest[kk]] * top_w[kk][:, None]
    return _final_ln(x1, y, g2, b2, tm)


def _masked_softmax(s, mask):
    s = jnp.where(mask, s.astype(F32), -jnp.inf)
    m = jnp.max(s, axis=-1, keepdims=True)
    m = jnp.where(jnp.isfinite(m), m, 0.0)
    e = jnp.exp(s - m)
    return e / jnp.maximum(jnp.sum(e, axis=-1, keepdims=True), 1e-30)


def _nsa_sample(q, g, kc_sum, vc_sum, ks_rows, vs_rows, kw, vw, past):
    scale = NSA_HEAD_DIM ** -0.5
    b = q.shape[0]
    nc = kc_sum.shape[1]
    ns = ks_rows.shape[1] // SEL_BLOCK
    q_pos = past
    s = jnp.einsum('bgrd,bngd->bgrn', q, kc_sum) * scale
    c_ok = (jnp.arange(nc) * CMP_STRIDE + (CMP_BLOCK - 1)) <= q_pos
    p_cmp = _masked_softmax(s, c_ok[None, None, None, :])
    o_cmp = jnp.einsum('bgrn,bngd->bgrd', p_cmp, vc_sum)
    r = SEL_BLOCK // CMP_STRIDE
    imp = jnp.pad(jnp.sum(p_cmp, axis=2), ((0, 0), (0, 0), (1, 1)))
    p_slc = jnp.sum(imp[..., :r * ns].reshape(b, NSA_KV_HEADS, ns, r), axis=-1) + imp[..., r::r]
    blk = jnp.arange(ns)
    cur = q_pos // SEL_BLOCK
    forced = (blk == 0) | (blk > cur - N_LOCAL_SEL)
    score = jnp.where(blk <= cur, p_slc + jnp.where(forced, FORCE_BONUS, 0.0), -FORCE_BONUS)
    _, idx = lax.top_k(score, min(SEL_TOP, ns))
    ks_blk = ks_rows.reshape(b, ns, SEL_BLOCK, NSA_KV_HEADS, NSA_HEAD_DIM).transpose(0, 3, 1, 2, 4)
    vs_blk = vs_rows.reshape(b, ns, SEL_BLOCK, NSA_KV_HEADS, NSA_HEAD_DIM).transpose(0, 3, 1, 2, 4)
    take = jax.vmap(jax.vmap(lambda blocks, ix: blocks[ix]))
    ks = take(ks_blk, idx).reshape(b, NSA_KV_HEADS, -1, NSA_HEAD_DIM)
    vs = take(vs_blk, idx).reshape(b, NSA_KV_HEADS, -1, NSA_HEAD_DIM)
    k_pos = (idx[..., None] * SEL_BLOCK + jnp.arange(SEL_BLOCK)).reshape(b, NSA_KV_HEADS, 1, -1)
    s = jnp.einsum('bgrd,bgkd->bgrk', q, ks) * scale
    p = _masked_softmax(s, k_pos <= q_pos)
    o_sel = jnp.einsum('bgrk,bgkd->bgrd', p, vs)
    kwn = kw.shape[1]
    w_pos = past - (kwn - 1) + jnp.arange(kwn)
    w_ok = (w_pos <= q_pos) & (w_pos > q_pos - WINDOW) & (w_pos >= 0)
    s = jnp.einsum('bgrd,bkgd->bgrk', q, kw) * scale
    p = _masked_softmax(s, w_ok[None, None, None, :])
    o_win = jnp.einsum('bgrk,bkgd->bgrd', p, vw)
    return g[:, 0, ..., None] * o_cmp + g[:, 1, ..., None] * o_sel + g[:, 2, ..., None] * o_win


def _hgrn_sample(hq, hf, hi, hg, s0, lb, norm_g):
    b = hq.shape[0]
    shp = (b, HGRN_HEADS, HGRN_HEAD_DIM)
    f = (lb + (1.0 - lb) * jax.nn.sigmoid(hf)).reshape(shp)
    q = jax.nn.silu(hq).reshape(shp)
    k = 1.0 - f
    v = hi.reshape(shp)
    s_new = f[..., None] * s0 + k[..., None] * v[:, :, None, :]
    o = jnp.einsum('bhd,bhdv->bhv', q, s_new, precision=lax.Precision.HIGHEST)
    o = o * lax.rsqrt(jnp.mean(jnp.square(o), axis=-1, keepdims=True) + LN_EPS) * norm_g
    return o.reshape(b, HGRN_WIDTH) * jax.nn.silu(hg), s_new


def kernel(x_prompt, x_sample, cache_k_cmp, cache_v_cmp, cache_k_sel, cache_v_sel, cache_k_win, cache_v_win, state_hgrn, page_table, w_in, cmp_pe, cmp_w1, cmp_b1, cmp_w2, cmp_b2, hgrn_gamma, hgrn_norm, w_branch_a, w_branch_b, w_out, ln1_g, ln1_b, w_router, b_router, w_gate_up, b_gate_up, w_down, b_down, ln2_g, ln2_b):
    l = 0
    bp, t, _ = x_prompt.shape
    bs = x_sample.shape[0]
    past = page_table.shape[1] * PAGE_SIZE
    win_keep = cache_k_win.shape[2]

    lower = jnp.cumsum(jax.nn.softmax(hgrn_gamma.astype(F32), axis=0), axis=0)[l][None, :]
    norm_g = hgrn_norm[l][None, :]
    w = w_in[l]
    c_g = NSA_WIDTH + 6 * KV_WIDTH
    n_g = 3 * NSA_HEADS
    w_main = jnp.concatenate([w[:, :c_g], w[:, c_g + n_g:]], axis=1).astype(BF16)
    w_gate_t = jnp.pad(w[:, c_g:c_g + n_g].T, ((0, GATE_ROWS - n_g), (0, 0))).astype(BF16)
    cw = [_compress_weights(cmp_pe[l, n], cmp_w1[l, n], cmp_b1[l, n], cmp_w2[l, n], cmp_b2[l, n]) for n in range(2)]
    wba = w_branch_a[l].astype(BF16)
    wbb = w_branch_b[l].astype(BF16)
    wout = w_out[l].astype(BF16)
    wr_t = w_router[l].T
    wr_hi = wr_t.astype(BF16)
    wr_lo = (wr_t - wr_hi.astype(F32)).astype(BF16)
    br = jnp.broadcast_to(b_router[l].astype(F32)[:, None], (N_EXPERTS, 1))
    g1, b1 = ln1_g[l][None, :], ln1_b[l][None, :]
    g2, b2 = ln2_g[l][None, :], ln2_b[l][None, :]

    def finish(x2d, o_a, o_b, gab, tm):
        x1, x1b, logit_t = _finish(x2d, o_a, o_b, gab, wba, wbb, wout, g1, b1, wr_hi, wr_lo, br, tm)
        return _moe_and_norm(x1, x1b, logit_t, w_gate_up[l], b_gate_up[l], w_down[l], b_down[l], g2, b2, tm)

    n = bp * t
    xp = x_prompt.reshape(n, D_MODEL)
    q, kc, vc, ks, vs, kw, vw, h4, gab, gt = _project(xp, w_main, w_gate_t, 256)
    r3 = lambda a: a.reshape(bp, t, a.shape[-1])
    kc_sum = _compress(r3(kc), *cw[0])
    vc_sum = _compress(r3(vc), *cw[1])
    o_nsa = _nsa_prompt(r3(q), gt, kc_sum, vc_sum, r3(ks), r3(vs), r3(kw), r3(vw))
    o_hgrn, p_state = _hgrn_prompt(r3(h4), lower, norm_g, bp, t)
    y_prompt = finish(xp, o_nsa.reshape(n, NSA_WIDTH), o_hgrn.reshape(n, HGRN_WIDTH), gab, 256).reshape(bp, t, D_MODEL)
    r5 = lambda a: a.reshape(1, bp, t, NSA_KV_HEADS, NSA_HEAD_DIM)
    win = min(WINDOW, t)
    new_p = (r5(kc), r5(vc), r5(ks), r5(vs), r5(kw)[:, :, t - win:], r5(vw)[:, :, t - win:], p_state[None])

    xs = x_sample.reshape(bs, D_MODEL)
    q, kc, vc, ks, vs, kw, vw, h4, gab, gt = _project(xs, w_main, w_gate_t, bs)
    pad = (-(past + 1)) % (SUBLANES * CMP_STRIDE)

    def full_rows(pool, new):
        old = pool[l][page_table].reshape(bs, past, KV_WIDTH)
        return jnp.concatenate([old, new[:, None, :], jnp.zeros((bs, pad, KV_WIDTH), F32)], axis=1)

    sel_len = past + 1 + (-(past + 1)) % SEL_BLOCK
    nc = sel_len // CMP_STRIDE - 1
    kc_sum = _compress(full_rows(cache_k_cmp, kc), *cw[0])[:, :nc]
    vc_sum = _compress(full_rows(cache_v_cmp, vc), *cw[1])[:, :nc]
    ks_rows = full_rows(cache_k_sel, ks)[:, :sel_len]
    vs_rows = full_rows(cache_v_sel, vs)[:, :sel_len]
    kw_band = jnp.concatenate([cache_k_win[l].reshape(bs, win_keep, KV_WIDTH), kw[:, None, :]], axis=1)
    vw_band = jnp.concatenate([cache_v_win[l].reshape(bs, win_keep, KV_WIDTH), vw[:, None, :]], axis=1)
    h4d = lambda a: a.reshape(a.shape[0], a.shape[1], NSA_KV_HEADS, NSA_HEAD_DIM)
    g_s = gt[:3 * NSA_HEADS].T.reshape(bs, 3, NSA_KV_HEADS, NSA_GROUP)
    o_nsa = _nsa_sample(q.reshape(bs, NSA_KV_HEADS, NSA_GROUP, NSA_HEAD_DIM), g_s, h4d(kc_sum), h4d(vc_sum),
                        h4d(ks_rows), h4d(vs_rows), h4d(kw_band), h4d(vw_band), past).reshape(bs, NSA_WIDTH)
    hw = HGRN_WIDTH
    o_hgrn, s_state = _hgrn_sample(h4[:, :hw], h4[:, hw:2 * hw], h4[:, 2 * hw:3 * hw], h4[:, 3 * hw:],
                                   state_hgrn[l], lower, norm_g)
    y_sample = finish(xs, o_nsa, o_hgrn, gab, bs).reshape(bs, 1, D_MODEL)
    s5 = lambda a: a.reshape(1, bs, 1, NSA_KV_HEADS, NSA_HEAD_DIM)
    w5 = lambda a: a[:, -win_keep:].reshape(1, bs, win_keep, NSA_KV_HEADS, NSA_HEAD_DIM)
    new_s = (s5(kc), s5(vc), s5(ks), s5(vs), w5(kw_band), w5(vw_band), s_state[None])

    return (y_prompt, y_sample) + new_p + new_s
```

```python
import functools
import math

import jax
import jax.numpy as jnp
from jax import lax
from jax.experimental import pallas as pl
from jax.experimental.pallas import tpu as pltpu

F32 = jnp.float32
BF16 = jnp.bfloat16

D_MODEL = 1024
PAGE_SIZE = 128
NSA_HEADS = 8
NSA_KV_HEADS = 2
NSA_GROUP = NSA_HEADS // NSA_KV_HEADS
NSA_HEAD_DIM = 64
NSA_WIDTH = NSA_HEADS * NSA_HEAD_DIM
KV_WIDTH = NSA_KV_HEADS * NSA_HEAD_DIM
CMP_BLOCK = 32
CMP_STRIDE = 16
CMP_HIDDEN = 128
SEL_BLOCK = 64
SEL_TOP = 16
N_LOCAL_SEL = 2
FORCE_BONUS = 1.0e4
WINDOW = 512
HGRN_HEADS = 4
HGRN_HEAD_DIM = 128
HGRN_WIDTH = HGRN_HEADS * HGRN_HEAD_DIM
HGRN_CHUNK = 16
N_EXPERTS = 32
TOP_K = 4
D_EXPERT = 1024
SWIGLU_LIMIT = 7.0
SWIGLU_ALPHA = 1.702
DEPTH = 1
DN_ALPHA = (2 * DEPTH) ** 0.25
LN_EPS = 1e-5

LANES = 128
SUBLANES = 8
VMEM_BYTES_V7X = 64 * 1024 * 1024
VMEM_LIMIT = VMEM_BYTES_V7X * 3 // 4

Q_TILE = 128
K_TILE = 128
SEL_SWEEP = 512
HGRN_TILE = 128
MOE_ROWS = 512
ROUTE_TILE = 256
NEG_INF = float("-inf")

_C_Q = 0
_C_KV = _C_Q + NSA_WIDTH
_C_H = _C_KV + 6 * KV_WIDTH
_C_GAB = _C_H + 4 * HGRN_WIDTH
_C_END = _C_GAB + 2 * D_MODEL
GATE_ROWS = 32


def _params(*sem):
    return pltpu.CompilerParams(dimension_semantics=sem, vmem_limit_bytes=VMEM_LIMIT)


def _nt_dot(a, b):
    return lax.dot_general(a, b, (((1,), (1,)), ((), ())), preferred_element_type=F32)


def _dot(a, b):
    return jnp.dot(a, b, preferred_element_type=F32)


def _proj_kernel(x_ref, w_ref, wg_ref, q_ref, kc_ref, vc_ref, ks_ref, vs_ref, kw_ref, vw_ref,
                 h_ref, gab_ref, gt_ref):
    x = x_ref[...].astype(BF16)
    q_ref[...] = _dot(x, w_ref[:, _C_Q:_C_KV])
    kv = _dot(x, w_ref[:, _C_KV:_C_H])
    for n, ref in enumerate((kc_ref, vc_ref, ks_ref, vs_ref, kw_ref, vw_ref)):
        ref[...] = kv[:, n * KV_WIDTH:(n + 1) * KV_WIDTH]
    h_ref[...] = _dot(x, w_ref[:, _C_H:_C_GAB])
    gab_ref[...] = _dot(x, w_ref[:, _C_GAB:_C_END])
    gt_ref[...] = jax.nn.sigmoid(_nt_dot(wg_ref[...], x))


def _project(x, w_main, w_gate_t, tm):
    n = x.shape[0]
    row = lambda w: pl.BlockSpec((tm, w), lambda i: (i, 0))
    full = lambda a: pl.BlockSpec(a.shape, lambda i: (0,) * a.ndim)
    out_shape = ([jax.ShapeDtypeStruct((n, NSA_WIDTH), F32)]
                 + [jax.ShapeDtypeStruct((n, KV_WIDTH), F32)] * 6
                 + [jax.ShapeDtypeStruct((n, 4 * HGRN_WIDTH), F32),
                    jax.ShapeDtypeStruct((n, 2 * D_MODEL), F32),
                    jax.ShapeDtypeStruct((GATE_ROWS, n), F32)])
    out_specs = ([row(NSA_WIDTH)] + [row(KV_WIDTH)] * 6 + [row(4 * HGRN_WIDTH), row(2 * D_MODEL),
                 pl.BlockSpec((GATE_ROWS, tm), lambda i: (0, i))])
    return pl.pallas_call(
        _proj_kernel, grid=(n // tm,),
        in_specs=[row(D_MODEL), full(w_main), full(w_gate_t)],
        out_specs=out_specs, out_shape=out_shape,
        compiler_params=_params("parallel"), name="in_proj")(x, w_main, w_gate_t)


def _gelu_tanh(x):
    return 0.5 * x * (1.0 + jnp.tanh(math.sqrt(2.0 / math.pi) * (x + 0.044715 * (x * x * x))))


def _compress_tail(f, bias_ref, w2_ref, b2_ref):
    j = f.shape[0]
    outs = []
    for g in range(NSA_KV_HEADS):
        base = g * 2 * CMP_HIDDEN
        first = f[:, base:base + CMP_HIDDEN]
        second = f[:, base + CMP_HIDDEN:base + 2 * CMP_HIDDEN]
        nxt = pltpu.roll(second, j - 1, axis=0)
        h = _gelu_tanh(first + nxt + bias_ref[g:g + 1, :])
        outs.append(_dot(h.astype(BF16), w2_ref[...]) + b2_ref[...])
    return jnp.concatenate(outs, axis=1)


def _compress_kernel(c_ref, w1_ref, bias_ref, w2_ref, b2_ref, o_ref):
    c = c_ref[0].astype(BF16)
    o_ref[0] = _compress_tail(_dot(c, w1_ref[...]), bias_ref, w2_ref, b2_ref)


def _compress(rows, w1full, bias, w2, b2):
    b, t, _ = rows.shape
    j = t // CMP_STRIDE
    c = rows.reshape(b, j, CMP_STRIDE * KV_WIDTH)
    full = lambda a: pl.BlockSpec(a.shape, lambda i: (0,) * a.ndim)
    return pl.pallas_call(
        _compress_kernel, grid=(b,),
        in_specs=[pl.BlockSpec((1, j, CMP_STRIDE * KV_WIDTH), lambda i: (i, 0, 0)),
                  full(w1full), full(bias), full(w2), full(b2)],
        out_specs=pl.BlockSpec((1, j, KV_WIDTH), lambda i: (i, 0, 0)),
        out_shape=jax.ShapeDtypeStruct((b, j, KV_WIDTH), F32),
        compiler_params=_params("parallel"), name="compress")(c, w1full, bias, w2, b2)


def _compress_weights(pe, w1, b1, w2, b2):
    hd, hid = NSA_HEAD_DIM, CMP_HIDDEN
    halves = w1.reshape(2, CMP_STRIDE, hd, hid)
    eye = jnp.eye(NSA_KV_HEADS, dtype=w1.dtype)
    w1full = jnp.einsum('apdh,kg->pkdgah', halves, eye).reshape(
        CMP_STRIDE * KV_WIDTH, NSA_KV_HEADS * 2 * hid).astype(BF16)
    bias = jnp.einsum('pd,pdh->h', pe, w1, precision=lax.Precision.HIGHEST) + b1
    bias = jnp.broadcast_to(bias[None, :], (SUBLANES, hid))
    return w1full, bias, w2.astype(BF16), b2.reshape(1, hd)


def _nsa_prompt_kernel(q_ref, gt_ref, kc_ref, vc_ref, ks_ref, vs_ref, kw_ref, vw_ref, o_ref,
                       ks16_ref, vst16_ref, kw16_ref, vwt16_ref, imp_ref, score_ref, selb_ref, seloff_ref,
                       m_ref, l_ref, acc_ref, sc_a_ref, sc_b_ref, mt_a_ref, mt_b_ref):
    g = pl.program_id(1)
    i = pl.program_id(2)
    t0 = pl.multiple_of(i * Q_TILE, Q_TILE)
    rows = NSA_GROUP * Q_TILE
    nc = kc_ref.shape[1]
    ns = score_ref.shape[0]
    t_len = ks_ref.shape[1]
    hd = NSA_HEAD_DIM

    @pl.when((g == 0) & (i == 0))
    def _():
        def cast_step(c, carry):
            r0 = pl.multiple_of(c * K_TILE, K_TILE)
            ks16_ref[pl.ds(r0, K_TILE), :] = ks_ref[0, pl.ds(r0, K_TILE), :].astype(BF16)
            kw16_ref[pl.ds(r0, K_TILE), :] = kw_ref[0, pl.ds(r0, K_TILE), :].astype(BF16)
            vst16_ref[:, pl.ds(r0, K_TILE)] = vs_ref[0, pl.ds(r0, K_TILE), :].T.astype(BF16)
            vwt16_ref[:, pl.ds(r0, K_TILE)] = vw_ref[0, pl.ds(r0, K_TILE), :].T.astype(BF16)
            return carry
        lax.fori_loop(0, t_len // K_TILE, cast_step, 0)

    q = q_ref[0] * (hd ** -0.5 * math.log2(math.e))
    lane_head = lax.broadcasted_iota(jnp.int32, (Q_TILE, KV_WIDTH), 1) // NSA_HEAD_DIM
    parts = []
    for r in range(NSA_GROUP):
        qr = q[:, r * NSA_HEAD_DIM:(r + 1) * NSA_HEAD_DIM]
        parts.append(jnp.where(lane_head == g, jnp.concatenate([qr, qr], axis=1), 0.0))
    qs = jnp.concatenate(parts, axis=0).astype(BF16)

    tok = t0 + lax.broadcasted_iota(jnp.int32, (1, Q_TILE), 1)
    tok4 = jnp.concatenate([tok] * NSA_GROUP, axis=1)

    s = _nt_dot(kc_ref[0].astype(BF16), qs)
    n_idx = lax.broadcasted_iota(jnp.int32, (nc, rows), 0)
    ok = (n_idx * CMP_STRIDE + (CMP_BLOCK - 1) <= tok4) & (n_idx < nc - 1)
    s = jnp.where(ok, s, NEG_INF)
    m = jnp.max(s, axis=0, keepdims=True)
    m = jnp.where(m > NEG_INF, m, 0.0)
    e = jnp.exp2(s - m)
    p = e / jnp.maximum(jnp.sum(e, axis=0, keepdims=True), 1e-30)
    o_cmp = _dot(vc_ref[0].T.astype(BF16), p.astype(BF16))
    o_cmp = jnp.where(g == 0, o_cmp[:hd], o_cmp[hd:])

    def tile4(bias):
        return jnp.concatenate([bias] * NSA_GROUP, axis=1)

    head_rows = pl.ds(pl.multiple_of(g * hd, hd), hd)

    wk = WINDOW + Q_TILE
    w0 = pl.multiple_of(jnp.maximum(t0 - WINDOW, 0), Q_TILE)
    key_w = w0 + lax.broadcasted_iota(jnp.int32, (wk, Q_TILE), 0)
    band = jnp.where((key_w <= tok) & (key_w > tok - WINDOW), 0.0, NEG_INF)
    sw = _nt_dot(kw16_ref[pl.ds(w0, wk), :], qs) + tile4(band)
    pw = jnp.exp2(sw - jnp.max(sw, axis=0, keepdims=True))
    o_win = (_dot(vwt16_ref[head_rows, pl.ds(w0, wk)], pw.astype(BF16))
             / jnp.sum(pw, axis=0, keepdims=True))

    key_d = t0 + lax.broadcasted_iota(jnp.int32, (Q_TILE, Q_TILE), 0)
    causal = jnp.where(key_d <= tok, 0.0, NEG_INF)
    sd = _nt_dot(ks16_ref[pl.ds(t0, Q_TILE), :], qs) + tile4(causal)
    md = jnp.max(sd, axis=0, keepdims=True)
    pd = jnp.exp2(sd - md)
    m_ref[...] = md
    l_ref[...] = jnp.sum(pd, axis=0, keepdims=True)
    acc_ref[...] = _dot(vst16_ref[head_rows, pl.ds(t0, Q_TILE)], pd.astype(BF16))

    imp = p[:, 0:Q_TILE]
    for r in range(1, NSA_GROUP):
        imp = imp + p[:, r * Q_TILE:(r + 1) * Q_TILE]
    imp_ref[0:SUBLANES, :] = jnp.zeros((SUBLANES, Q_TILE), F32)
    imp_ref[SUBLANES:SUBLANES + nc, :] = imp
    ratio = SEL_BLOCK // CMP_STRIDE
    p_slc = imp_ref[pl.ds(SUBLANES - 1, ns, stride=ratio), :]
    for d in range(ratio):
        p_slc = p_slc + imp_ref[pl.ds(SUBLANES + d, ns, stride=ratio), :]
    blk = lax.broadcasted_iota(jnp.int32, (ns, Q_TILE), 0)
    cur = tok // SEL_BLOCK
    forced = (blk == 0) | (blk > cur - N_LOCAL_SEL)
    score = jnp.where(blk <= cur, p_slc + jnp.where(forced, FORCE_BONUS, 0.0), -FORCE_BONUS)
    score_ref[...] = score

    def rank_step(jp, rank):
        row = score_ref[pl.ds(jp, 1), :]
        beats = (row > score) | ((row == score) & (jp < blk))
        return rank + beats.astype(F32)
    n_top = min(SEL_TOP, ns)
    visible = blk <= cur

    @pl.when(t0 + Q_TILE <= n_top * SEL_BLOCK)
    def _():
        selb_ref[...] = jnp.where(visible, 0.0, NEG_INF)

    @pl.when(t0 + Q_TILE > n_top * SEL_BLOCK)
    def _():
        rank = lax.fori_loop(0, ns, rank_step, jnp.zeros((ns, Q_TILE), F32), unroll=8)
        selb_ref[...] = jnp.where((rank < float(n_top)) & visible, 0.0, NEG_INF)

    first_blk = i * (Q_TILE // SEL_BLOCK)
    seloff_ref[...] = jnp.where(blk < first_blk, selb_ref[...], NEG_INF)
    per_tile = SEL_SWEEP // SEL_BLOCK
    n_sweep = (t0 + SEL_SWEEP - 1) // SEL_SWEEP

    last_tile = t_len // SEL_SWEEP - 1

    def score_tile(kt, s_ref, mx_ref):
        k0 = pl.multiple_of(kt * SEL_SWEEP, SEL_SWEEP)
        bias = jnp.concatenate(
            [jnp.broadcast_to(seloff_ref[pl.ds(kt * per_tile + j, 1), :], (SEL_BLOCK, Q_TILE))
             for j in range(per_tile)], axis=0)
        sc = _nt_dot(ks16_ref[pl.ds(k0, SEL_SWEEP), :], qs) + tile4(bias)
        s_ref[...] = sc
        mx_ref[...] = jnp.max(sc, axis=0, keepdims=True)

    def consume_tile(kt, s_ref, mx_ref):
        k0 = pl.multiple_of(kt * SEL_SWEEP, SEL_SWEEP)
        m_new = jnp.maximum(m_ref[...], mx_ref[...])
        alpha = jnp.exp2(m_ref[...] - m_new)
        pp = jnp.exp2(s_ref[...] - m_new)
        l_ref[...] = alpha * l_ref[...] + jnp.sum(pp, axis=0, keepdims=True)
        acc_ref[...] = alpha * acc_ref[...] + _dot(vst16_ref[head_rows, pl.ds(k0, SEL_SWEEP)], pp.astype(BF16))
        m_ref[...] = m_new

    @pl.when(n_sweep > 0)
    def _():
        score_tile(0, sc_a_ref, mt_a_ref)

    def sel_step(kp, carry):
        kt = kp * 2
        score_tile(jnp.minimum(kt + 1, last_tile), sc_b_ref, mt_b_ref)
        consume_tile(kt, sc_a_ref, mt_a_ref)
        score_tile(jnp.minimum(kt + 2, last_tile), sc_a_ref, mt_a_ref)
        consume_tile(jnp.minimum(kt + 1, last_tile), sc_b_ref, mt_b_ref)
        return carry
    lax.fori_loop(0, (n_sweep + 1) // 2, sel_step, 0)
    o_sel = acc_ref[...] / l_ref[...]

    def gate(branch):
        gr = gt_ref[pl.ds(branch * NSA_HEADS + g * NSA_GROUP, NSA_GROUP), :]
        return jnp.concatenate([gr[r:r + 1, :] for r in range(NSA_GROUP)], axis=1)
    o_t = gate(0) * o_cmp + gate(1) * o_sel + gate(2) * o_win
    outs = []
    for r in range(0, NSA_GROUP, 2):
        pair = jnp.concatenate([o_t[:, r * Q_TILE:(r + 1) * Q_TILE],
                                o_t[:, (r + 1) * Q_TILE:(r + 2) * Q_TILE]], axis=0)
        outs.append(pair.T)
    o_ref[0] = jnp.concatenate(outs, axis=1)


def _nsa_prompt(q, gt, kc_sum, vc_sum, ks, vs, kw, vw):
    b, t, _ = q.shape
    assert t % (2 * SEL_SWEEP) == 0 and t >= WINDOW + Q_TILE, t
    nt = t // Q_TILE
    nc = kc_sum.shape[1]
    ns = t // SEL_BLOCK
    rows = NSA_GROUP * Q_TILE
    per_b = lambda a: pl.BlockSpec((1,) + a.shape[1:], lambda bi, g, i: (bi, 0, 0))
    return pl.pallas_call(
        _nsa_prompt_kernel, grid=(b, NSA_KV_HEADS, nt),
        in_specs=[pl.BlockSpec((1, Q_TILE, NSA_WIDTH // NSA_KV_HEADS), lambda bi, g, i: (bi, i, g)),
                  pl.BlockSpec((GATE_ROWS, Q_TILE), lambda bi, g, i: (0, bi * nt + i)),
                  per_b(kc_sum), per_b(vc_sum), per_b(ks), per_b(vs), per_b(kw), per_b(vw)],
        out_specs=pl.BlockSpec((1, Q_TILE, NSA_WIDTH // NSA_KV_HEADS), lambda bi, g, i: (bi, i, g)),
        out_shape=jax.ShapeDtypeStruct((b, t, NSA_WIDTH), F32),
        scratch_shapes=[pltpu.VMEM((t, KV_WIDTH), BF16),
                        pltpu.VMEM((KV_WIDTH, t), BF16),
                        pltpu.VMEM((t, KV_WIDTH), BF16),
                        pltpu.VMEM((KV_WIDTH, t), BF16),
                        pltpu.VMEM((SUBLANES + nc, Q_TILE), F32),
                        pltpu.VMEM((ns, Q_TILE), F32),
                        pltpu.VMEM((ns, Q_TILE), F32),
                        pltpu.VMEM((ns, Q_TILE), F32),
                        pltpu.VMEM((1, rows), F32),
                        pltpu.VMEM((1, rows), F32),
                        pltpu.VMEM((NSA_HEAD_DIM, rows), F32),
                        pltpu.VMEM((SEL_SWEEP, rows), F32),
                        pltpu.VMEM((SEL_SWEEP, rows), F32),
                        pltpu.VMEM((1, rows), F32),
                        pltpu.VMEM((1, rows), F32)],
        compiler_params=_params("arbitrary", "arbitrary", "arbitrary"),
        name="nsa_prompt")(q, gt, kc_sum, vc_sum, ks, vs, kw, vw)


def _hgrn_prompt_kernel(hq_ref, hf_ref, hi_ref, hg_ref, lb_ref, ng_ref, o_ref, st_out_ref, st_ref):
    c = pl.program_id(1)
    n = HGRN_TILE
    sub = HGRN_CHUNK
    hd = HGRN_HEAD_DIM

    @pl.when(c == 0)
    def _():
        st_ref[...] = jnp.zeros(st_ref.shape, F32)

    pos = lax.broadcasted_iota(jnp.int32, (n, hd), 0) % sub
    rc = lax.broadcasted_iota(jnp.int32, (n, n), 0)
    cc = lax.broadcasted_iota(jnp.int32, (n, n), 1)
    intra = (rc // sub == cc // sub) & (cc <= rc)
    tok_chunk = lax.broadcasted_iota(jnp.int32, (hd, n), 1) // sub

    for h in range(HGRN_HEADS):
        cols = slice(h * hd, (h + 1) * hd)
        lb = lb_ref[:, cols]
        f = lb + (1.0 - lb) * jax.nn.sigmoid(hf_ref[0, :, cols])
        logf = jnp.log(f)
        b = logf
        suf = logf
        sh = 1
        while sh < sub:
            b = b + jnp.where(pos >= sh, pltpu.roll(b, sh, axis=0), 0.0)
            suf = suf + jnp.where(pos + sh < sub, pltpu.roll(suf, n - sh, axis=0), 0.0)
            sh *= 2
        hq = hq_ref[0, :, cols]
        k = 1.0 - f
        q_dec = (hq * jax.nn.sigmoid(hq) * jnp.exp(b)).astype(BF16)
        k_inv = (k * jnp.exp(-b)).astype(BF16)
        k_end = (k * jnp.exp(suf - logf)).astype(BF16)
        v = hi_ref[0, :, cols]

        a = jnp.where(intra, _nt_dot(q_dec, k_inv), 0.0)
        o = _dot(a.astype(BF16), v.astype(BF16))

        vt = v.T
        st = st_ref[h]
        inter = []
        for ci in range(n // sub):
            inter.append(_nt_dot(q_dec[ci * sub:(ci + 1) * sub, :], st.astype(BF16)))
            kv_t = _dot(jnp.where(tok_chunk == ci, vt, 0.0).astype(BF16), k_end)
            st = st * jnp.exp(suf[ci * sub:ci * sub + 1, :]) + kv_t
        st_ref[h] = st
        o = o + jnp.concatenate(inter, axis=0)
        o = o * lax.rsqrt(jnp.mean(o * o, axis=-1, keepdims=True) + LN_EPS) * ng_ref[...]
        hg = hg_ref[0, :, cols]
        o_ref[0, :, cols] = o * (hg * jax.nn.sigmoid(hg))

    @pl.when(c == pl.num_programs(1) - 1)
    def _():
        for h in range(HGRN_HEADS):
            st_out_ref[0, h] = st_ref[h].T


def _hgrn_prompt(h4, lb, norm_g, b, t):
    nchunk = t // HGRN_TILE
    part = lambda p: pl.BlockSpec((1, HGRN_TILE, HGRN_WIDTH), lambda bi, c: (bi, c, p))
    return pl.pallas_call(
        _hgrn_prompt_kernel, grid=(b, nchunk),
        in_specs=[part(0), part(1), part(2), part(3),
                  pl.BlockSpec((1, HGRN_WIDTH), lambda bi, c: (0, 0)),
                  pl.BlockSpec((1, HGRN_HEAD_DIM), lambda bi, c: (0, 0))],
        out_specs=[pl.BlockSpec((1, HGRN_TILE, HGRN_WIDTH), lambda bi, c: (bi, c, 0)),
                   pl.BlockSpec((1, HGRN_HEADS, HGRN_HEAD_DIM, HGRN_HEAD_DIM), lambda bi, c: (bi, 0, 0, 0))],
        out_shape=[jax.ShapeDtypeStruct((b, t, HGRN_WIDTH), F32),
                   jax.ShapeDtypeStruct((b, HGRN_HEADS, HGRN_HEAD_DIM, HGRN_HEAD_DIM), F32)],
        scratch_shapes=[pltpu.VMEM((HGRN_HEADS, HGRN_HEAD_DIM, HGRN_HEAD_DIM), F32)],
        compiler_params=_params("parallel", "arbitrary"),
        name="hgrn_prompt")(h4, h4, h4, h4, lb, norm_g)


def _layer_norm(y, g, b):
    mu = jnp.mean(y, axis=-1, keepdims=True)
    d = y - mu
    var = jnp.mean(d * d, axis=-1, keepdims=True)
    return d * lax.rsqrt(var + LN_EPS) * g + b


def _finish_kernel(x_ref, oa_ref, ob_ref, gab_ref, wba_ref, wbb_ref, wout_ref, g_ref, b_ref,
                   wr_hi_ref, wr_lo_ref, br_ref, x1_ref, x1b_ref, lt_ref):
    a = _dot(oa_ref[...].astype(BF16), wba_ref[...])
    bb = _dot(ob_ref[...].astype(BF16), wbb_ref[...])
    mix = jax.nn.sigmoid(gab_ref[:, :D_MODEL]) * a + jax.nn.sigmoid(gab_ref[:, D_MODEL:]) * bb
    y = DN_ALPHA * x_ref[...] + _dot(mix.astype(BF16), wout_ref[...])
    x1 = _layer_norm(y, g_ref[...], b_ref[...])
    x1_ref[...] = x1
    hi = x1.astype(BF16)
    x1b_ref[...] = hi
    lo = (x1 - hi.astype(F32)).astype(BF16)
    lt_ref[...] = (_nt_dot(wr_hi_ref[...], hi) + _nt_dot(wr_hi_ref[...], lo)
                   + _nt_dot(wr_lo_ref[...], hi) + br_ref[...])


def _finish(x, oa, ob, gab, wba, wbb, wout, g1, b1, wr_hi, wr_lo, br, tm):
    n = x.shape[0]
    row = lambda w: pl.BlockSpec((tm, w), lambda i: (i, 0))
    full = lambda a: pl.BlockSpec(a.shape, lambda i: (0,) * a.ndim)
    return pl.pallas_call(
        _finish_kernel, grid=(n // tm,),
        in_specs=[row(D_MODEL), row(NSA_WIDTH), row(HGRN_WIDTH), row(2 * D_MODEL),
                  full(wba), full(wbb), full(wout), full(g1), full(b1), full(wr_hi), full(wr_lo), full(br)],
        out_specs=[row(D_MODEL), row(D_MODEL), pl.BlockSpec((N_EXPERTS, tm), lambda i: (0, i))],
        out_shape=[jax.ShapeDtypeStruct((n, D_MODEL), F32), jax.ShapeDtypeStruct((n, D_MODEL), BF16),
                   jax.ShapeDtypeStruct((N_EXPERTS, n), F32)],
        compiler_params=_params("parallel"), name="merge_ln_router")(
            x, oa, ob, gab, wba, wbb, wout, g1, b1, wr_hi, wr_lo, br)


def _route_kernel(lt_ref, tri_ref, e_ref, w_ref, pos_ref, cnt_ref, carry_ref):
    i = pl.program_id(0)

    @pl.when(i == 0)
    def _():
        carry_ref[...] = jnp.zeros(carry_ref.shape, F32)

    logit = lt_ref[...]
    tn = logit.shape[1]
    eid = lax.broadcasted_iota(jnp.int32, (N_EXPERTS, tn), 0)
    rank = jnp.zeros((N_EXPERTS, tn), F32)
    for ep in range(N_EXPERTS):
        row = logit[ep:ep + 1, :]
        rank = rank + ((row > logit) | ((row == logit) & (ep < eid))).astype(F32)
    sel = rank < float(TOP_K)
    top = jnp.max(logit, axis=0, keepdims=True)
    ex = jnp.where(sel, jnp.exp(logit - top), 0.0)
    wgt = ex / jnp.sum(ex, axis=0, keepdims=True)
    self = sel.astype(F32)
    incl = _dot(self.astype(BF16), tri_ref[...])
    pos = carry_ref[:, 0:1] + incl - self
    carry_ref[...] = carry_ref[...] + jnp.sum(self, axis=1, keepdims=True)
    eid_f = eid.astype(F32)
    for kk in range(TOP_K):
        pick = sel & (rank == float(kk))
        e_ref[kk:kk + 1, :] = jnp.sum(jnp.where(pick, eid_f, 0.0), axis=0, keepdims=True).astype(jnp.int32)
        w_ref[kk:kk + 1, :] = jnp.sum(jnp.where(pick, wgt, 0.0), axis=0, keepdims=True)
        pos_ref[kk:kk + 1, :] = jnp.sum(jnp.where(pick, pos, 0.0), axis=0, keepdims=True).astype(jnp.int32)
    cnt_ref[...] = carry_ref[...]


def _route(logit_t):
    n = logit_t.shape[1]
    tn = ROUTE_TILE if n % ROUTE_TILE == 0 else n
    tri = (lax.broadcasted_iota(jnp.int32, (tn, tn), 0) <= lax.broadcasted_iota(jnp.int32, (tn, tn), 1)).astype(BF16)
    col = lambda r: pl.BlockSpec((r, tn), lambda i: (0, i))
    return pl.pallas_call(
        _route_kernel, grid=(n // tn,),
        in_specs=[col(N_EXPERTS), pl.BlockSpec((tn, tn), lambda i: (0, 0))],
        out_specs=[col(TOP_K), col(TOP_K), col(TOP_K), pl.BlockSpec((N_EXPERTS, LANES), lambda i: (0, 0))],
        out_shape=[jax.ShapeDtypeStruct((TOP_K, n), jnp.int32), jax.ShapeDtypeStruct((TOP_K, n), F32),
                   jax.ShapeDtypeStruct((TOP_K, n), jnp.int32), jax.ShapeDtypeStruct((N_EXPERTS, LANES), F32)],
        scratch_shapes=[pltpu.VMEM((N_EXPERTS, LANES), F32)],
        compiler_params=_params("arbitrary"), name="route")(logit_t, tri)


def _expert_kernel(blk_e_ref, nblk_ref, xb_ref, wgu_ref, bgu_ref, wd_ref, bd_ref, y_ref, wgu_b, wd_b):
    i = pl.program_id(0)
    live = i < nblk_ref[0]

    @pl.when(live & ((i == 0) | (blk_e_ref[i] != blk_e_ref[jnp.maximum(i - 1, 0)])))
    def _():
        wgu_b[...] = wgu_ref[0].astype(BF16)
        wd_b[...] = wd_ref[0].astype(BF16)

    @pl.when(live)
    def _():
        gu = _dot(xb_ref[...], wgu_b[...]) + bgu_ref[0]
        gate = jnp.minimum(gu[:, :D_EXPERT], SWIGLU_LIMIT)
        up = jnp.clip(gu[:, D_EXPERT:], -SWIGLU_LIMIT, SWIGLU_LIMIT)
        h = gate * jax.nn.sigmoid(SWIGLU_ALPHA * gate) * (up + 1.0)
        y_ref[...] = _dot(h.astype(BF16), wd_b[...]) + bd_ref[0]

    @pl.when(i >= nblk_ref[0])
    def _():
        y_ref[...] = jnp.zeros(y_ref.shape, F32)


def _experts(xb, blk_e, nblk, w_gu, b_gu, w_down, b_down):
    m = xb.shape[0]
    grid_spec = pltpu.PrefetchScalarGridSpec(
        num_scalar_prefetch=2, grid=(m // MOE_ROWS,),
        in_specs=[pl.BlockSpec((MOE_ROWS, D_MODEL), lambda i, be, nb: (i, 0)),
                  pl.BlockSpec((1, D_MODEL, 2 * D_EXPERT), lambda i, be, nb: (be[i], 0, 0)),
                  pl.BlockSpec((1, 1, 2 * D_EXPERT), lambda i, be, nb: (be[i], 0, 0)),
                  pl.BlockSpec((1, D_EXPERT, D_MODEL), lambda i, be, nb: (be[i], 0, 0)),
                  pl.BlockSpec((1, 1, D_MODEL), lambda i, be, nb: (be[i], 0, 0))],
        out_specs=pl.BlockSpec((MOE_ROWS, D_MODEL), lambda i, be, nb: (i, 0)),
        scratch_shapes=[pltpu.VMEM((D_MODEL, 2 * D_EXPERT), BF16), pltpu.VMEM((D_EXPERT, D_MODEL), BF16)])
    return pl.pallas_call(
        _expert_kernel, grid_spec=grid_spec,
        out_shape=jax.ShapeDtypeStruct((m, D_MODEL), F32),
        compiler_params=_params("arbitrary"), name="experts")(
            blk_e, nblk, xb, w_gu, b_gu.reshape(N_EXPERTS, 1, 2 * D_EXPERT),
            w_down, b_down.reshape(N_EXPERTS, 1, D_MODEL))


def _final_ln_kernel(x1_ref, y_ref, g_ref, b_ref, o_ref):
    o_ref[...] = _layer_norm(DN_ALPHA * x1_ref[...] + y_ref[...], g_ref[...], b_ref[...])


def _final_ln(x1, y, g2, b2, tm):
    n = x1.shape[0]
    row = pl.BlockSpec((tm, D_MODEL), lambda i: (i, 0))
    vec = pl.BlockSpec((1, D_MODEL), lambda i: (0, 0))
    return pl.pallas_call(
        _final_ln_kernel, grid=(n // tm,), in_specs=[row, row, vec, vec], out_specs=row,
        out_shape=jax.ShapeDtypeStruct((n, D_MODEL), F32),
        compiler_params=_params("parallel"), name="final_ln")(x1, y, g2, b2)


def _moe_and_norm(x1, x1b, logit_t, w_gu, b_gu, w_down, b_down, g2, b2, tm):
    n = x1.shape[0]
    top_e, top_w, top_pos, counts = _route(logit_t)
    counts = counts[:, 0].astype(jnp.int32)
    padded = (counts + MOE_ROWS - 1) // MOE_ROWS * MOE_ROWS
    pad_end = jnp.cumsum(padded)
    pad_start = pad_end - padded
    n_blocks = -(-(n * TOP_K + N_EXPERTS * (MOE_ROWS - 1)) // MOE_ROWS)
    dest = pad_start[top_e] + top_pos
    tok = jnp.broadcast_to(jnp.arange(n, dtype=jnp.int32)[None, :], dest.shape)
    slot_tok = jnp.zeros((n_blocks * MOE_ROWS,), jnp.int32).at[dest.reshape(-1)].set(tok.reshape(-1))
    xb = x1b[slot_tok]
    blk_e = jnp.minimum(jnp.searchsorted(pad_end, jnp.arange(n_blocks, dtype=jnp.int32) * MOE_ROWS, side='right'),
                        N_EXPERTS - 1).astype(jnp.int32)
    nblk = (pad_end[-1:] // MOE_ROWS).astype(jnp.int32)
    yb = _experts(xb, blk_e, nblk, w_gu, b_gu, w_down, b_down)
    y = jnp.zeros((n, D_MODEL), F32)
    for kk in range(TOP_K):
        y = y + yb[dest[kk]] * top_w[kk][:, None]
    return _final_ln(x1, y, g2, b2, tm)


def _page_copy(cache_hbm, table_ref, bi, p, buf_ref, slot, c, sem_ref):
    col = pl.multiple_of(p * PAGE_SIZE, PAGE_SIZE)
    return pltpu.make_async_copy(cache_hbm.at[table_ref[bi, p]],
                                 buf_ref.at[slot, c, :, pl.ds(col, PAGE_SIZE)], sem_ref.at[slot, c])


def _fetch_pages(caches, table_ref, buf_ref, sem_ref):
    b = pl.program_id(0)
    n_pages = table_ref.shape[1]
    slot = b % 2

    def start(bi, sl):
        def body(p, carry):
            for c, cache in enumerate(caches):
                _page_copy(cache, table_ref, bi, p, buf_ref, sl, c, sem_ref).start()
            return carry
        lax.fori_loop(0, n_pages, body, 0)

    @pl.when(b == 0)
    def _():
        start(0, 0)

    @pl.when(b + 1 < pl.num_programs(0))
    def _():
        start(b + 1, 1 - slot)

    def wait(p, carry):
        for c, cache in enumerate(caches):
            _page_copy(cache, table_ref, b, p, buf_ref, slot, c, sem_ref).wait()
        return carry
    lax.fori_loop(0, n_pages, wait, 0)
    return slot


def _cmp_pages_kernel(table_ref, kcache, vcache, knew_ref, vnew_ref,
                      w1k_ref, bk_ref, w2k_ref, b2k_ref, w1v_ref, bv_ref, w2v_ref, b2v_ref,
                      ko_ref, vo_ref, buf_ref, rows_ref, sem_ref):
    slot = _fetch_pages((kcache, vcache), table_ref, buf_ref, sem_ref)
    n_pages = table_ref.shape[1]
    past = n_pages * PAGE_SIZE
    n_rows = rows_ref.shape[0]
    j = n_rows // CMP_STRIDE
    first_row = lax.broadcasted_iota(jnp.int32, (n_rows - past, KV_WIDTH), 0) == 0
    plan = ((knew_ref, w1k_ref, bk_ref, w2k_ref, b2k_ref, ko_ref),
            (vnew_ref, w1v_ref, bv_ref, w2v_ref, b2v_ref, vo_ref))
    for c, (new_ref, w1_ref, bias_ref, w2_ref, b2_ref, o_ref) in enumerate(plan):
        def to_rows(p, carry):
            off = pl.multiple_of(p * PAGE_SIZE, PAGE_SIZE)
            rows_ref[pl.ds(off, PAGE_SIZE), :] = buf_ref[slot, c, :, pl.ds(off, PAGE_SIZE)].T
            return carry
        lax.fori_loop(0, n_pages, to_rows, 0)
        rows_ref[past:, :] = jnp.where(first_row, new_ref[0], 0.0)
        f = None
        for p in range(CMP_STRIDE):
            x = rows_ref[pl.ds(p, j, stride=CMP_STRIDE), :].astype(BF16)
            part = _dot(x, w1_ref[p * KV_WIDTH:(p + 1) * KV_WIDTH, :])
            f = part if f is None else f + part
        o_ref[0, 0:j, :] = _compress_tail(f, bias_ref, w2_ref, b2_ref)
        o_ref[0, j:, :] = jnp.zeros((o_ref.shape[1] - j, KV_WIDTH), F32)


def _cmp_pages(table, kcache, vcache, knew, vnew, cw_k, cw_v):
    bs, n_pages = table.shape
    past = n_pages * PAGE_SIZE
    n_rows = past + SUBLANES * CMP_STRIDE
    j = n_rows // CMP_STRIDE
    jp = -(-j // LANES) * LANES
    full = lambda a: pl.BlockSpec(a.shape, lambda b, tbl: (0,) * a.ndim)
    new = pl.BlockSpec((1, 1, KV_WIDTH), lambda b, tbl: (b, 0, 0))
    out = pl.BlockSpec((1, jp, KV_WIDTH), lambda b, tbl: (b, 0, 0))
    grid_spec = pltpu.PrefetchScalarGridSpec(
        num_scalar_prefetch=1, grid=(bs,),
        in_specs=[pl.BlockSpec(memory_space=pl.ANY), pl.BlockSpec(memory_space=pl.ANY), new, new]
        + [full(a) for a in cw_k] + [full(a) for a in cw_v],
        out_specs=[out, out],
        scratch_shapes=[pltpu.VMEM((2, 2, KV_WIDTH, past), F32), pltpu.VMEM((n_rows, KV_WIDTH), F32),
                        pltpu.SemaphoreType.DMA((2, 2))])
    return pl.pallas_call(
        _cmp_pages_kernel, grid_spec=grid_spec,
        out_shape=[jax.ShapeDtypeStruct((bs, jp, KV_WIDTH), F32)] * 2,
        compiler_params=_params("arbitrary"), name="cmp_pages")(
            table, kcache, vcache, knew.reshape(bs, 1, KV_WIDTH), vnew.reshape(bs, 1, KV_WIDTH), *cw_k, *cw_v)


def _nsa_sample_kernel(table_ref, kcache, vcache, q8_ref, gcol_ref, kc_ref, vc_ref, kwt_ref, vwt_ref,
                       ksn_ref, vsn_ref, kwn_ref, vwn_ref, kwc_ref, vwc_ref,
                       o_ref, kwo_ref, vwo_ref, buf_ref, expand_ref, imp_ref, sem_ref):
    b = pl.program_id(0)
    slot = _fetch_pages((kcache, vcache), table_ref, buf_ref, sem_ref)
    past = table_ref.shape[1] * PAGE_SIZE
    q_pos = past
    sel_len = -(-(past + 1) // SEL_BLOCK) * SEL_BLOCK
    nc = sel_len // CMP_STRIDE - 1
    ns = sel_len // SEL_BLOCK
    ncp = kc_ref.shape[1]
    nsp = -(-ns // SUBLANES) * SUBLANES
    nsq = -(-nsp // LANES) * LANES
    nbp = expand_ref.shape[0]
    rows8 = NSA_HEADS
    cur = q_pos // SEL_BLOCK
    n_top = min(SEL_TOP, ns)

    @pl.when(b == 0)
    def _():
        blk_i = lax.broadcasted_iota(jnp.int32, (nbp, past), 0)
        key_i = lax.broadcasted_iota(jnp.int32, (nbp, past), 1)
        expand_ref[...] = (blk_i == key_i // SEL_BLOCK).astype(BF16)

    q8 = q8_ref[0] * (NSA_HEAD_DIM ** -0.5 * math.log2(math.e))
    q8b = q8.astype(BF16)

    st = _nt_dot(kc_ref[0].astype(BF16), q8b)
    n_idx = lax.broadcasted_iota(jnp.int32, (ncp, rows8), 0)
    st = jnp.where((n_idx * CMP_STRIDE + (CMP_BLOCK - 1) <= q_pos) & (n_idx < nc), st, NEG_INF)
    mc = jnp.max(st, axis=0, keepdims=True)
    mc = jnp.where(mc > NEG_INF, mc, 0.0)
    ec = jnp.exp2(st - mc)
    pt = ec / jnp.maximum(jnp.sum(ec, axis=0, keepdims=True), 1e-30)
    o_cmp_t = _dot(vc_ref[0].T.astype(BF16), pt.astype(BF16))
    o_cmp = jnp.concatenate([o_cmp_t, jnp.zeros((KV_WIDTH, LANES - rows8), F32)], axis=1).T[0:rows8, :]

    imp = jnp.concatenate([jnp.sum(pt[:, g * NSA_GROUP:(g + 1) * NSA_GROUP], axis=1, keepdims=True)
                           for g in range(NSA_KV_HEADS)], axis=1)
    imp_ref[0:SUBLANES, :] = jnp.zeros((SUBLANES, NSA_KV_HEADS), F32)
    imp_ref[SUBLANES:SUBLANES + ncp, :] = imp
    ratio = SEL_BLOCK // CMP_STRIDE
    p_slc = imp_ref[pl.ds(SUBLANES - 1, nsp, stride=ratio), :]
    for d in range(ratio):
        p_slc = p_slc + imp_ref[pl.ds(SUBLANES + d, nsp, stride=ratio), :]
    blk = lax.broadcasted_iota(jnp.int32, (nsp, NSA_KV_HEADS), 0)
    forced = (blk == 0) | (blk > cur - N_LOCAL_SEL)
    score = jnp.where(blk <= cur, p_slc + jnp.where(forced, FORCE_BONUS, 0.0), -FORCE_BONUS)
    low = -2.0 * FORCE_BONUS
    score_pad = jnp.concatenate([score, jnp.full((nsp, LANES - NSA_KV_HEADS), low, F32)], axis=1)
    score_pad = jnp.concatenate([score_pad, jnp.full((nsq - nsp, LANES), low, F32)], axis=0)
    score_rows = score_pad.T
    jp = lax.broadcasted_iota(jnp.int32, (nsp, nsq), 0)
    jj = lax.broadcasted_iota(jnp.int32, (nsp, nsq), 1)
    sel_rows = []
    for g in range(NSA_KV_HEADS):
        col = score[:, g:g + 1]
        row = score_rows[g:g + 1, :]
        beats = (col > row) | ((col == row) & (jp < jj))
        rank = jnp.sum(beats.astype(F32), axis=0, keepdims=True)
        sel_rows.append(((rank < float(n_top)) & (jj[0:1, :] <= cur)).astype(F32))
    row_head = lax.broadcasted_iota(jnp.int32, (rows8, nsq), 0) // NSA_GROUP
    sel8 = jnp.where(row_head == 0, sel_rows[0], sel_rows[1])

    chosen = _dot(sel8[:, 0:nbp].astype(BF16), expand_ref[...])
    kt16 = buf_ref[slot, 0].astype(BF16)
    vt16 = buf_ref[slot, 1].astype(BF16)
    s = _dot(q8b, kt16) + jnp.where(chosen > 0.5, 0.0, NEG_INF)
    s_new = jnp.sum(q8 * ksn_ref[0], axis=1, keepdims=True)
    m = jnp.maximum(jnp.max(s, axis=1, keepdims=True), s_new)
    p = jnp.exp2(s - m)
    p_new = jnp.exp2(s_new - m)
    o_sel = ((_nt_dot(p.astype(BF16), vt16) + p_new * vsn_ref[0])
             / (jnp.sum(p, axis=1, keepdims=True) + p_new))

    kwt = kwt_ref[0]
    vwt = vwt_ref[0]
    w = kwt.shape[1]
    lane = lax.broadcasted_iota(jnp.int32, (rows8, w), 1)
    w_pos = past - w + lane
    sw = jnp.where((w_pos > q_pos - WINDOW) & (w_pos >= 0), _dot(q8b, kwt.astype(BF16)), NEG_INF)
    sw_new = jnp.sum(q8 * kwn_ref[0], axis=1, keepdims=True)
    mw = jnp.maximum(jnp.max(sw, axis=1, keepdims=True), sw_new)
    pw = jnp.exp2(sw - mw)
    pw_new = jnp.exp2(sw_new - mw)
    o_win = ((_nt_dot(pw.astype(BF16), vwt.astype(BF16)) + pw_new * vwn_ref[0])
             / (jnp.sum(pw, axis=1, keepdims=True) + pw_new))

    gates = gcol_ref[0]
    o_ref[0] = gates[:, 0:1] * o_cmp + gates[:, 1:2] * o_sel + gates[:, 2:3] * o_win

    last = lax.broadcasted_iota(jnp.int32, (KV_WIDTH, w), 1) == w - 1
    kwo_ref[0] = jnp.where(last, kwc_ref[0], pltpu.roll(kwt, w - 1, axis=1))
    vwo_ref[0] = jnp.where(last, vwc_ref[0], pltpu.roll(vwt, w - 1, axis=1))


def _nsa_sample(table, kcache, vcache, q8, gcol, kc_sum, vc_sum, kwt, vwt, ks_new, vs_new, kw_new, vw_new):
    bs, n_pages = table.shape
    past = n_pages * PAGE_SIZE
    w = kwt.shape[2]
    ncp = kc_sum.shape[1]
    nbp = -(-(past // SEL_BLOCK) // LANES) * LANES
    per_b = lambda a: pl.BlockSpec((1,) + a.shape[1:], lambda b, tbl: (b,) + (0,) * (a.ndim - 1))
    row = lambda a: a.reshape(bs, 1, KV_WIDTH)
    col = lambda a: a.reshape(bs, KV_WIDTH, 1)
    operands = (q8, gcol, kc_sum, vc_sum, kwt, vwt, row(ks_new), row(vs_new), row(kw_new), row(vw_new),
                col(kw_new), col(vw_new))
    grid_spec = pltpu.PrefetchScalarGridSpec(
        num_scalar_prefetch=1, grid=(bs,),
        in_specs=[pl.BlockSpec(memory_space=pl.ANY), pl.BlockSpec(memory_space=pl.ANY)]
        + [per_b(a) for a in operands],
        out_specs=[pl.BlockSpec((1, NSA_HEADS, KV_WIDTH), lambda b, tbl: (b, 0, 0)),
                   pl.BlockSpec((1, KV_WIDTH, w), lambda b, tbl: (b, 0, 0)),
                   pl.BlockSpec((1, KV_WIDTH, w), lambda b, tbl: (b, 0, 0))],
        scratch_shapes=[pltpu.VMEM((2, 2, KV_WIDTH, past), F32), pltpu.VMEM((nbp, past), BF16),
                        pltpu.VMEM((SUBLANES + ncp, NSA_KV_HEADS), F32), pltpu.SemaphoreType.DMA((2, 2))])
    return pl.pallas_call(
        _nsa_sample_kernel, grid_spec=grid_spec,
        out_shape=[jax.ShapeDtypeStruct((bs, NSA_HEADS, KV_WIDTH), F32),
                   jax.ShapeDtypeStruct((bs, KV_WIDTH, w), F32), jax.ShapeDtypeStruct((bs, KV_WIDTH, w), F32)],
        compiler_params=_params("arbitrary"), name="nsa_sample")(table, kcache, vcache, *operands)


def _hgrn_sample(hq, hf, hi, hg, s0, lb, norm_g):
    b = hq.shape[0]
    shp = (b, HGRN_HEADS, HGRN_HEAD_DIM)
    f = (lb + (1.0 - lb) * jax.nn.sigmoid(hf)).reshape(shp)
    q = jax.nn.silu(hq).reshape(shp)
    k = 1.0 - f
    v = hi.reshape(shp)
    s_new = f[..., None] * s0 + k[..., None] * v[:, :, None, :]
    o = jnp.einsum('bhd,bhdv->bhv', q, s_new, precision=lax.Precision.HIGHEST)
    o = o * lax.rsqrt(jnp.mean(jnp.square(o), axis=-1, keepdims=True) + LN_EPS) * norm_g
    return o.reshape(b, HGRN_WIDTH) * jax.nn.silu(hg), s_new


def kernel(x_prompt, x_sample, cache_k_cmp, cache_v_cmp, cache_k_sel, cache_v_sel, cache_k_win, cache_v_win, state_hgrn, page_table, w_in, cmp_pe, cmp_w1, cmp_b1, cmp_w2, cmp_b2, hgrn_gamma, hgrn_norm, w_branch_a, w_branch_b, w_out, ln1_g, ln1_b, w_router, b_router, w_gate_up, b_gate_up, w_down, b_down, ln2_g, ln2_b):
    l = 0
    bp, t, _ = x_prompt.shape
    bs = x_sample.shape[0]
    past = page_table.shape[1] * PAGE_SIZE
    win_keep = cache_k_win.shape[2]

    lower = jnp.cumsum(jax.nn.softmax(hgrn_gamma.astype(F32), axis=0), axis=0)[l][None, :]
    norm_g = hgrn_norm[l][None, :]
    w = w_in[l]
    c_g = NSA_WIDTH + 6 * KV_WIDTH
    n_g = 3 * NSA_HEADS
    w_main = jnp.concatenate([w[:, :c_g], w[:, c_g + n_g:]], axis=1).astype(BF16)
    w_gate_t = jnp.pad(w[:, c_g:c_g + n_g].T, ((0, GATE_ROWS - n_g), (0, 0))).astype(BF16)
    cw = [_compress_weights(cmp_pe[l, n], cmp_w1[l, n], cmp_b1[l, n], cmp_w2[l, n], cmp_b2[l, n]) for n in range(2)]
    wba = w_branch_a[l].astype(BF16)
    wbb = w_branch_b[l].astype(BF16)
    wout = w_out[l].astype(BF16)
    wr_t = w_router[l].T
    wr_hi = wr_t.astype(BF16)
    wr_lo = (wr_t - wr_hi.astype(F32)).astype(BF16)
    br = jnp.broadcast_to(b_router[l].astype(F32)[:, None], (N_EXPERTS, 1))
    g1, b1 = ln1_g[l][None, :], ln1_b[l][None, :]
    g2, b2 = ln2_g[l][None, :], ln2_b[l][None, :]

    def finish(x2d, o_a, o_b, gab, tm):
        x1, x1b, logit_t = _finish(x2d, o_a, o_b, gab, wba, wbb, wout, g1, b1, wr_hi, wr_lo, br, tm)
        return _moe_and_norm(x1, x1b, logit_t, w_gate_up[l], b_gate_up[l], w_down[l], b_down[l], g2, b2, tm)

    n = bp * t
    xp = x_prompt.reshape(n, D_MODEL)
    q, kc, vc, ks, vs, kw, vw, h4, gab, gt = _project(xp, w_main, w_gate_t, 256)
    r3 = lambda a: a.reshape(bp, t, a.shape[-1])
    kc_sum = _compress(r3(kc), *cw[0])
    vc_sum = _compress(r3(vc), *cw[1])
    o_nsa = _nsa_prompt(r3(q), gt, kc_sum, vc_sum, r3(ks), r3(vs), r3(kw), r3(vw))
    o_hgrn, p_state = _hgrn_prompt(r3(h4), lower, norm_g, bp, t)
    y_prompt = finish(xp, o_nsa.reshape(n, NSA_WIDTH), o_hgrn.reshape(n, HGRN_WIDTH), gab, 256).reshape(bp, t, D_MODEL)
    r5 = lambda a: a.reshape(1, bp, t, NSA_KV_HEADS, NSA_HEAD_DIM)
    win = min(WINDOW, t)
    new_p = (r5(kc), r5(vc), r5(ks), r5(vs), r5(kw)[:, :, t - win:], r5(vw)[:, :, t - win:], p_state[None])

    xs = x_sample.reshape(bs, D_MODEL)
    q, kc, vc, ks, vs, kw, vw, h4, gab, gt = _project(xs, w_main, w_gate_t, bs)
    pages = lambda c: jnp.transpose(c[l], (0, 2, 3, 1)).reshape(c.shape[1], KV_WIDTH, PAGE_SIZE)
    band = lambda c: jnp.transpose(c[l], (0, 2, 3, 1)).reshape(bs, KV_WIDTH, win_keep)
    kc_sum, vc_sum = _cmp_pages(page_table, pages(cache_k_cmp), pages(cache_v_cmp), kc, vc, cw[0], cw[1])
    head_eye = jnp.eye(NSA_KV_HEADS, dtype=F32)
    q8 = jnp.einsum('bgrd,gh->bgrhd', q.reshape(bs, NSA_KV_HEADS, NSA_GROUP, NSA_HEAD_DIM),
                    head_eye).reshape(bs, NSA_HEADS, KV_WIDTH)
    gcol = jnp.pad(gt[:3 * NSA_HEADS].T.reshape(bs, 3, NSA_HEADS).transpose(0, 2, 1),
                   ((0, 0), (0, 0), (0, LANES - 3)))
    o8, kw_t, vw_t = _nsa_sample(page_table, pages(cache_k_sel), pages(cache_v_sel), q8, gcol, kc_sum, vc_sum,
                                 band(cache_k_win), band(cache_v_win), ks, vs, kw, vw)
    o_nsa = jnp.einsum('bgrhd,gh->bgrd', o8.reshape(bs, NSA_KV_HEADS, NSA_GROUP, NSA_KV_HEADS, NSA_HEAD_DIM),
                       head_eye).reshape(bs, NSA_WIDTH)
    hw = HGRN_WIDTH
    o_hgrn, s_state = _hgrn_sample(h4[:, :hw], h4[:, hw:2 * hw], h4[:, 2 * hw:3 * hw], h4[:, 3 * hw:],
                                   state_hgrn[l], lower, norm_g)
    y_sample = finish(xs, o_nsa, o_hgrn, gab, bs).reshape(bs, 1, D_MODEL)
    s5 = lambda a: a.reshape(1, bs, 1, NSA_KV_HEADS, NSA_HEAD_DIM)
    w5 = lambda a: jnp.transpose(a.reshape(bs, NSA_KV_HEADS, NSA_HEAD_DIM, win_keep), (0, 3, 1, 2))[None]
    new_s = (s5(kc), s5(vc), s5(ks), s5(vs), w5(kw_t), w5(vw_t), s_state[None])

    return (y_prompt, y_sample) + new_p + new_s
```

```python
import functools
import math

import jax
import jax.numpy as jnp
from jax import lax
from jax.experimental import pallas as pl
from jax.experimental.pallas import tpu as pltpu

F32 = jnp.float32
BF16 = jnp.bfloat16

D_MODEL = 1024
PAGE_SIZE = 128
NSA_HEADS = 8
NSA_KV_HEADS = 2
NSA_GROUP = NSA_HEADS // NSA_KV_HEADS
NSA_HEAD_DIM = 64
NSA_WIDTH = NSA_HEADS * NSA_HEAD_DIM
KV_WIDTH = NSA_KV_HEADS * NSA_HEAD_DIM
CMP_BLOCK = 32
CMP_STRIDE = 16
CMP_HIDDEN = 128
SEL_BLOCK = 64
SEL_TOP = 16
N_LOCAL_SEL = 2
FORCE_BONUS = 1.0e4
WINDOW = 512
HGRN_HEADS = 4
HGRN_HEAD_DIM = 128
HGRN_WIDTH = HGRN_HEADS * HGRN_HEAD_DIM
HGRN_CHUNK = 16
N_EXPERTS = 32
TOP_K = 4
D_EXPERT = 1024
SWIGLU_LIMIT = 7.0
SWIGLU_ALPHA = 1.702
DEPTH = 1
DN_ALPHA = (2 * DEPTH) ** 0.25
LN_EPS = 1e-5

LANES = 128
SUBLANES = 8
VMEM_BYTES_V7X = 64 * 1024 * 1024
VMEM_LIMIT = VMEM_BYTES_V7X * 3 // 4

Q_TILE = 128
K_TILE = 128
SEL_SWEEP = 512
HGRN_TILE = 128
MOE_ROWS = 512
ROUTE_TILE = 256
NEG_INF = float("-inf")

_C_Q = 0
_C_KV = _C_Q + NSA_WIDTH
_C_H = _C_KV + 6 * KV_WIDTH
_C_GAB = _C_H + 4 * HGRN_WIDTH
_C_END = _C_GAB + 2 * D_MODEL
GATE_ROWS = 32


def _params(*sem):
    return pltpu.CompilerParams(dimension_semantics=sem, vmem_limit_bytes=VMEM_LIMIT)


def _nt_dot(a, b):
    return lax.dot_general(a, b, (((1,), (1,)), ((), ())), preferred_element_type=F32)


def _dot(a, b):
    return jnp.dot(a, b, preferred_element_type=F32)


def _proj_kernel(x_ref, w_ref, wg_ref, q_ref, kc_ref, vc_ref, ks_ref, vs_ref, kw_ref, vw_ref,
                 h_ref, gab_ref, gt_ref, *kvt_refs):
    x = x_ref[...].astype(BF16)
    q_ref[...] = _dot(x, w_ref[:, _C_Q:_C_KV])
    kv = _dot(x, w_ref[:, _C_KV:_C_H])
    for n, ref in enumerate((kc_ref, vc_ref, ks_ref, vs_ref, kw_ref, vw_ref)):
        ref[...] = kv[:, n * KV_WIDTH:(n + 1) * KV_WIDTH]
    for n, ref in enumerate(kvt_refs):
        for c in range(0, kv.shape[0], LANES):
            ref[0, :, c:c + LANES] = kv[c:c + LANES, n * KV_WIDTH:(n + 1) * KV_WIDTH].T
    h_ref[...] = _dot(x, w_ref[:, _C_H:_C_GAB])
    gab_ref[...] = _dot(x, w_ref[:, _C_GAB:_C_END])
    gt_ref[...] = jax.nn.sigmoid(_nt_dot(wg_ref[...], x))


def _project(x, w_main, w_gate_t, tm, batch):
    n = x.shape[0]
    per_b = n // batch // tm
    row = lambda w: pl.BlockSpec((tm, w), lambda i: (i, 0))
    full = lambda a: pl.BlockSpec(a.shape, lambda i: (0,) * a.ndim)
    out_shape = ([jax.ShapeDtypeStruct((n, NSA_WIDTH), F32)]
                 + [jax.ShapeDtypeStruct((n, KV_WIDTH), F32)] * 6
                 + [jax.ShapeDtypeStruct((n, 4 * HGRN_WIDTH), F32),
                    jax.ShapeDtypeStruct((n, 2 * D_MODEL), F32),
                    jax.ShapeDtypeStruct((GATE_ROWS, n), F32)]
                 + [jax.ShapeDtypeStruct((batch, KV_WIDTH, n // batch), F32)] * 6)
    out_specs = ([row(NSA_WIDTH)] + [row(KV_WIDTH)] * 6 + [row(4 * HGRN_WIDTH), row(2 * D_MODEL),
                 pl.BlockSpec((GATE_ROWS, tm), lambda i: (0, i))]
                 + [pl.BlockSpec((1, KV_WIDTH, tm), lambda i: (i // per_b, 0, i % per_b))] * 6)
    return pl.pallas_call(
        _proj_kernel, grid=(n // tm,),
        in_specs=[row(D_MODEL), full(w_main), full(w_gate_t)],
        out_specs=out_specs, out_shape=out_shape,
        compiler_params=_params("parallel"), name="in_proj")(x, w_main, w_gate_t)


def _gelu_tanh(x):
    return 0.5 * x * (1.0 + jnp.tanh(math.sqrt(2.0 / math.pi) * (x + 0.044715 * (x * x * x))))


def _compress_tail(f, bias_ref, w2_ref, b2_ref):
    j = f.shape[0]
    outs = []
    for g in range(NSA_KV_HEADS):
        base = g * 2 * CMP_HIDDEN
        first = f[:, base:base + CMP_HIDDEN]
        second = f[:, base + CMP_HIDDEN:base + 2 * CMP_HIDDEN]
        nxt = pltpu.roll(second, j - 1, axis=0)
        h = _gelu_tanh(first + nxt + bias_ref[g:g + 1, :])
        outs.append(_dot(h.astype(BF16), w2_ref[...]) + b2_ref[...])
    return jnp.concatenate(outs, axis=1)


def _compress_kernel(c_ref, w1_ref, bias_ref, w2_ref, b2_ref, o_ref):
    c = c_ref[0].astype(BF16)
    o_ref[0] = _compress_tail(_dot(c, w1_ref[...]), bias_ref, w2_ref, b2_ref)


def _compress(rows, w1full, bias, w2, b2):
    b, t, _ = rows.shape
    j = t // CMP_STRIDE
    c = rows.reshape(b, j, CMP_STRIDE * KV_WIDTH)
    full = lambda a: pl.BlockSpec(a.shape, lambda i: (0,) * a.ndim)
    return pl.pallas_call(
        _compress_kernel, grid=(b,),
        in_specs=[pl.BlockSpec((1, j, CMP_STRIDE * KV_WIDTH), lambda i: (i, 0, 0)),
                  full(w1full), full(bias), full(w2), full(b2)],
        out_specs=pl.BlockSpec((1, j, KV_WIDTH), lambda i: (i, 0, 0)),
        out_shape=jax.ShapeDtypeStruct((b, j, KV_WIDTH), F32),
        compiler_params=_params("parallel"), name="compress")(c, w1full, bias, w2, b2)


def _compress_weights(pe, w1, b1, w2, b2):
    hd, hid = NSA_HEAD_DIM, CMP_HIDDEN
    halves = w1.reshape(2, CMP_STRIDE, hd, hid)
    eye = jnp.eye(NSA_KV_HEADS, dtype=w1.dtype)
    w1full = jnp.einsum('apdh,kg->pkdgah', halves, eye).reshape(
        CMP_STRIDE * KV_WIDTH, NSA_KV_HEADS * 2 * hid).astype(BF16)
    bias = jnp.einsum('pd,pdh->h', pe, w1, precision=lax.Precision.HIGHEST) + b1
    bias = jnp.broadcast_to(bias[None, :], (SUBLANES, hid))
    return w1full, bias, w2.astype(BF16), b2.reshape(1, hd)


def _nsa_prompt_kernel(q_ref, gt_ref, kc_ref, vc_ref, ks_ref, vs_ref, kw_ref, vw_ref, o_ref,
                       ks16_ref, vst16_ref, kw16_ref, vwt16_ref, imp_ref, score_ref, selb_ref, seloff_ref,
                       m_ref, l_ref, acc_ref, sc_a_ref, sc_b_ref, mt_a_ref, mt_b_ref):
    g = pl.program_id(1)
    i = pl.program_id(2)
    t0 = pl.multiple_of(i * Q_TILE, Q_TILE)
    rows = NSA_GROUP * Q_TILE
    nc = kc_ref.shape[1]
    ns = score_ref.shape[0]
    t_len = ks_ref.shape[1]
    hd = NSA_HEAD_DIM

    @pl.when((g == 0) & (i == 0))
    def _():
        def cast_step(c, carry):
            r0 = pl.multiple_of(c * K_TILE, K_TILE)
            ks16_ref[pl.ds(r0, K_TILE), :] = ks_ref[0, pl.ds(r0, K_TILE), :].astype(BF16)
            kw16_ref[pl.ds(r0, K_TILE), :] = kw_ref[0, pl.ds(r0, K_TILE), :].astype(BF16)
            vst16_ref[:, pl.ds(r0, K_TILE)] = vs_ref[0, pl.ds(r0, K_TILE), :].T.astype(BF16)
            vwt16_ref[:, pl.ds(r0, K_TILE)] = vw_ref[0, pl.ds(r0, K_TILE), :].T.astype(BF16)
            return carry
        lax.fori_loop(0, t_len // K_TILE, cast_step, 0)

    q = q_ref[0] * (hd ** -0.5 * math.log2(math.e))
    lane_head = lax.broadcasted_iota(jnp.int32, (Q_TILE, KV_WIDTH), 1) // NSA_HEAD_DIM
    parts = []
    for r in range(NSA_GROUP):
        qr = q[:, r * NSA_HEAD_DIM:(r + 1) * NSA_HEAD_DIM]
        parts.append(jnp.where(lane_head == g, jnp.concatenate([qr, qr], axis=1), 0.0))
    qs = jnp.concatenate(parts, axis=0).astype(BF16)

    tok = t0 + lax.broadcasted_iota(jnp.int32, (1, Q_TILE), 1)
    tok4 = jnp.concatenate([tok] * NSA_GROUP, axis=1)

    s = _nt_dot(kc_ref[0].astype(BF16), qs)
    n_idx = lax.broadcasted_iota(jnp.int32, (nc, rows), 0)
    ok = (n_idx * CMP_STRIDE + (CMP_BLOCK - 1) <= tok4) & (n_idx < nc - 1)
    s = jnp.where(ok, s, NEG_INF)
    m = jnp.max(s, axis=0, keepdims=True)
    m = jnp.where(m > NEG_INF, m, 0.0)
    e = jnp.exp2(s - m)
    p = e / jnp.maximum(jnp.sum(e, axis=0, keepdims=True), 1e-30)
    o_cmp = _dot(vc_ref[0].T.astype(BF16), p.astype(BF16))
    o_cmp = jnp.where(g == 0, o_cmp[:hd], o_cmp[hd:])

    def tile4(bias):
        return jnp.concatenate([bias] * NSA_GROUP, axis=1)

    head_rows = pl.ds(pl.multiple_of(g * hd, hd), hd)

    wk = WINDOW + Q_TILE
    w0 = pl.multiple_of(jnp.maximum(t0 - WINDOW, 0), Q_TILE)
    key_w = w0 + lax.broadcasted_iota(jnp.int32, (wk, Q_TILE), 0)
    band = jnp.where((key_w <= tok) & (key_w > tok - WINDOW), 0.0, NEG_INF)
    sw = _nt_dot(kw16_ref[pl.ds(w0, wk), :], qs) + tile4(band)
    pw = jnp.exp2(sw - jnp.max(sw, axis=0, keepdims=True))
    o_win = (_dot(vwt16_ref[head_rows, pl.ds(w0, wk)], pw.astype(BF16))
             / jnp.sum(pw, axis=0, keepdims=True))

    key_d = t0 + lax.broadcasted_iota(jnp.int32, (Q_TILE, Q_TILE), 0)
    causal = jnp.where(key_d <= tok, 0.0, NEG_INF)
    sd = _nt_dot(ks16_ref[pl.ds(t0, Q_TILE), :], qs) + tile4(causal)
    md = jnp.max(sd, axis=0, keepdims=True)
    pd = jnp.exp2(sd - md)
    m_ref[...] = md
    l_ref[...] = jnp.sum(pd, axis=0, keepdims=True)
    acc_ref[...] = _dot(vst16_ref[head_rows, pl.ds(t0, Q_TILE)], pd.astype(BF16))

    imp = p[:, 0:Q_TILE]
    for r in range(1, NSA_GROUP):
        imp = imp + p[:, r * Q_TILE:(r + 1) * Q_TILE]
    imp_ref[0:SUBLANES, :] = jnp.zeros((SUBLANES, Q_TILE), F32)
    imp_ref[SUBLANES:SUBLANES + nc, :] = imp
    ratio = SEL_BLOCK // CMP_STRIDE
    p_slc = imp_ref[pl.ds(SUBLANES - 1, ns, stride=ratio), :]
    for d in range(ratio):
        p_slc = p_slc + imp_ref[pl.ds(SUBLANES + d, ns, stride=ratio), :]
    blk = lax.broadcasted_iota(jnp.int32, (ns, Q_TILE), 0)
    cur = tok // SEL_BLOCK
    forced = (blk == 0) | (blk > cur - N_LOCAL_SEL)
    score = jnp.where(blk <= cur, p_slc + jnp.where(forced, FORCE_BONUS, 0.0), -FORCE_BONUS)
    score_ref[...] = score

    def rank_step(jp, rank):
        row = score_ref[pl.ds(jp, 1), :]
        beats = (row > score) | ((row == score) & (jp < blk))
        return rank + beats.astype(F32)
    n_top = min(SEL_TOP, ns)
    visible = blk <= cur

    @pl.when(t0 + Q_TILE <= n_top * SEL_BLOCK)
    def _():
        selb_ref[...] = jnp.where(visible, 0.0, NEG_INF)

    @pl.when(t0 + Q_TILE > n_top * SEL_BLOCK)
    def _():
        rank = lax.fori_loop(0, ns, rank_step, jnp.zeros((ns, Q_TILE), F32), unroll=8)
        selb_ref[...] = jnp.where((rank < float(n_top)) & visible, 0.0, NEG_INF)

    first_blk = i * (Q_TILE // SEL_BLOCK)
    seloff_ref[...] = jnp.where(blk < first_blk, selb_ref[...], NEG_INF)
    per_tile = SEL_SWEEP // SEL_BLOCK
    n_sweep = (t0 + SEL_SWEEP - 1) // SEL_SWEEP

    last_tile = t_len // SEL_SWEEP - 1

    def score_tile(kt, s_ref, mx_ref):
        k0 = pl.multiple_of(kt * SEL_SWEEP, SEL_SWEEP)
        bias = jnp.concatenate(
            [jnp.broadcast_to(seloff_ref[pl.ds(kt * per_tile + j, 1), :], (SEL_BLOCK, Q_TILE))
             for j in range(per_tile)], axis=0)
        sc = _nt_dot(ks16_ref[pl.ds(k0, SEL_SWEEP), :], qs) + tile4(bias)
        s_ref[...] = sc
        mx_ref[...] = jnp.max(sc, axis=0, keepdims=True)

    def consume_tile(kt, s_ref, mx_ref):
        k0 = pl.multiple_of(kt * SEL_SWEEP, SEL_SWEEP)
        m_new = jnp.maximum(m_ref[...], mx_ref[...])
        alpha = jnp.exp2(m_ref[...] - m_new)
        pp = jnp.exp2(s_ref[...] - m_new)
        l_ref[...] = alpha * l_ref[...] + jnp.sum(pp, axis=0, keepdims=True)
        acc_ref[...] = alpha * acc_ref[...] + _dot(vst16_ref[head_rows, pl.ds(k0, SEL_SWEEP)], pp.astype(BF16))
        m_ref[...] = m_new

    @pl.when(n_sweep > 0)
    def _():
        score_tile(0, sc_a_ref, mt_a_ref)

    def sel_step(kp, carry):
        kt = kp * 2
        score_tile(jnp.minimum(kt + 1, last_tile), sc_b_ref, mt_b_ref)
        consume_tile(kt, sc_a_ref, mt_a_ref)
        score_tile(jnp.minimum(kt + 2, last_tile), sc_a_ref, mt_a_ref)
        consume_tile(jnp.minimum(kt + 1, last_tile), sc_b_ref, mt_b_ref)
        return carry
    lax.fori_loop(0, (n_sweep + 1) // 2, sel_step, 0)
    o_sel = acc_ref[...] / l_ref[...]

    def gate(branch):
        gr = gt_ref[pl.ds(branch * NSA_HEADS + g * NSA_GROUP, NSA_GROUP), :]
        return jnp.concatenate([gr[r:r + 1, :] for r in range(NSA_GROUP)], axis=1)
    o_t = gate(0) * o_cmp + gate(1) * o_sel + gate(2) * o_win
    outs = []
    for r in range(0, NSA_GROUP, 2):
        pair = jnp.concatenate([o_t[:, r * Q_TILE:(r + 1) * Q_TILE],
                                o_t[:, (r + 1) * Q_TILE:(r + 2) * Q_TILE]], axis=0)
        outs.append(pair.T)
    o_ref[0] = jnp.concatenate(outs, axis=1)


def _nsa_prompt(q, gt, kc_sum, vc_sum, ks, vs, kw, vw):
    b, t, _ = q.shape
    assert t % (2 * SEL_SWEEP) == 0 and t >= WINDOW + Q_TILE, t
    nt = t // Q_TILE
    nc = kc_sum.shape[1]
    ns = t // SEL_BLOCK
    rows = NSA_GROUP * Q_TILE
    per_b = lambda a: pl.BlockSpec((1,) + a.shape[1:], lambda bi, g, i: (bi, 0, 0))
    return pl.pallas_call(
        _nsa_prompt_kernel, grid=(b, NSA_KV_HEADS, nt),
        in_specs=[pl.BlockSpec((1, Q_TILE, NSA_WIDTH // NSA_KV_HEADS), lambda bi, g, i: (bi, i, g)),
                  pl.BlockSpec((GATE_ROWS, Q_TILE), lambda bi, g, i: (0, bi * nt + i)),
                  per_b(kc_sum), per_b(vc_sum), per_b(ks), per_b(vs), per_b(kw), per_b(vw)],
        out_specs=pl.BlockSpec((1, Q_TILE, NSA_WIDTH // NSA_KV_HEADS), lambda bi, g, i: (bi, i, g)),
        out_shape=jax.ShapeDtypeStruct((b, t, NSA_WIDTH), F32),
        scratch_shapes=[pltpu.VMEM((t, KV_WIDTH), BF16),
                        pltpu.VMEM((KV_WIDTH, t), BF16),
                        pltpu.VMEM((t, KV_WIDTH), BF16),
                        pltpu.VMEM((KV_WIDTH, t), BF16),
                        pltpu.VMEM((SUBLANES + nc, Q_TILE), F32),
                        pltpu.VMEM((ns, Q_TILE), F32),
                        pltpu.VMEM((ns, Q_TILE), F32),
                        pltpu.VMEM((ns, Q_TILE), F32),
                        pltpu.VMEM((1, rows), F32),
                        pltpu.VMEM((1, rows), F32),
                        pltpu.VMEM((NSA_HEAD_DIM, rows), F32),
                        pltpu.VMEM((SEL_SWEEP, rows), F32),
                        pltpu.VMEM((SEL_SWEEP, rows), F32),
                        pltpu.VMEM((1, rows), F32),
                        pltpu.VMEM((1, rows), F32)],
        compiler_params=_params("arbitrary", "arbitrary", "arbitrary"),
        name="nsa_prompt")(q, gt, kc_sum, vc_sum, ks, vs, kw, vw)


def _hgrn_prompt_kernel(hq_ref, hf_ref, hi_ref, hg_ref, lb_ref, ng_ref, o_ref, st_out_ref, st_ref):
    c = pl.program_id(1)
    n = HGRN_TILE
    sub = HGRN_CHUNK
    hd = HGRN_HEAD_DIM

    @pl.when(c == 0)
    def _():
        st_ref[...] = jnp.zeros(st_ref.shape, F32)

    pos = lax.broadcasted_iota(jnp.int32, (n, hd), 0) % sub
    rc = lax.broadcasted_iota(jnp.int32, (n, n), 0)
    cc = lax.broadcasted_iota(jnp.int32, (n, n), 1)
    intra = (rc // sub == cc // sub) & (cc <= rc)
    tok_chunk = lax.broadcasted_iota(jnp.int32, (hd, n), 1) // sub

    for h in range(HGRN_HEADS):
        cols = slice(h * hd, (h + 1) * hd)
        lb = lb_ref[:, cols]
        f = lb + (1.0 - lb) * jax.nn.sigmoid(hf_ref[0, :, cols])
        logf = jnp.log(f)
        b = logf
        suf = logf
        sh = 1
        while sh < sub:
            b = b + jnp.where(pos >= sh, pltpu.roll(b, sh, axis=0), 0.0)
            suf = suf + jnp.where(pos + sh < sub, pltpu.roll(suf, n - sh, axis=0), 0.0)
            sh *= 2
        hq = hq_ref[0, :, cols]
        k = 1.0 - f
        q_dec = (hq * jax.nn.sigmoid(hq) * jnp.exp(b)).astype(BF16)
        k_inv = (k * jnp.exp(-b)).astype(BF16)
        k_end = (k * jnp.exp(suf - logf)).astype(BF16)
        v = hi_ref[0, :, cols]

        a = jnp.where(intra, _nt_dot(q_dec, k_inv), 0.0)
        o = _dot(a.astype(BF16), v.astype(BF16))

        vt = v.T
        st = st_ref[h]
        inter = []
        for ci in range(n // sub):
            inter.append(_nt_dot(q_dec[ci * sub:(ci + 1) * sub, :], st.astype(BF16)))
            kv_t = _dot(jnp.where(tok_chunk == ci, vt, 0.0).astype(BF16), k_end)
            st = st * jnp.exp(suf[ci * sub:ci * sub + 1, :]) + kv_t
        st_ref[h] = st
        o = o + jnp.concatenate(inter, axis=0)
        o = o * lax.rsqrt(jnp.mean(o * o, axis=-1, keepdims=True) + LN_EPS) * ng_ref[...]
        hg = hg_ref[0, :, cols]
        o_ref[0, :, cols] = o * (hg * jax.nn.sigmoid(hg))

    @pl.when(c == pl.num_programs(1) - 1)
    def _():
        for h in range(HGRN_HEADS):
            st_out_ref[0, h] = st_ref[h].T


def _hgrn_prompt(h4, lb, norm_g, b, t):
    nchunk = t // HGRN_TILE
    part = lambda p: pl.BlockSpec((1, HGRN_TILE, HGRN_WIDTH), lambda bi, c: (bi, c, p))
    return pl.pallas_call(
        _hgrn_prompt_kernel, grid=(b, nchunk),
        in_specs=[part(0), part(1), part(2), part(3),
                  pl.BlockSpec((1, HGRN_WIDTH), lambda bi, c: (0, 0)),
                  pl.BlockSpec((1, HGRN_HEAD_DIM), lambda bi, c: (0, 0))],
        out_specs=[pl.BlockSpec((1, HGRN_TILE, HGRN_WIDTH), lambda bi, c: (bi, c, 0)),
                   pl.BlockSpec((1, HGRN_HEADS, HGRN_HEAD_DIM, HGRN_HEAD_DIM), lambda bi, c: (bi, 0, 0, 0))],
        out_shape=[jax.ShapeDtypeStruct((b, t, HGRN_WIDTH), F32),
                   jax.ShapeDtypeStruct((b, HGRN_HEADS, HGRN_HEAD_DIM, HGRN_HEAD_DIM), F32)],
        scratch_shapes=[pltpu.VMEM((HGRN_HEADS, HGRN_HEAD_DIM, HGRN_HEAD_DIM), F32)],
        compiler_params=_params("parallel", "arbitrary"),
        name="hgrn_prompt")(h4, h4, h4, h4, lb, norm_g)


def _layer_norm(y, g, b):
    mu = jnp.mean(y, axis=-1, keepdims=True)
    d = y - mu
    var = jnp.mean(d * d, axis=-1, keepdims=True)
    return d * lax.rsqrt(var + LN_EPS) * g + b


def _finish_kernel(x_ref, oa_ref, ob_ref, gab_ref, wba_ref, wbb_ref, wout_ref, g_ref, b_ref,
                   wr_hi_ref, wr_lo_ref, br_ref, x1_ref, x1b_ref, lt_ref):
    a = _dot(oa_ref[...].astype(BF16), wba_ref[...])
    bb = _dot(ob_ref[...].astype(BF16), wbb_ref[...])
    mix = jax.nn.sigmoid(gab_ref[:, :D_MODEL]) * a + jax.nn.sigmoid(gab_ref[:, D_MODEL:]) * bb
    y = DN_ALPHA * x_ref[...] + _dot(mix.astype(BF16), wout_ref[...])
    x1 = _layer_norm(y, g_ref[...], b_ref[...])
    x1_ref[...] = x1
    hi = x1.astype(BF16)
    x1b_ref[...] = hi
    lo = (x1 - hi.astype(F32)).astype(BF16)
    lt_ref[...] = (_nt_dot(wr_hi_ref[...], hi) + _nt_dot(wr_hi_ref[...], lo)
                   + _nt_dot(wr_lo_ref[...], hi) + br_ref[...])


def _finish(x, oa, ob, gab, wba, wbb, wout, g1, b1, wr_hi, wr_lo, br, tm):
    n = x.shape[0]
    row = lambda w: pl.BlockSpec((tm, w), lambda i: (i, 0))
    full = lambda a: pl.BlockSpec(a.shape, lambda i: (0,) * a.ndim)
    return pl.pallas_call(
        _finish_kernel, grid=(n // tm,),
        in_specs=[row(D_MODEL), row(NSA_WIDTH), row(HGRN_WIDTH), row(2 * D_MODEL),
                  full(wba), full(wbb), full(wout), full(g1), full(b1), full(wr_hi), full(wr_lo), full(br)],
        out_specs=[row(D_MODEL), row(D_MODEL), pl.BlockSpec((N_EXPERTS, tm), lambda i: (0, i))],
        out_shape=[jax.ShapeDtypeStruct((n, D_MODEL), F32), jax.ShapeDtypeStruct((n, D_MODEL), BF16),
                   jax.ShapeDtypeStruct((N_EXPERTS, n), F32)],
        compiler_params=_params("parallel"), name="merge_ln_router")(
            x, oa, ob, gab, wba, wbb, wout, g1, b1, wr_hi, wr_lo, br)


def _route_kernel(lt_ref, tri_ref, e_ref, w_ref, pos_ref, cnt_ref, carry_ref):
    i = pl.program_id(0)

    @pl.when(i == 0)
    def _():
        carry_ref[...] = jnp.zeros(carry_ref.shape, F32)

    logit = lt_ref[...]
    tn = logit.shape[1]
    eid = lax.broadcasted_iota(jnp.int32, (N_EXPERTS, tn), 0)
    rank = jnp.zeros((N_EXPERTS, tn), F32)
    for ep in range(N_EXPERTS):
        row = logit[ep:ep + 1, :]
        rank = rank + ((row > logit) | ((row == logit) & (ep < eid))).astype(F32)
    sel = rank < float(TOP_K)
    top = jnp.max(logit, axis=0, keepdims=True)
    ex = jnp.where(sel, jnp.exp(logit - top), 0.0)
    wgt = ex / jnp.sum(ex, axis=0, keepdims=True)
    self = sel.astype(F32)
    incl = _dot(self.astype(BF16), tri_ref[...])
    pos = carry_ref[:, 0:1] + incl - self
    carry_ref[...] = carry_ref[...] + jnp.sum(self, axis=1, keepdims=True)
    eid_f = eid.astype(F32)
    for kk in range(TOP_K):
        pick = sel & (rank == float(kk))
        e_ref[kk:kk + 1, :] = jnp.sum(jnp.where(pick, eid_f, 0.0), axis=0, keepdims=True).astype(jnp.int32)
        w_ref[kk:kk + 1, :] = jnp.sum(jnp.where(pick, wgt, 0.0), axis=0, keepdims=True)
        pos_ref[kk:kk + 1, :] = jnp.sum(jnp.where(pick, pos, 0.0), axis=0, keepdims=True).astype(jnp.int32)
    cnt_ref[...] = carry_ref[...]


def _route(logit_t):
    n = logit_t.shape[1]
    tn = ROUTE_TILE if n % ROUTE_TILE == 0 else n
    tri = (lax.broadcasted_iota(jnp.int32, (tn, tn), 0) <= lax.broadcasted_iota(jnp.int32, (tn, tn), 1)).astype(BF16)
    col = lambda r: pl.BlockSpec((r, tn), lambda i: (0, i))
    return pl.pallas_call(
        _route_kernel, grid=(n // tn,),
        in_specs=[col(N_EXPERTS), pl.BlockSpec((tn, tn), lambda i: (0, 0))],
        out_specs=[col(TOP_K), col(TOP_K), col(TOP_K), pl.BlockSpec((N_EXPERTS, LANES), lambda i: (0, 0))],
        out_shape=[jax.ShapeDtypeStruct((TOP_K, n), jnp.int32), jax.ShapeDtypeStruct((TOP_K, n), F32),
                   jax.ShapeDtypeStruct((TOP_K, n), jnp.int32), jax.ShapeDtypeStruct((N_EXPERTS, LANES), F32)],
        scratch_shapes=[pltpu.VMEM((N_EXPERTS, LANES), F32)],
        compiler_params=_params("arbitrary"), name="route")(logit_t, tri)


def _expert_kernel(blk_e_ref, nblk_ref, xb_ref, wgu_ref, bgu_ref, wd_ref, bd_ref, y_ref, wgu_b, wd_b):
    i = pl.program_id(0)
    live = i < nblk_ref[0]

    @pl.when(live & ((i == 0) | (blk_e_ref[i] != blk_e_ref[jnp.maximum(i - 1, 0)])))
    def _():
        wgu_b[...] = wgu_ref[0].astype(BF16)
        wd_b[...] = wd_ref[0].astype(BF16)

    @pl.when(live)
    def _():
        gu = _dot(xb_ref[...], wgu_b[...]) + bgu_ref[0]
        gate = jnp.minimum(gu[:, :D_EXPERT], SWIGLU_LIMIT)
        up = jnp.clip(gu[:, D_EXPERT:], -SWIGLU_LIMIT, SWIGLU_LIMIT)
        h = gate * jax.nn.sigmoid(SWIGLU_ALPHA * gate) * (up + 1.0)
        y_ref[...] = _dot(h.astype(BF16), wd_b[...]) + bd_ref[0]

    @pl.when(i >= nblk_ref[0])
    def _():
        y_ref[...] = jnp.zeros(y_ref.shape, F32)


def _experts(xb, blk_e, nblk, w_gu, b_gu, w_down, b_down):
    m = xb.shape[0]
    grid_spec = pltpu.PrefetchScalarGridSpec(
        num_scalar_prefetch=2, grid=(m // MOE_ROWS,),
        in_specs=[pl.BlockSpec((MOE_ROWS, D_MODEL), lambda i, be, nb: (i, 0)),
                  pl.BlockSpec((1, D_MODEL, 2 * D_EXPERT), lambda i, be, nb: (be[i], 0, 0)),
                  pl.BlockSpec((1, 1, 2 * D_EXPERT), lambda i, be, nb: (be[i], 0, 0)),
                  pl.BlockSpec((1, D_EXPERT, D_MODEL), lambda i, be, nb: (be[i], 0, 0)),
                  pl.BlockSpec((1, 1, D_MODEL), lambda i, be, nb: (be[i], 0, 0))],
        out_specs=pl.BlockSpec((MOE_ROWS, D_MODEL), lambda i, be, nb: (i, 0)),
        scratch_shapes=[pltpu.VMEM((D_MODEL, 2 * D_EXPERT), BF16), pltpu.VMEM((D_EXPERT, D_MODEL), BF16)])
    return pl.pallas_call(
        _expert_kernel, grid_spec=grid_spec,
        out_shape=jax.ShapeDtypeStruct((m, D_MODEL), F32),
        compiler_params=_params("arbitrary"), name="experts")(
            blk_e, nblk, xb, w_gu, b_gu.reshape(N_EXPERTS, 1, 2 * D_EXPERT),
            w_down, b_down.reshape(N_EXPERTS, 1, D_MODEL))


def _final_ln_kernel(x1_ref, y_ref, g_ref, b_ref, o_ref):
    o_ref[...] = _layer_norm(DN_ALPHA * x1_ref[...] + y_ref[...], g_ref[...], b_ref[...])


def _final_ln(x1, y, g2, b2, tm):
    n = x1.shape[0]
    row = pl.BlockSpec((tm, D_MODEL), lambda i: (i, 0))
    vec = pl.BlockSpec((1, D_MODEL), lambda i: (0, 0))
    return pl.pallas_call(
        _final_ln_kernel, grid=(n // tm,), in_specs=[row, row, vec, vec], out_specs=row,
        out_shape=jax.ShapeDtypeStruct((n, D_MODEL), F32),
        compiler_params=_params("parallel"), name="final_ln")(x1, y, g2, b2)


def _moe_and_norm(x1, x1b, logit_t, w_gu, b_gu, w_down, b_down, g2, b2, tm):
    n = x1.shape[0]
    top_e, top_w, top_pos, counts = _route(logit_t)
    counts = counts[:, 0].astype(jnp.int32)
    padded = (counts + MOE_ROWS - 1) // MOE_ROWS * MOE_ROWS
    pad_end = jnp.cumsum(padded)
    pad_start = pad_end - padded
    n_blocks = -(-(n * TOP_K + N_EXPERTS * (MOE_ROWS - 1)) // MOE_ROWS)
    experts = jnp.arange(N_EXPERTS, dtype=jnp.int32)
    start_of = jnp.sum(jnp.where(top_e[..., None] == experts, pad_start, 0), axis=-1)
    dest = start_of + top_pos
    tok = jnp.broadcast_to(jnp.arange(n, dtype=jnp.int32)[None, :], dest.shape)
    slot_tok = jnp.zeros((n_blocks * MOE_ROWS,), jnp.int32).at[dest.reshape(-1)].set(tok.reshape(-1))
    xb = x1b[slot_tok]
    blk_row0 = jnp.arange(n_blocks, dtype=jnp.int32) * MOE_ROWS
    blk_e = jnp.minimum(jnp.sum((pad_end[None, :] <= blk_row0[:, None]).astype(jnp.int32), axis=1),
                        N_EXPERTS - 1)
    nblk = (pad_end[-1:] // MOE_ROWS).astype(jnp.int32)
    yb = _experts(xb, blk_e, nblk, w_gu, b_gu, w_down, b_down)
    y = jnp.zeros((n, D_MODEL), F32)
    for kk in range(TOP_K):
        y = y + yb[dest[kk]] * top_w[kk][:, None]
    return _final_ln(x1, y, g2, b2, tm)


def _page_copy(cache_hbm, table_ref, bi, p, buf_ref, slot, c, sem_ref):
    col = pl.multiple_of(p * PAGE_SIZE, PAGE_SIZE)
    return pltpu.make_async_copy(cache_hbm.at[table_ref[bi, p]],
                                 buf_ref.at[slot, c, :, pl.ds(col, PAGE_SIZE)], sem_ref.at[slot, c])


def _fetch_pages(caches, table_ref, buf_ref, sem_ref):
    b = pl.program_id(0)
    n_pages = table_ref.shape[1]
    slot = b % 2

    def start(bi, sl):
        def body(p, carry):
            for c, cache in enumerate(caches):
                _page_copy(cache, table_ref, bi, p, buf_ref, sl, c, sem_ref).start()
            return carry
        lax.fori_loop(0, n_pages, body, 0)

    @pl.when(b == 0)
    def _():
        start(0, 0)

    @pl.when(b + 1 < pl.num_programs(0))
    def _():
        start(b + 1, 1 - slot)

    def wait(p, carry):
        for c, cache in enumerate(caches):
            _page_copy(cache, table_ref, b, p, buf_ref, slot, c, sem_ref).wait()
        return carry
    lax.fori_loop(0, n_pages, wait, 0)
    return slot


def _cmp_pages_kernel(table_ref, kcache, vcache, knew_ref, vnew_ref,
                      w1k_ref, bk_ref, w2k_ref, b2k_ref, w1v_ref, bv_ref, w2v_ref, b2v_ref,
                      ko_ref, vo_ref, buf_ref, rows_ref, sem_ref):
    slot = _fetch_pages((kcache, vcache), table_ref, buf_ref, sem_ref)
    n_pages = table_ref.shape[1]
    past = n_pages * PAGE_SIZE
    n_rows = rows_ref.shape[0]
    j = n_rows // CMP_STRIDE
    first_row = lax.broadcasted_iota(jnp.int32, (n_rows - past, KV_WIDTH), 0) == 0
    plan = ((knew_ref, w1k_ref, bk_ref, w2k_ref, b2k_ref, ko_ref),
            (vnew_ref, w1v_ref, bv_ref, w2v_ref, b2v_ref, vo_ref))
    for c, (new_ref, w1_ref, bias_ref, w2_ref, b2_ref, o_ref) in enumerate(plan):
        def to_rows(p, carry):
            off = pl.multiple_of(p * PAGE_SIZE, PAGE_SIZE)
            rows_ref[pl.ds(off, PAGE_SIZE), :] = buf_ref[slot, c, :, pl.ds(off, PAGE_SIZE)].T
            return carry
        lax.fori_loop(0, n_pages, to_rows, 0, unroll=8)
        rows_ref[past:, :] = jnp.where(first_row, new_ref[0], 0.0)
        f = None
        for p in range(0, CMP_STRIDE, 2):
            x = jnp.concatenate([rows_ref[pl.ds(p, j, stride=CMP_STRIDE), :],
                                 rows_ref[pl.ds(p + 1, j, stride=CMP_STRIDE), :]], axis=1).astype(BF16)
            part = _dot(x, w1_ref[p * KV_WIDTH:(p + 2) * KV_WIDTH, :])
            f = part if f is None else f + part
        o_ref[0, 0:j, :] = _compress_tail(f, bias_ref, w2_ref, b2_ref)
        o_ref[0, j:, :] = jnp.zeros((o_ref.shape[1] - j, KV_WIDTH), F32)


def _cmp_pages(table, kcache, vcache, knew, vnew, cw_k, cw_v):
    bs, n_pages = table.shape
    past = n_pages * PAGE_SIZE
    n_rows = past + SUBLANES * CMP_STRIDE
    j = n_rows // CMP_STRIDE
    jp = -(-j // LANES) * LANES
    full = lambda a: pl.BlockSpec(a.shape, lambda b, tbl: (0,) * a.ndim)
    new = pl.BlockSpec((1, 1, KV_WIDTH), lambda b, tbl: (b, 0, 0))
    out = pl.BlockSpec((1, jp, KV_WIDTH), lambda b, tbl: (b, 0, 0))
    grid_spec = pltpu.PrefetchScalarGridSpec(
        num_scalar_prefetch=1, grid=(bs,),
        in_specs=[pl.BlockSpec(memory_space=pl.ANY), pl.BlockSpec(memory_space=pl.ANY), new, new]
        + [full(a) for a in cw_k] + [full(a) for a in cw_v],
        out_specs=[out, out],
        scratch_shapes=[pltpu.VMEM((2, 2, KV_WIDTH, past), F32), pltpu.VMEM((n_rows, KV_WIDTH), F32),
                        pltpu.SemaphoreType.DMA((2, 2))])
    return pl.pallas_call(
        _cmp_pages_kernel, grid_spec=grid_spec,
        out_shape=[jax.ShapeDtypeStruct((bs, jp, KV_WIDTH), F32)] * 2,
        compiler_params=_params("arbitrary"), name="cmp_pages")(
            table, kcache, vcache, knew.reshape(bs, 1, KV_WIDTH), vnew.reshape(bs, 1, KV_WIDTH), *cw_k, *cw_v)


def _nsa_sample_kernel(table_ref, kcache, vcache, q8_ref, gcol_ref, kc_ref, vc_ref, kwt_ref, vwt_ref,
                       ksn_ref, vsn_ref, kwn_ref, vwn_ref, kwc_ref, vwc_ref,
                       o_ref, kwo_ref, vwo_ref, buf_ref, expand_ref, imp_ref, sem_ref):
    b = pl.program_id(0)
    slot = _fetch_pages((kcache, vcache), table_ref, buf_ref, sem_ref)
    past = table_ref.shape[1] * PAGE_SIZE
    q_pos = past
    sel_len = -(-(past + 1) // SEL_BLOCK) * SEL_BLOCK
    nc = sel_len // CMP_STRIDE - 1
    ns = sel_len // SEL_BLOCK
    ncp = kc_ref.shape[1]
    nsp = -(-ns // SUBLANES) * SUBLANES
    nsq = -(-nsp // LANES) * LANES
    nbp = expand_ref.shape[0]
    rows8 = NSA_HEADS
    cur = q_pos // SEL_BLOCK
    n_top = min(SEL_TOP, ns)

    @pl.when(b == 0)
    def _():
        blk_i = lax.broadcasted_iota(jnp.int32, (nbp, past), 0)
        key_i = lax.broadcasted_iota(jnp.int32, (nbp, past), 1)
        expand_ref[...] = (blk_i == key_i // SEL_BLOCK).astype(BF16)

    q8 = q8_ref[0] * (NSA_HEAD_DIM ** -0.5 * math.log2(math.e))
    q8b = q8.astype(BF16)

    st = _nt_dot(kc_ref[0].astype(BF16), q8b)
    n_idx = lax.broadcasted_iota(jnp.int32, (ncp, rows8), 0)
    st = jnp.where((n_idx * CMP_STRIDE + (CMP_BLOCK - 1) <= q_pos) & (n_idx < nc), st, NEG_INF)
    mc = jnp.max(st, axis=0, keepdims=True)
    mc = jnp.where(mc > NEG_INF, mc, 0.0)
    ec = jnp.exp2(st - mc)
    pt = ec / jnp.maximum(jnp.sum(ec, axis=0, keepdims=True), 1e-30)
    o_cmp_t = _dot(vc_ref[0].T.astype(BF16), pt.astype(BF16))
    o_cmp = jnp.concatenate([o_cmp_t, jnp.zeros((KV_WIDTH, LANES - rows8), F32)], axis=1).T[0:rows8, :]

    imp = jnp.concatenate([jnp.sum(pt[:, g * NSA_GROUP:(g + 1) * NSA_GROUP], axis=1, keepdims=True)
                           for g in range(NSA_KV_HEADS)], axis=1)
    imp_ref[0:SUBLANES, :] = jnp.zeros((SUBLANES, NSA_KV_HEADS), F32)
    imp_ref[SUBLANES:SUBLANES + ncp, :] = imp
    ratio = SEL_BLOCK // CMP_STRIDE
    p_slc = imp_ref[pl.ds(SUBLANES - 1, nsp, stride=ratio), :]
    for d in range(ratio):
        p_slc = p_slc + imp_ref[pl.ds(SUBLANES + d, nsp, stride=ratio), :]
    blk = lax.broadcasted_iota(jnp.int32, (nsp, NSA_KV_HEADS), 0)
    forced = (blk == 0) | (blk > cur - N_LOCAL_SEL)
    score = jnp.where(blk <= cur, p_slc + jnp.where(forced, FORCE_BONUS, 0.0), -FORCE_BONUS)
    low = -2.0 * FORCE_BONUS
    score_pad = jnp.concatenate([score, jnp.full((nsp, LANES - NSA_KV_HEADS), low, F32)], axis=1)
    score_pad = jnp.concatenate([score_pad, jnp.full((nsq - nsp, LANES), low, F32)], axis=0)
    score_rows = score_pad.T
    jp = lax.broadcasted_iota(jnp.int32, (nsp, nsq), 0)
    jj = lax.broadcasted_iota(jnp.int32, (nsp, nsq), 1)
    sel_rows = []
    for g in range(NSA_KV_HEADS):
        col = score[:, g:g + 1]
        row = score_rows[g:g + 1, :]
        beats = (col > row) | ((col == row) & (jp < jj))
        rank = jnp.sum(beats.astype(F32), axis=0, keepdims=True)
        sel_rows.append(((rank < float(n_top)) & (jj[0:1, :] <= cur)).astype(F32))
    row_head = lax.broadcasted_iota(jnp.int32, (rows8, nsq), 0) // NSA_GROUP
    sel8 = jnp.where(row_head == 0, sel_rows[0], sel_rows[1])

    chosen = _dot(sel8[:, 0:nbp].astype(BF16), expand_ref[...])
    kt16 = buf_ref[slot, 0].astype(BF16)
    vt16 = buf_ref[slot, 1].astype(BF16)
    s = _dot(q8b, kt16) + jnp.where(chosen > 0.5, 0.0, NEG_INF)
    s_new = jnp.sum(q8 * ksn_ref[0], axis=1, keepdims=True)
    m = jnp.maximum(jnp.max(s, axis=1, keepdims=True), s_new)
    p = jnp.exp2(s - m)
    p_new = jnp.exp2(s_new - m)
    o_sel = ((_nt_dot(p.astype(BF16), vt16) + p_new * vsn_ref[0])
             / (jnp.sum(p, axis=1, keepdims=True) + p_new))

    kwt = kwt_ref[0]
    vwt = vwt_ref[0]
    w = kwt.shape[1]
    lane = lax.broadcasted_iota(jnp.int32, (rows8, w), 1)
    w_pos = past - w + lane
    sw = jnp.where((w_pos > q_pos - WINDOW) & (w_pos >= 0), _dot(q8b, kwt.astype(BF16)), NEG_INF)
    sw_new = jnp.sum(q8 * kwn_ref[0], axis=1, keepdims=True)
    mw = jnp.maximum(jnp.max(sw, axis=1, keepdims=True), sw_new)
    pw = jnp.exp2(sw - mw)
    pw_new = jnp.exp2(sw_new - mw)
    o_win = ((_nt_dot(pw.astype(BF16), vwt.astype(BF16)) + pw_new * vwn_ref[0])
             / (jnp.sum(pw, axis=1, keepdims=True) + pw_new))

    gates = gcol_ref[0]
    o_ref[0] = gates[:, 0:1] * o_cmp + gates[:, 1:2] * o_sel + gates[:, 2:3] * o_win

    last = lax.broadcasted_iota(jnp.int32, (KV_WIDTH, w), 1) == w - 1
    kwo_ref[0] = jnp.where(last, kwc_ref[0], pltpu.roll(kwt, w - 1, axis=1))
    vwo_ref[0] = jnp.where(last, vwc_ref[0], pltpu.roll(vwt, w - 1, axis=1))


def _nsa_sample(table, kcache, vcache, q8, gcol, kc_sum, vc_sum, kwt, vwt, ks_new, vs_new, kw_new, vw_new):
    bs, n_pages = table.shape
    past = n_pages * PAGE_SIZE
    w = kwt.shape[2]
    ncp = kc_sum.shape[1]
    nbp = -(-(past // SEL_BLOCK) // LANES) * LANES
    per_b = lambda a: pl.BlockSpec((1,) + a.shape[1:], lambda b, tbl: (b,) + (0,) * (a.ndim - 1))
    row = lambda a: a.reshape(bs, 1, KV_WIDTH)
    col = lambda a: a.reshape(bs, KV_WIDTH, 1)
    operands = (q8, gcol, kc_sum, vc_sum, kwt, vwt, row(ks_new), row(vs_new), row(kw_new), row(vw_new),
                col(kw_new), col(vw_new))
    grid_spec = pltpu.PrefetchScalarGridSpec(
        num_scalar_prefetch=1, grid=(bs,),
        in_specs=[pl.BlockSpec(memory_space=pl.ANY), pl.BlockSpec(memory_space=pl.ANY)]
        + [per_b(a) for a in operands],
        out_specs=[pl.BlockSpec((1, NSA_HEADS, KV_WIDTH), lambda b, tbl: (b, 0, 0)),
                   pl.BlockSpec((1, KV_WIDTH, w), lambda b, tbl: (b, 0, 0)),
                   pl.BlockSpec((1, KV_WIDTH, w), lambda b, tbl: (b, 0, 0))],
        scratch_shapes=[pltpu.VMEM((2, 2, KV_WIDTH, past), F32), pltpu.VMEM((nbp, past), BF16),
                        pltpu.VMEM((SUBLANES + ncp, NSA_KV_HEADS), F32), pltpu.SemaphoreType.DMA((2, 2))])
    return pl.pallas_call(
        _nsa_sample_kernel, grid_spec=grid_spec,
        out_shape=[jax.ShapeDtypeStruct((bs, NSA_HEADS, KV_WIDTH), F32),
                   jax.ShapeDtypeStruct((bs, KV_WIDTH, w), F32), jax.ShapeDtypeStruct((bs, KV_WIDTH, w), F32)],
        compiler_params=_params("arbitrary"), name="nsa_sample")(table, kcache, vcache, *operands)


def _hgrn_sample_kernel(h4_ref, s_ref, lb_ref, ng_ref, o_ref, so_ref):
    nb = h4_ref.shape[0]
    hd = HGRN_HEAD_DIM
    hw = HGRN_WIDTH
    hq = h4_ref[:, 0:hw]
    lb = lb_ref[...]
    f = lb + (1.0 - lb) * jax.nn.sigmoid(h4_ref[:, hw:2 * hw])
    k = 1.0 - f
    q = hq * jax.nn.sigmoid(hq)
    hg = h4_ref[:, 3 * hw:4 * hw]
    for h in range(HGRN_HEADS):
        cols = slice(h * hd, (h + 1) * hd)
        stack = jnp.concatenate([f[:, cols], k[:, cols], q[:, cols],
                                 jnp.zeros((LANES - 3 * nb, hd), F32)], axis=0).T
        outs = []
        for bi in range(nb):
            v_row = h4_ref[bi:bi + 1, 2 * hw + h * hd:2 * hw + (h + 1) * hd]
            s_new = stack[:, bi:bi + 1] * s_ref[bi, h] + stack[:, nb + bi:nb + bi + 1] * v_row
            so_ref[bi, h] = s_new
            outs.append(jnp.sum(stack[:, 2 * nb + bi:2 * nb + bi + 1] * s_new, axis=0, keepdims=True))
        o = jnp.concatenate(outs, axis=0)
        o = o * lax.rsqrt(jnp.mean(o * o, axis=-1, keepdims=True) + LN_EPS) * ng_ref[...]
        o_ref[:, cols] = o * (hg[:, cols] * jax.nn.sigmoid(hg[:, cols]))


def _hgrn_sample(h4, s0, lb, norm_g):
    bs = h4.shape[0]
    nb = SUBLANES
    st_spec = pl.BlockSpec((nb, HGRN_HEADS, HGRN_HEAD_DIM, HGRN_HEAD_DIM), lambda i: (i, 0, 0, 0))
    return pl.pallas_call(
        _hgrn_sample_kernel, grid=(bs // nb,),
        in_specs=[pl.BlockSpec((nb, 4 * HGRN_WIDTH), lambda i: (i, 0)), st_spec,
                  pl.BlockSpec((1, HGRN_WIDTH), lambda i: (0, 0)),
                  pl.BlockSpec((1, HGRN_HEAD_DIM), lambda i: (0, 0))],
        out_specs=[pl.BlockSpec((nb, HGRN_WIDTH), lambda i: (i, 0)), st_spec],
        out_shape=[jax.ShapeDtypeStruct((bs, HGRN_WIDTH), F32), jax.ShapeDtypeStruct(s0.shape, F32)],
        compiler_params=_params("parallel"), name="hgrn_sample")(h4, s0, lb, norm_g)


def kernel(x_prompt, x_sample, cache_k_cmp, cache_v_cmp, cache_k_sel, cache_v_sel, cache_k_win, cache_v_win, state_hgrn, page_table, w_in, cmp_pe, cmp_w1, cmp_b1, cmp_w2, cmp_b2, hgrn_gamma, hgrn_norm, w_branch_a, w_branch_b, w_out, ln1_g, ln1_b, w_router, b_router, w_gate_up, b_gate_up, w_down, b_down, ln2_g, ln2_b):
    l = 0
    bp, t, _ = x_prompt.shape
    bs = x_sample.shape[0]
    past = page_table.shape[1] * PAGE_SIZE
    win_keep = cache_k_win.shape[2]

    lower = jnp.cumsum(jax.nn.softmax(hgrn_gamma.astype(F32), axis=0), axis=0)[l][None, :]
    norm_g = hgrn_norm[l][None, :]
    w = w_in[l]
    c_g = NSA_WIDTH + 6 * KV_WIDTH
    n_g = 3 * NSA_HEADS
    w_main = jnp.concatenate([w[:, :c_g], w[:, c_g + n_g:]], axis=1).astype(BF16)
    w_gate_t = jnp.pad(w[:, c_g:c_g + n_g].T, ((0, GATE_ROWS - n_g), (0, 0))).astype(BF16)
    cw = [_compress_weights(cmp_pe[l, n], cmp_w1[l, n], cmp_b1[l, n], cmp_w2[l, n], cmp_b2[l, n]) for n in range(2)]
    wba = w_branch_a[l].astype(BF16)
    wbb = w_branch_b[l].astype(BF16)
    wout = w_out[l].astype(BF16)
    wr_t = w_router[l].T
    wr_hi = wr_t.astype(BF16)
    wr_lo = (wr_t - wr_hi.astype(F32)).astype(BF16)
    br = jnp.broadcast_to(b_router[l].astype(F32)[:, None], (N_EXPERTS, 1))
    g1, b1 = ln1_g[l][None, :], ln1_b[l][None, :]
    g2, b2 = ln2_g[l][None, :], ln2_b[l][None, :]

    def finish(x2d, o_a, o_b, gab, tm):
        x1, x1b, logit_t = _finish(x2d, o_a, o_b, gab, wba, wbb, wout, g1, b1, wr_hi, wr_lo, br, tm)
        return _moe_and_norm(x1, x1b, logit_t, w_gate_up[l], b_gate_up[l], w_down[l], b_down[l], g2, b2, tm)

    n = bp * t
    xp = x_prompt.reshape(n, D_MODEL)
    q, kc, vc, ks, vs, kw, vw, h4, gab, gt, *kv_t = _project(xp, w_main, w_gate_t, 256, bp)
    r3 = lambda a: a.reshape(bp, t, a.shape[-1])
    kc_sum = _compress(r3(kc), *cw[0])
    vc_sum = _compress(r3(vc), *cw[1])
    o_nsa = _nsa_prompt(r3(q), gt, kc_sum, vc_sum, r3(ks), r3(vs), r3(kw), r3(vw))
    o_hgrn, p_state = _hgrn_prompt(r3(h4), lower, norm_g, bp, t)
    y_prompt = finish(xp, o_nsa.reshape(n, NSA_WIDTH), o_hgrn.reshape(n, HGRN_WIDTH), gab, 256).reshape(bp, t, D_MODEL)
    win = min(WINDOW, t)
    t5 = lambda a: jnp.transpose(a.reshape(a.shape[0], NSA_KV_HEADS, NSA_HEAD_DIM, a.shape[2]), (0, 3, 1, 2))[None]
    new_p = (tuple(t5(a) for a in kv_t[:4]) + tuple(t5(a[:, :, t - win:]) for a in kv_t[4:])
             + (p_state[None],))

    xs = x_sample.reshape(bs, D_MODEL)
    q, kc, vc, ks, vs, kw, vw, h4, gab, gt, *_ = _project(xs, w_main, w_gate_t, bs, 1)
    pages = lambda c: jnp.transpose(c[l], (0, 2, 3, 1)).reshape(c.shape[1], KV_WIDTH, PAGE_SIZE)
    band = lambda c: jnp.transpose(c[l], (0, 2, 3, 1)).reshape(bs, KV_WIDTH, win_keep)
    kc_sum, vc_sum = _cmp_pages(page_table, pages(cache_k_cmp), pages(cache_v_cmp), kc, vc, cw[0], cw[1])
    head_eye = jnp.eye(NSA_KV_HEADS, dtype=F32)
    q8 = jnp.einsum('bgrd,gh->bgrhd', q.reshape(bs, NSA_KV_HEADS, NSA_GROUP, NSA_HEAD_DIM),
                    head_eye).reshape(bs, NSA_HEADS, KV_WIDTH)
    gcol = jnp.pad(gt[:3 * NSA_HEADS].T.reshape(bs, 3, NSA_HEADS).transpose(0, 2, 1),
                   ((0, 0), (0, 0), (0, LANES - 3)))
    o8, kw_t, vw_t = _nsa_sample(page_table, pages(cache_k_sel), pages(cache_v_sel), q8, gcol, kc_sum, vc_sum,
                                 band(cache_k_win), band(cache_v_win), ks, vs, kw, vw)
    o_nsa = jnp.einsum('bgrhd,gh->bgrd', o8.reshape(bs, NSA_KV_HEADS, NSA_GROUP, NSA_KV_HEADS, NSA_HEAD_DIM),
                       head_eye).reshape(bs, NSA_WIDTH)
    o_hgrn, s_state = _hgrn_sample(h4, state_hgrn[l], lower, norm_g)
    y_sample = finish(xs, o_nsa, o_hgrn, gab, bs).reshape(bs, 1, D_MODEL)
    s5 = lambda a: a.reshape(1, bs, 1, NSA_KV_HEADS, NSA_HEAD_DIM)
    w5 = lambda a: jnp.transpose(a.reshape(bs, NSA_KV_HEADS, NSA_HEAD_DIM, win_keep), (0, 3, 1, 2))[None]
    new_s = (s5(kc), s5(vc), s5(ks), s5(vs), w5(kw_t), w5(vw_t), s_state[None])

    return (y_prompt, y_sample) + new_p + new_s
```

```python
import functools
import math

import jax
import jax.numpy as jnp
from jax import lax
from jax.experimental import pallas as pl
from jax.experimental.pallas import tpu as pltpu

F32 = jnp.float32
BF16 = jnp.bfloat16

D_MODEL = 1024
PAGE_SIZE = 128
NSA_HEADS = 8
NSA_KV_HEADS = 2
NSA_GROUP = NSA_HEADS // NSA_KV_HEADS
NSA_HEAD_DIM = 64
NSA_WIDTH = NSA_HEADS * NSA_HEAD_DIM
KV_WIDTH = NSA_KV_HEADS * NSA_HEAD_DIM
CMP_BLOCK = 32
CMP_STRIDE = 16
CMP_HIDDEN = 128
SEL_BLOCK = 64
SEL_TOP = 16
N_LOCAL_SEL = 2
FORCE_BONUS = 1.0e4
WINDOW = 512
HGRN_HEADS = 4
HGRN_HEAD_DIM = 128
HGRN_WIDTH = HGRN_HEADS * HGRN_HEAD_DIM
HGRN_CHUNK = 16
N_EXPERTS = 32
TOP_K = 4
D_EXPERT = 1024
SWIGLU_LIMIT = 7.0
SWIGLU_ALPHA = 1.702
DEPTH = 1
DN_ALPHA = (2 * DEPTH) ** 0.25
LN_EPS = 1e-5

LANES = 128
SUBLANES = 8
VMEM_BYTES_V7X = 64 * 1024 * 1024
VMEM_LIMIT = VMEM_BYTES_V7X * 3 // 4

Q_TILE = 128
K_TILE = 128
SEL_SWEEP = 512
HGRN_TILE = 128
MOE_ROWS = 512
ROUTE_TILE = 256
NEG_INF = float("-inf")

_C_Q = 0
_C_KV = _C_Q + NSA_WIDTH
_C_H = _C_KV + 6 * KV_WIDTH
_C_GAB = _C_H + 4 * HGRN_WIDTH
_C_END = _C_GAB + 2 * D_MODEL
GATE_ROWS = 32


def _params(*sem):
    return pltpu.CompilerParams(dimension_semantics=sem, vmem_limit_bytes=VMEM_LIMIT)


def _nt_dot(a, b):
    return lax.dot_general(a, b, (((1,), (1,)), ((), ())), preferred_element_type=F32)


def _dot(a, b):
    return jnp.dot(a, b, preferred_element_type=F32)


def _proj_kernel(x_ref, w_ref, wg_ref, q_ref, kc_ref, vc_ref, ks_ref, vs_ref, kw_ref, vw_ref,
                 h_ref, gab_ref, gt_ref, *kvt_refs):
    x = x_ref[...].astype(BF16)
    q_ref[...] = _dot(x, w_ref[:, _C_Q:_C_KV])
    kv = _dot(x, w_ref[:, _C_KV:_C_H])
    for n, ref in enumerate((kc_ref, vc_ref, ks_ref, vs_ref, kw_ref, vw_ref)):
        ref[...] = kv[:, n * KV_WIDTH:(n + 1) * KV_WIDTH]
    for n, ref in enumerate(kvt_refs):
        for c in range(0, kv.shape[0], LANES):
            ref[0, :, c:c + LANES] = kv[c:c + LANES, n * KV_WIDTH:(n + 1) * KV_WIDTH].T
    h_ref[...] = _dot(x, w_ref[:, _C_H:_C_GAB])
    gab_ref[...] = _dot(x, w_ref[:, _C_GAB:_C_END])
    gt_ref[...] = jax.nn.sigmoid(_nt_dot(wg_ref[...], x))


def _project(x, w_main, w_gate_t, tm, batch):
    n = x.shape[0]
    per_b = n // batch // tm
    row = lambda w: pl.BlockSpec((tm, w), lambda i: (i, 0))
    full = lambda a: pl.BlockSpec(a.shape, lambda i: (0,) * a.ndim)
    out_shape = ([jax.ShapeDtypeStruct((n, NSA_WIDTH), F32)]
                 + [jax.ShapeDtypeStruct((n, KV_WIDTH), F32)] * 6
                 + [jax.ShapeDtypeStruct((n, 4 * HGRN_WIDTH), F32),
                    jax.ShapeDtypeStruct((n, 2 * D_MODEL), F32),
                    jax.ShapeDtypeStruct((GATE_ROWS, n), F32)]
                 + [jax.ShapeDtypeStruct((batch, KV_WIDTH, n // batch), F32)] * 6)
    out_specs = ([row(NSA_WIDTH)] + [row(KV_WIDTH)] * 6 + [row(4 * HGRN_WIDTH), row(2 * D_MODEL),
                 pl.BlockSpec((GATE_ROWS, tm), lambda i: (0, i))]
                 + [pl.BlockSpec((1, KV_WIDTH, tm), lambda i: (i // per_b, 0, i % per_b))] * 6)
    return pl.pallas_call(
        _proj_kernel, grid=(n // tm,),
        in_specs=[row(D_MODEL), full(w_main), full(w_gate_t)],
        out_specs=out_specs, out_shape=out_shape,
        compiler_params=_params("parallel"), name="in_proj")(x, w_main, w_gate_t)


def _gelu_tanh(x):
    return 0.5 * x * (1.0 + jnp.tanh(math.sqrt(2.0 / math.pi) * (x + 0.044715 * (x * x * x))))


def _compress_tail(f, bias_ref, w2_ref, b2_ref):
    j = f.shape[0]
    outs = []
    for g in range(NSA_KV_HEADS):
        base = g * 2 * CMP_HIDDEN
        first = f[:, base:base + CMP_HIDDEN]
        second = f[:, base + CMP_HIDDEN:base + 2 * CMP_HIDDEN]
        nxt = pltpu.roll(second, j - 1, axis=0)
        h = _gelu_tanh(first + nxt + bias_ref[g:g + 1, :])
        outs.append(_dot(h.astype(BF16), w2_ref[...]) + b2_ref[...])
    return jnp.concatenate(outs, axis=1)


def _compress_kernel(c_ref, w1_ref, bias_ref, w2_ref, b2_ref, o_ref):
    c = c_ref[0].astype(BF16)
    o_ref[0] = _compress_tail(_dot(c, w1_ref[...]), bias_ref, w2_ref, b2_ref)


def _compress(rows, w1full, bias, w2, b2):
    b, t, _ = rows.shape
    j = t // CMP_STRIDE
    c = rows.reshape(b, j, CMP_STRIDE * KV_WIDTH)
    full = lambda a: pl.BlockSpec(a.shape, lambda i: (0,) * a.ndim)
    return pl.pallas_call(
        _compress_kernel, grid=(b,),
        in_specs=[pl.BlockSpec((1, j, CMP_STRIDE * KV_WIDTH), lambda i: (i, 0, 0)),
                  full(w1full), full(bias), full(w2), full(b2)],
        out_specs=pl.BlockSpec((1, j, KV_WIDTH), lambda i: (i, 0, 0)),
        out_shape=jax.ShapeDtypeStruct((b, j, KV_WIDTH), F32),
        compiler_params=_params("parallel"), name="compress")(c, w1full, bias, w2, b2)


def _compress_weights(pe, w1, b1, w2, b2):
    hd, hid = NSA_HEAD_DIM, CMP_HIDDEN
    halves = w1.reshape(2, CMP_STRIDE, hd, hid)
    eye = jnp.eye(NSA_KV_HEADS, dtype=w1.dtype)
    w1full = jnp.einsum('apdh,kg->pkdgah', halves, eye).reshape(
        CMP_STRIDE * KV_WIDTH, NSA_KV_HEADS * 2 * hid).astype(BF16)
    bias = jnp.einsum('pd,pdh->h', pe, w1, precision=lax.Precision.HIGHEST) + b1
    bias = jnp.broadcast_to(bias[None, :], (SUBLANES, hid))
    return w1full, bias, w2.astype(BF16), b2.reshape(1, hd)


def _nsa_prompt_kernel(q_ref, gt_ref, kc_ref, vc_ref, ks_ref, vs_ref, kw_ref, vw_ref, o_ref,
                       ks16_ref, vst16_ref, kw16_ref, vwt16_ref, imp_ref, score_ref, selb_ref, seloff_ref,
                       m_ref, l_ref, acc_ref, sc_a_ref, sc_b_ref, mt_a_ref, mt_b_ref):
    g = pl.program_id(1)
    i = pl.program_id(2)
    t0 = pl.multiple_of(i * Q_TILE, Q_TILE)
    rows = NSA_GROUP * Q_TILE
    nc = kc_ref.shape[1]
    ns = score_ref.shape[0]
    t_len = ks_ref.shape[1]
    hd = NSA_HEAD_DIM

    @pl.when((g == 0) & (i == 0))
    def _():
        def cast_step(c, carry):
            r0 = pl.multiple_of(c * K_TILE, K_TILE)
            ks16_ref[pl.ds(r0, K_TILE), :] = ks_ref[0, pl.ds(r0, K_TILE), :].astype(BF16)
            kw16_ref[pl.ds(r0, K_TILE), :] = kw_ref[0, pl.ds(r0, K_TILE), :].astype(BF16)
            vst16_ref[:, pl.ds(r0, K_TILE)] = vs_ref[0, pl.ds(r0, K_TILE), :].T.astype(BF16)
            vwt16_ref[:, pl.ds(r0, K_TILE)] = vw_ref[0, pl.ds(r0, K_TILE), :].T.astype(BF16)
            return carry
        lax.fori_loop(0, t_len // K_TILE, cast_step, 0)

    q = q_ref[0] * (hd ** -0.5 * math.log2(math.e))
    lane_head = lax.broadcasted_iota(jnp.int32, (Q_TILE, KV_WIDTH), 1) // NSA_HEAD_DIM
    parts = []
    for r in range(NSA_GROUP):
        qr = q[:, r * NSA_HEAD_DIM:(r + 1) * NSA_HEAD_DIM]
        parts.append(jnp.where(lane_head == g, jnp.concatenate([qr, qr], axis=1), 0.0))
    qs = jnp.concatenate(parts, axis=0).astype(BF16)

    tok = t0 + lax.broadcasted_iota(jnp.int32, (1, Q_TILE), 1)

    def tile4(bias):
        return jnp.concatenate([bias] * NSA_GROUP, axis=1)

    last_ok = jnp.minimum(jnp.right_shift(tok - (CMP_BLOCK - 1), 4), nc - 2)
    n_idx = lax.broadcasted_iota(jnp.int32, (nc, Q_TILE), 0)
    s = _nt_dot(kc_ref[0].astype(BF16), qs) + tile4(jnp.where(n_idx <= last_ok, 0.0, NEG_INF))
    m = jnp.max(s, axis=0, keepdims=True)
    m = jnp.where(m > NEG_INF, m, 0.0)
    e = jnp.exp2(s - m)
    p = e / jnp.maximum(jnp.sum(e, axis=0, keepdims=True), 1e-30)
    o_cmp = _dot(vc_ref[0].T.astype(BF16), p.astype(BF16))
    o_cmp = jnp.where(g == 0, o_cmp[:hd], o_cmp[hd:])

    head_rows = pl.ds(pl.multiple_of(g * hd, hd), hd)

    wk = WINDOW + Q_TILE
    w0 = pl.multiple_of(jnp.maximum(t0 - WINDOW, 0), Q_TILE)
    key_w = w0 + lax.broadcasted_iota(jnp.int32, (wk, Q_TILE), 0)
    band = jnp.where(lax.bitcast_convert_type(tok - key_w, jnp.uint32) < WINDOW, 0.0, NEG_INF)
    sw = _nt_dot(kw16_ref[pl.ds(w0, wk), :], qs) + tile4(band)
    pw = jnp.exp2(sw - jnp.max(sw, axis=0, keepdims=True))
    o_win = (_dot(vwt16_ref[head_rows, pl.ds(w0, wk)], pw.astype(BF16))
             / jnp.sum(pw, axis=0, keepdims=True))

    key_d = t0 + lax.broadcasted_iota(jnp.int32, (Q_TILE, Q_TILE), 0)
    causal = jnp.where(key_d <= tok, 0.0, NEG_INF)
    sd = _nt_dot(ks16_ref[pl.ds(t0, Q_TILE), :], qs) + tile4(causal)
    md = jnp.max(sd, axis=0, keepdims=True)
    pd = jnp.exp2(sd - md)
    m_ref[...] = md
    l_ref[...] = jnp.sum(pd, axis=0, keepdims=True)
    acc_ref[...] = _dot(vst16_ref[head_rows, pl.ds(t0, Q_TILE)], pd.astype(BF16))

    imp = p[:, 0:Q_TILE]
    for r in range(1, NSA_GROUP):
        imp = imp + p[:, r * Q_TILE:(r + 1) * Q_TILE]
    imp_ref[0:SUBLANES, :] = jnp.zeros((SUBLANES, Q_TILE), F32)
    imp_ref[SUBLANES:SUBLANES + nc, :] = imp
    ratio = SEL_BLOCK // CMP_STRIDE
    p_slc = imp_ref[pl.ds(SUBLANES - 1, ns, stride=ratio), :]
    for d in range(ratio):
        p_slc = p_slc + imp_ref[pl.ds(SUBLANES + d, ns, stride=ratio), :]
    blk = lax.broadcasted_iota(jnp.int32, (ns, Q_TILE), 0)
    cur = tok // SEL_BLOCK
    forced = (blk == 0) | (blk > cur - N_LOCAL_SEL)
    score = jnp.where(blk <= cur, p_slc + jnp.where(forced, FORCE_BONUS, 0.0), -FORCE_BONUS)
    score_ref[...] = score

    def rank_step(jp, rank):
        row = score_ref[pl.ds(jp, 1), :]
        beats = (row > score) | ((row == score) & (jp < blk))
        return rank + beats.astype(F32)
    n_top = min(SEL_TOP, ns)
    visible = blk <= cur

    @pl.when(t0 + Q_TILE <= n_top * SEL_BLOCK)
    def _():
        selb_ref[...] = jnp.where(visible, 0.0, NEG_INF)

    @pl.when(t0 + Q_TILE > n_top * SEL_BLOCK)
    def _():
        rank = lax.fori_loop(0, ns, rank_step, jnp.zeros((ns, Q_TILE), F32), unroll=8)
        selb_ref[...] = jnp.where((rank < float(n_top)) & visible, 0.0, NEG_INF)

    first_blk = i * (Q_TILE // SEL_BLOCK)
    seloff_ref[...] = jnp.where(blk < first_blk, selb_ref[...], NEG_INF)
    per_tile = SEL_SWEEP // SEL_BLOCK
    n_sweep = (t0 + SEL_SWEEP - 1) // SEL_SWEEP

    last_tile = t_len // SEL_SWEEP - 1

    def score_tile(kt, s_ref, mx_ref):
        k0 = pl.multiple_of(kt * SEL_SWEEP, SEL_SWEEP)
        bias = jnp.concatenate(
            [jnp.broadcast_to(seloff_ref[pl.ds(kt * per_tile + j, 1), :], (SEL_BLOCK, Q_TILE))
             for j in range(per_tile)], axis=0)
        sc = _nt_dot(ks16_ref[pl.ds(k0, SEL_SWEEP), :], qs) + tile4(bias)
        s_ref[...] = sc
        mx_ref[...] = jnp.max(sc, axis=0, keepdims=True)

    def consume_tile(kt, s_ref, mx_ref):
        k0 = pl.multiple_of(kt * SEL_SWEEP, SEL_SWEEP)
        m_new = jnp.maximum(m_ref[...], mx_ref[...])
        alpha = jnp.exp2(m_ref[...] - m_new)
        pp = jnp.exp2(s_ref[...] - m_new)
        l_ref[...] = alpha * l_ref[...] + jnp.sum(pp, axis=0, keepdims=True)
        acc_ref[...] = alpha * acc_ref[...] + _dot(vst16_ref[head_rows, pl.ds(k0, SEL_SWEEP)], pp.astype(BF16))
        m_ref[...] = m_new

    @pl.when(n_sweep > 0)
    def _():
        score_tile(0, sc_a_ref, mt_a_ref)

    def sel_step(kp, carry):
        kt = kp * 2
        score_tile(jnp.minimum(kt + 1, last_tile), sc_b_ref, mt_b_ref)
        consume_tile(kt, sc_a_ref, mt_a_ref)
        score_tile(jnp.minimum(kt + 2, last_tile), sc_a_ref, mt_a_ref)
        consume_tile(jnp.minimum(kt + 1, last_tile), sc_b_ref, mt_b_ref)
        return carry
    lax.fori_loop(0, (n_sweep + 1) // 2, sel_step, 0)
    o_sel = acc_ref[...] / l_ref[...]

    def gate(branch):
        gr = gt_ref[pl.ds(branch * NSA_HEADS + g * NSA_GROUP, NSA_GROUP), :]
        return jnp.concatenate([gr[r:r + 1, :] for r in range(NSA_GROUP)], axis=1)
    o_t = gate(0) * o_cmp + gate(1) * o_sel + gate(2) * o_win
    outs = []
    for r in range(0, NSA_GROUP, 2):
        pair = jnp.concatenate([o_t[:, r * Q_TILE:(r + 1) * Q_TILE],
                                o_t[:, (r + 1) * Q_TILE:(r + 2) * Q_TILE]], axis=0)
        outs.append(pair.T)
    o_ref[0] = jnp.concatenate(outs, axis=1)


def _nsa_prompt(q, gt, kc_sum, vc_sum, ks, vs, kw, vw):
    b, t, _ = q.shape
    assert t % (2 * SEL_SWEEP) == 0 and t >= WINDOW + Q_TILE, t
    nt = t // Q_TILE
    nc = kc_sum.shape[1]
    ns = t // SEL_BLOCK
    rows = NSA_GROUP * Q_TILE
    per_b = lambda a: pl.BlockSpec((1,) + a.shape[1:], lambda bi, g, i: (bi, 0, 0))
    return pl.pallas_call(
        _nsa_prompt_kernel, grid=(b, NSA_KV_HEADS, nt),
        in_specs=[pl.BlockSpec((1, Q_TILE, NSA_WIDTH // NSA_KV_HEADS), lambda bi, g, i: (bi, i, g)),
                  pl.BlockSpec((GATE_ROWS, Q_TILE), lambda bi, g, i: (0, bi * nt + i)),
                  per_b(kc_sum), per_b(vc_sum), per_b(ks), per_b(vs), per_b(kw), per_b(vw)],
        out_specs=pl.BlockSpec((1, Q_TILE, NSA_WIDTH // NSA_KV_HEADS), lambda bi, g, i: (bi, i, g)),
        out_shape=jax.ShapeDtypeStruct((b, t, NSA_WIDTH), F32),
        scratch_shapes=[pltpu.VMEM((t, KV_WIDTH), BF16),
                        pltpu.VMEM((KV_WIDTH, t), BF16),
                        pltpu.VMEM((t, KV_WIDTH), BF16),
                        pltpu.VMEM((KV_WIDTH, t), BF16),
                        pltpu.VMEM((SUBLANES + nc, Q_TILE), F32),
                        pltpu.VMEM((ns, Q_TILE), F32),
                        pltpu.VMEM((ns, Q_TILE), F32),
                        pltpu.VMEM((ns, Q_TILE), F32),
                        pltpu.VMEM((1, rows), F32),
                        pltpu.VMEM((1, rows), F32),
                        pltpu.VMEM((NSA_HEAD_DIM, rows), F32),
                        pltpu.VMEM((SEL_SWEEP, rows), F32),
                        pltpu.VMEM((SEL_SWEEP, rows), F32),
                        pltpu.VMEM((1, rows), F32),
                        pltpu.VMEM((1, rows), F32)],
        compiler_params=_params("arbitrary", "arbitrary", "arbitrary"),
        name="nsa_prompt")(q, gt, kc_sum, vc_sum, ks, vs, kw, vw)


def _hgrn_prompt_kernel(hq_ref, hf_ref, hi_ref, hg_ref, lb_ref, ng_ref, o_ref, st_out_ref, st_ref):
    c = pl.program_id(1)
    n = HGRN_TILE
    sub = HGRN_CHUNK
    hd = HGRN_HEAD_DIM

    @pl.when(c == 0)
    def _():
        st_ref[...] = jnp.zeros(st_ref.shape, F32)

    pos = lax.broadcasted_iota(jnp.int32, (n, hd), 0) % sub
    rc = lax.broadcasted_iota(jnp.int32, (n, n), 0)
    cc = lax.broadcasted_iota(jnp.int32, (n, n), 1)
    intra = (rc // sub == cc // sub) & (cc <= rc)
    tok_chunk = lax.broadcasted_iota(jnp.int32, (hd, n), 1) // sub

    for h in range(HGRN_HEADS):
        cols = slice(h * hd, (h + 1) * hd)
        lb = lb_ref[:, cols]
        f = lb + (1.0 - lb) * jax.nn.sigmoid(hf_ref[0, :, cols])
        logf = jnp.log(f)
        b = logf
        suf = logf
        sh = 1
        while sh < sub:
            b = b + jnp.where(pos >= sh, pltpu.roll(b, sh, axis=0), 0.0)
            suf = suf + jnp.where(pos + sh < sub, pltpu.roll(suf, n - sh, axis=0), 0.0)
            sh *= 2
        hq = hq_ref[0, :, cols]
        k = 1.0 - f
        q_dec = (hq * jax.nn.sigmoid(hq) * jnp.exp(b)).astype(BF16)
        k_inv = (k * jnp.exp(-b)).astype(BF16)
        k_end = (k * jnp.exp(suf - logf)).astype(BF16)
        v = hi_ref[0, :, cols]

        a = jnp.where(intra, _nt_dot(q_dec, k_inv), 0.0)
        o = _dot(a.astype(BF16), v.astype(BF16))

        vt = v.T
        st = st_ref[h]
        inter = []
        for ci in range(n // sub):
            inter.append(_nt_dot(q_dec[ci * sub:(ci + 1) * sub, :], st.astype(BF16)))
            kv_t = _dot(jnp.where(tok_chunk == ci, vt, 0.0).astype(BF16), k_end)
            st = st * jnp.exp(suf[ci * sub:ci * sub + 1, :]) + kv_t
        st_ref[h] = st
        o = o + jnp.concatenate(inter, axis=0)
        o = o * lax.rsqrt(jnp.mean(o * o, axis=-1, keepdims=True) + LN_EPS) * ng_ref[...]
        hg = hg_ref[0, :, cols]
        o_ref[0, :, cols] = o * (hg * jax.nn.sigmoid(hg))

    @pl.when(c == pl.num_programs(1) - 1)
    def _():
        for h in range(HGRN_HEADS):
            st_out_ref[0, h] = st_ref[h].T


def _hgrn_prompt(h4, lb, norm_g, b, t):
    nchunk = t // HGRN_TILE
    part = lambda p: pl.BlockSpec((1, HGRN_TILE, HGRN_WIDTH), lambda bi, c: (bi, c, p))
    return pl.pallas_call(
        _hgrn_prompt_kernel, grid=(b, nchunk),
        in_specs=[part(0), part(1), part(2), part(3),
                  pl.BlockSpec((1, HGRN_WIDTH), lambda bi, c: (0, 0)),
                  pl.BlockSpec((1, HGRN_HEAD_DIM), lambda bi, c: (0, 0))],
        out_specs=[pl.BlockSpec((1, HGRN_TILE, HGRN_WIDTH), lambda bi, c: (bi, c, 0)),
                   pl.BlockSpec((1, HGRN_HEADS, HGRN_HEAD_DIM, HGRN_HEAD_DIM), lambda bi, c: (bi, 0, 0, 0))],
        out_shape=[jax.ShapeDtypeStruct((b, t, HGRN_WIDTH), F32),
                   jax.ShapeDtypeStruct((b, HGRN_HEADS, HGRN_HEAD_DIM, HGRN_HEAD_DIM), F32)],
        scratch_shapes=[pltpu.VMEM((HGRN_HEADS, HGRN_HEAD_DIM, HGRN_HEAD_DIM), F32)],
        compiler_params=_params("parallel", "arbitrary"),
        name="hgrn_prompt")(h4, h4, h4, h4, lb, norm_g)


def _layer_norm(y, g, b):
    mu = jnp.mean(y, axis=-1, keepdims=True)
    d = y - mu
    var = jnp.mean(d * d, axis=-1, keepdims=True)
    return d * lax.rsqrt(var + LN_EPS) * g + b


def _finish_kernel(x_ref, oa_ref, ob_ref, gab_ref, wba_ref, wbb_ref, wout_ref, g_ref, b_ref,
                   wr_hi_ref, wr_lo_ref, br_ref, x1_ref, x1b_ref, lt_ref):
    a = _dot(oa_ref[...].astype(BF16), wba_ref[...])
    bb = _dot(ob_ref[...].astype(BF16), wbb_ref[...])
    mix = jax.nn.sigmoid(gab_ref[:, :D_MODEL]) * a + jax.nn.sigmoid(gab_ref[:, D_MODEL:]) * bb
    y = DN_ALPHA * x_ref[...] + _dot(mix.astype(BF16), wout_ref[...])
    x1 = _layer_norm(y, g_ref[...], b_ref[...])
    x1_ref[...] = x1
    hi = x1.astype(BF16)
    x1b_ref[...] = hi
    lo = (x1 - hi.astype(F32)).astype(BF16)
    lt_ref[...] = (_nt_dot(wr_hi_ref[...], hi) + _nt_dot(wr_hi_ref[...], lo)
                   + _nt_dot(wr_lo_ref[...], hi) + br_ref[...])


def _finish(x, oa, ob, gab, wba, wbb, wout, g1, b1, wr_hi, wr_lo, br, tm):
    n = x.shape[0]
    row = lambda w: pl.BlockSpec((tm, w), lambda i: (i, 0))
    full = lambda a: pl.BlockSpec(a.shape, lambda i: (0,) * a.ndim)
    return pl.pallas_call(
        _finish_kernel, grid=(n // tm,),
        in_specs=[row(D_MODEL), row(NSA_WIDTH), row(HGRN_WIDTH), row(2 * D_MODEL),
                  full(wba), full(wbb), full(wout), full(g1), full(b1), full(wr_hi), full(wr_lo), full(br)],
        out_specs=[row(D_MODEL), row(D_MODEL), pl.BlockSpec((N_EXPERTS, tm), lambda i: (0, i))],
        out_shape=[jax.ShapeDtypeStruct((n, D_MODEL), F32), jax.ShapeDtypeStruct((n, D_MODEL), BF16),
                   jax.ShapeDtypeStruct((N_EXPERTS, n), F32)],
        compiler_params=_params("parallel"), name="merge_ln_router")(
            x, oa, ob, gab, wba, wbb, wout, g1, b1, wr_hi, wr_lo, br)


def _route_kernel(lt_ref, tri_ref, e_ref, w_ref, pos_ref, cnt_ref, carry_ref):
    i = pl.program_id(0)

    @pl.when(i == 0)
    def _():
        carry_ref[...] = jnp.zeros(carry_ref.shape, F32)

    logit = lt_ref[...]
    tn = logit.shape[1]
    eid = lax.broadcasted_iota(jnp.int32, (N_EXPERTS, tn), 0)
    rank = jnp.zeros((N_EXPERTS, tn), F32)
    for ep in range(N_EXPERTS):
        row = logit[ep:ep + 1, :]
        rank = rank + ((row > logit) | ((row == logit) & (ep < eid))).astype(F32)
    sel = rank < float(TOP_K)
    top = jnp.max(logit, axis=0, keepdims=True)
    ex = jnp.where(sel, jnp.exp(logit - top), 0.0)
    wgt = ex / jnp.sum(ex, axis=0, keepdims=True)
    self = sel.astype(F32)
    incl = _dot(self.astype(BF16), tri_ref[...])
    pos = carry_ref[:, 0:1] + incl - self
    carry_ref[...] = carry_ref[...] + jnp.sum(self, axis=1, keepdims=True)
    eid_f = eid.astype(F32)
    for kk in range(TOP_K):
        pick = sel & (rank == float(kk))
        e_ref[kk:kk + 1, :] = jnp.sum(jnp.where(pick, eid_f, 0.0), axis=0, keepdims=True).astype(jnp.int32)
        w_ref[kk:kk + 1, :] = jnp.sum(jnp.where(pick, wgt, 0.0), axis=0, keepdims=True)
        pos_ref[kk:kk + 1, :] = jnp.sum(jnp.where(pick, pos, 0.0), axis=0, keepdims=True).astype(jnp.int32)
    cnt_ref[...] = carry_ref[...]


def _route(logit_t):
    n = logit_t.shape[1]
    tn = ROUTE_TILE if n % ROUTE_TILE == 0 else n
    tri = (lax.broadcasted_iota(jnp.int32, (tn, tn), 0) <= lax.broadcasted_iota(jnp.int32, (tn, tn), 1)).astype(BF16)
    col = lambda r: pl.BlockSpec((r, tn), lambda i: (0, i))
    return pl.pallas_call(
        _route_kernel, grid=(n // tn,),
        in_specs=[col(N_EXPERTS), pl.BlockSpec((tn, tn), lambda i: (0, 0))],
        out_specs=[col(TOP_K), col(TOP_K), col(TOP_K), pl.BlockSpec((N_EXPERTS, LANES), lambda i: (0, 0))],
        out_shape=[jax.ShapeDtypeStruct((TOP_K, n), jnp.int32), jax.ShapeDtypeStruct((TOP_K, n), F32),
                   jax.ShapeDtypeStruct((TOP_K, n), jnp.int32), jax.ShapeDtypeStruct((N_EXPERTS, LANES), F32)],
        scratch_shapes=[pltpu.VMEM((N_EXPERTS, LANES), F32)],
        compiler_params=_params("arbitrary"), name="route")(logit_t, tri)


def _expert_kernel(blk_e_ref, nblk_ref, xb_ref, wgu_ref, bgu_ref, wd_ref, bd_ref, y_ref, wgu_b, wd_b):
    i = pl.program_id(0)
    live = i < nblk_ref[0]

    @pl.when(live & ((i == 0) | (blk_e_ref[i] != blk_e_ref[jnp.maximum(i - 1, 0)])))
    def _():
        wgu_b[...] = wgu_ref[0].astype(BF16)
        wd_b[...] = wd_ref[0].astype(BF16)

    @pl.when(live)
    def _():
        gu = _dot(xb_ref[...], wgu_b[...]) + bgu_ref[0]
        gate = jnp.minimum(gu[:, :D_EXPERT], SWIGLU_LIMIT)
        up = jnp.clip(gu[:, D_EXPERT:], -SWIGLU_LIMIT, SWIGLU_LIMIT)
        h = gate * jax.nn.sigmoid(SWIGLU_ALPHA * gate) * (up + 1.0)
        y_ref[...] = _dot(h.astype(BF16), wd_b[...]) + bd_ref[0]

    @pl.when(i >= nblk_ref[0])
    def _():
        y_ref[...] = jnp.zeros(y_ref.shape, F32)


def _experts(xb, blk_e, nblk, w_gu, b_gu, w_down, b_down):
    m = xb.shape[0]
    grid_spec = pltpu.PrefetchScalarGridSpec(
        num_scalar_prefetch=2, grid=(m // MOE_ROWS,),
        in_specs=[pl.BlockSpec((MOE_ROWS, D_MODEL), lambda i, be, nb: (i, 0)),
                  pl.BlockSpec((1, D_MODEL, 2 * D_EXPERT), lambda i, be, nb: (be[i], 0, 0)),
                  pl.BlockSpec((1, 1, 2 * D_EXPERT), lambda i, be, nb: (be[i], 0, 0)),
                  pl.BlockSpec((1, D_EXPERT, D_MODEL), lambda i, be, nb: (be[i], 0, 0)),
                  pl.BlockSpec((1, 1, D_MODEL), lambda i, be, nb: (be[i], 0, 0))],
        out_specs=pl.BlockSpec((MOE_ROWS, D_MODEL), lambda i, be, nb: (i, 0)),
        scratch_shapes=[pltpu.VMEM((D_MODEL, 2 * D_EXPERT), BF16), pltpu.VMEM((D_EXPERT, D_MODEL), BF16)])
    return pl.pallas_call(
        _expert_kernel, grid_spec=grid_spec,
        out_shape=jax.ShapeDtypeStruct((m, D_MODEL), F32),
        compiler_params=_params("arbitrary"), name="experts")(
            blk_e, nblk, xb, w_gu, b_gu.reshape(N_EXPERTS, 1, 2 * D_EXPERT),
            w_down, b_down.reshape(N_EXPERTS, 1, D_MODEL))


def _final_ln_kernel(x1_ref, y_ref, g_ref, b_ref, o_ref):
    o_ref[...] = _layer_norm(DN_ALPHA * x1_ref[...] + y_ref[...], g_ref[...], b_ref[...])


def _final_ln(x1, y, g2, b2, tm):
    n = x1.shape[0]
    row = pl.BlockSpec((tm, D_MODEL), lambda i: (i, 0))
    vec = pl.BlockSpec((1, D_MODEL), lambda i: (0, 0))
    return pl.pallas_call(
        _final_ln_kernel, grid=(n // tm,), in_specs=[row, row, vec, vec], out_specs=row,
        out_shape=jax.ShapeDtypeStruct((n, D_MODEL), F32),
        compiler_params=_params("parallel"), name="final_ln")(x1, y, g2, b2)


def _moe_dispatch(x1b, logit_t):
    n = x1b.shape[0]
    top_e, top_w, top_pos, counts = _route(logit_t)
    counts = counts[:, 0].astype(jnp.int32)
    padded = (counts + MOE_ROWS - 1) // MOE_ROWS * MOE_ROWS
    pad_end = jnp.cumsum(padded)
    pad_start = pad_end - padded
    n_blocks = -(-(n * TOP_K + N_EXPERTS * (MOE_ROWS - 1)) // MOE_ROWS)
    experts = jnp.arange(N_EXPERTS, dtype=jnp.int32)
    start_of = jnp.sum(jnp.where(top_e[..., None] == experts, pad_start, 0), axis=-1)
    dest = start_of + top_pos
    tok = jnp.broadcast_to(jnp.arange(n, dtype=jnp.int32)[None, :], dest.shape)
    slot_tok = jnp.zeros((n_blocks * MOE_ROWS,), jnp.int32).at[dest.reshape(-1)].set(tok.reshape(-1))
    xb = x1b[slot_tok]
    blk_row0 = jnp.arange(n_blocks, dtype=jnp.int32) * MOE_ROWS
    blk_e = jnp.minimum(jnp.sum((pad_end[None, :] <= blk_row0[:, None]).astype(jnp.int32), axis=1),
                        N_EXPERTS - 1)
    nblk = (pad_end[-1:] // MOE_ROWS).astype(jnp.int32)
    return xb, blk_e, nblk, dest, top_w, slot_tok


def _moe_finish(x1, dispatch, w_gu, b_gu, w_down, b_down, g2, b2, tm, after=None):
    xb, blk_e, nblk, dest, top_w, _ = dispatch
    if after is not None:
        nblk, _ = lax.optimization_barrier((nblk, after))
    yb = _experts(xb, blk_e, nblk, w_gu, b_gu, w_down, b_down)
    y = jnp.zeros(x1.shape, F32)
    for kk in range(TOP_K):
        y = y + yb[dest[kk]] * top_w[kk][:, None]
    return _final_ln(x1, y, g2, b2, tm)


def _page_copy(cache_hbm, table_ref, bi, p, buf_ref, slot, c, sem_ref):
    return pltpu.make_async_copy(cache_hbm.at[table_ref[bi, p]], buf_ref.at[slot, c, p], sem_ref.at[slot, c])


def _fetch_pages(caches, table_ref, buf_ref, sem_ref):
    b = pl.program_id(0)
    n_pages = table_ref.shape[1]
    slot = b % 2

    def start(bi, sl):
        def body(p, carry):
            for c, cache in enumerate(caches):
                _page_copy(cache, table_ref, bi, p, buf_ref, sl, c, sem_ref).start()
            return carry
        lax.fori_loop(0, n_pages, body, 0)

    @pl.when(b == 0)
    def _():
        start(0, 0)

    @pl.when(b + 1 < pl.num_programs(0))
    def _():
        start(b + 1, 1 - slot)

    def wait(p, carry):
        for c, cache in enumerate(caches):
            _page_copy(cache, table_ref, b, p, buf_ref, slot, c, sem_ref).wait()
        return carry
    lax.fori_loop(0, n_pages, wait, 0)
    return slot


def _cmp_pages_kernel(table_ref, kcache, vcache, knew_ref, vnew_ref,
                      w1k_ref, bk_ref, w2k_ref, b2k_ref, w1v_ref, bv_ref, w2v_ref, b2v_ref,
                      ko_ref, vo_ref, buf_ref, rows_ref, sem_ref):
    slot = _fetch_pages((kcache, vcache), table_ref, buf_ref, sem_ref)
    n_pages = table_ref.shape[1]
    past = n_pages * PAGE_SIZE
    n_rows = rows_ref.shape[0]
    j = n_rows // CMP_STRIDE
    first_row = lax.broadcasted_iota(jnp.int32, (n_rows - past, KV_WIDTH), 0) == 0
    plan = ((knew_ref, w1k_ref, bk_ref, w2k_ref, b2k_ref, ko_ref),
            (vnew_ref, w1v_ref, bv_ref, w2v_ref, b2v_ref, vo_ref))
    for c, (new_ref, w1_ref, bias_ref, w2_ref, b2_ref, o_ref) in enumerate(plan):
        def to_rows(p, carry):
            off = pl.multiple_of(p * PAGE_SIZE, PAGE_SIZE)
            rows_ref[pl.ds(off, PAGE_SIZE), :] = buf_ref[slot, c, p].T
            return carry
        lax.fori_loop(0, n_pages, to_rows, 0, unroll=8)
        rows_ref[past:, :] = jnp.where(first_row, new_ref[0], 0.0)
        f = None
        for p in range(0, CMP_STRIDE, 2):
            x = jnp.concatenate([rows_ref[pl.ds(p, j, stride=CMP_STRIDE), :],
                                 rows_ref[pl.ds(p + 1, j, stride=CMP_STRIDE), :]], axis=1).astype(BF16)
            part = _dot(x, w1_ref[p * KV_WIDTH:(p + 2) * KV_WIDTH, :])
            f = part if f is None else f + part
        o_ref[0, 0:j, :] = _compress_tail(f, bias_ref, w2_ref, b2_ref)
        o_ref[0, j:, :] = jnp.zeros((o_ref.shape[1] - j, KV_WIDTH), F32)


def _cmp_pages(table, kcache, vcache, knew, vnew, cw_k, cw_v):
    bs, n_pages = table.shape
    past = n_pages * PAGE_SIZE
    n_rows = past + SUBLANES * CMP_STRIDE
    j = n_rows // CMP_STRIDE
    jp = -(-j // LANES) * LANES
    full = lambda a: pl.BlockSpec(a.shape, lambda b, tbl: (0,) * a.ndim)
    new = pl.BlockSpec((1, 1, KV_WIDTH), lambda b, tbl: (b, 0, 0))
    out = pl.BlockSpec((1, jp, KV_WIDTH), lambda b, tbl: (b, 0, 0))
    grid_spec = pltpu.PrefetchScalarGridSpec(
        num_scalar_prefetch=1, grid=(bs,),
        in_specs=[pl.BlockSpec(memory_space=pl.ANY), pl.BlockSpec(memory_space=pl.ANY), new, new]
        + [full(a) for a in cw_k] + [full(a) for a in cw_v],
        out_specs=[out, out],
        scratch_shapes=[pltpu.VMEM((2, 2, n_pages, KV_WIDTH, PAGE_SIZE), F32),
                        pltpu.VMEM((n_rows, KV_WIDTH), F32), pltpu.SemaphoreType.DMA((2, 2))])
    return pl.pallas_call(
        _cmp_pages_kernel, grid_spec=grid_spec,
        out_shape=[jax.ShapeDtypeStruct((bs, jp, KV_WIDTH), F32)] * 2,
        compiler_params=_params("arbitrary"), name="cmp_pages")(
            table, kcache, vcache, knew.reshape(bs, 1, KV_WIDTH), vnew.reshape(bs, 1, KV_WIDTH), *cw_k, *cw_v)


def _nsa_sample_kernel(table_ref, kcache, vcache, q8_ref, gcol_ref, kc_ref, vc_ref, kwt_ref, vwt_ref,
                       ksn_ref, vsn_ref, kwn_ref, vwn_ref, kwc_ref, vwc_ref,
                       o_ref, kwo_ref, vwo_ref, buf_ref, expand_ref, imp_ref, sem_ref):
    b = pl.program_id(0)
    slot = _fetch_pages((kcache, vcache), table_ref, buf_ref, sem_ref)
    past = table_ref.shape[1] * PAGE_SIZE
    q_pos = past
    sel_len = -(-(past + 1) // SEL_BLOCK) * SEL_BLOCK
    nc = sel_len // CMP_STRIDE - 1
    ns = sel_len // SEL_BLOCK
    ncp = kc_ref.shape[1]
    nsp = -(-ns // SUBLANES) * SUBLANES
    nsq = -(-nsp // LANES) * LANES
    nbp = expand_ref.shape[0]
    rows8 = NSA_HEADS
    cur = q_pos // SEL_BLOCK
    n_top = min(SEL_TOP, ns)

    @pl.when(b == 0)
    def _():
        blk_i = lax.broadcasted_iota(jnp.int32, (nbp, past), 0)
        key_i = lax.broadcasted_iota(jnp.int32, (nbp, past), 1)
        expand_ref[...] = (blk_i == key_i // SEL_BLOCK).astype(BF16)

    q8 = q8_ref[0] * (NSA_HEAD_DIM ** -0.5 * math.log2(math.e))
    q8b = q8.astype(BF16)

    st = _nt_dot(kc_ref[0].astype(BF16), q8b)
    n_idx = lax.broadcasted_iota(jnp.int32, (ncp, rows8), 0)
    st = jnp.where((n_idx * CMP_STRIDE + (CMP_BLOCK - 1) <= q_pos) & (n_idx < nc), st, NEG_INF)
    mc = jnp.max(st, axis=0, keepdims=True)
    mc = jnp.where(mc > NEG_INF, mc, 0.0)
    ec = jnp.exp2(st - mc)
    pt = ec / jnp.maximum(jnp.sum(ec, axis=0, keepdims=True), 1e-30)
    o_cmp_t = _dot(vc_ref[0].T.astype(BF16), pt.astype(BF16))
    o_cmp = jnp.concatenate([o_cmp_t, jnp.zeros((KV_WIDTH, LANES - rows8), F32)], axis=1).T[0:rows8, :]

    imp = jnp.concatenate([jnp.sum(pt[:, g * NSA_GROUP:(g + 1) * NSA_GROUP], axis=1, keepdims=True)
                           for g in range(NSA_KV_HEADS)], axis=1)
    imp_ref[0:SUBLANES, :] = jnp.zeros((SUBLANES, NSA_KV_HEADS), F32)
    imp_ref[SUBLANES:SUBLANES + ncp, :] = imp
    ratio = SEL_BLOCK // CMP_STRIDE
    p_slc = imp_ref[pl.ds(SUBLANES - 1, nsp, stride=ratio), :]
    for d in range(ratio):
        p_slc = p_slc + imp_ref[pl.ds(SUBLANES + d, nsp, stride=ratio), :]
    blk = lax.broadcasted_iota(jnp.int32, (nsp, NSA_KV_HEADS), 0)
    forced = (blk == 0) | (blk > cur - N_LOCAL_SEL)
    score = jnp.where(blk <= cur, p_slc + jnp.where(forced, FORCE_BONUS, 0.0), -FORCE_BONUS)
    low = -2.0 * FORCE_BONUS
    score_pad = jnp.concatenate([score, jnp.full((nsp, LANES - NSA_KV_HEADS), low, F32)], axis=1)
    score_pad = jnp.concatenate([score_pad, jnp.full((nsq - nsp, LANES), low, F32)], axis=0)
    score_rows = score_pad.T
    jp = lax.broadcasted_iota(jnp.int32, (nsp, nsq), 0)
    jj = lax.broadcasted_iota(jnp.int32, (nsp, nsq), 1)
    sel_rows = []
    for g in range(NSA_KV_HEADS):
        col = score[:, g:g + 1]
        row = score_rows[g:g + 1, :]
        beats = (col > row) | ((col == row) & (jp < jj))
        rank = jnp.sum(beats.astype(F32), axis=0, keepdims=True)
        sel_rows.append(((rank < float(n_top)) & (jj[0:1, :] <= cur)).astype(F32))
    row_head = lax.broadcasted_iota(jnp.int32, (rows8, nsq), 0) // NSA_GROUP
    sel8 = jnp.where(row_head == 0, sel_rows[0], sel_rows[1])

    chosen = _dot(sel8[:, 0:nbp].astype(BF16), expand_ref[...])
    n_pages = table_ref.shape[1]
    kt16 = jnp.concatenate([buf_ref[slot, 0, p].astype(BF16) for p in range(n_pages)], axis=1)
    vt16 = jnp.concatenate([buf_ref[slot, 1, p].astype(BF16) for p in range(n_pages)], axis=1)
    s = _dot(q8b, kt16) + jnp.where(chosen > 0.5, 0.0, NEG_INF)
    s_new = jnp.sum(q8 * ksn_ref[0], axis=1, keepdims=True)
    m = jnp.maximum(jnp.max(s, axis=1, keepdims=True), s_new)
    p = jnp.exp2(s - m)
    p_new = jnp.exp2(s_new - m)
    o_sel = ((_nt_dot(p.astype(BF16), vt16) + p_new * vsn_ref[0])
             / (jnp.sum(p, axis=1, keepdims=True) + p_new))

    kwt = kwt_ref[0]
    vwt = vwt_ref[0]
    w = kwt.shape[1]
    lane = lax.broadcasted_iota(jnp.int32, (rows8, w), 1)
    w_pos = past - w + lane
    sw = jnp.where((w_pos > q_pos - WINDOW) & (w_pos >= 0), _dot(q8b, kwt.astype(BF16)), NEG_INF)
    sw_new = jnp.sum(q8 * kwn_ref[0], axis=1, keepdims=True)
    mw = jnp.maximum(jnp.max(sw, axis=1, keepdims=True), sw_new)
    pw = jnp.exp2(sw - mw)
    pw_new = jnp.exp2(sw_new - mw)
    o_win = ((_nt_dot(pw.astype(BF16), vwt.astype(BF16)) + pw_new * vwn_ref[0])
             / (jnp.sum(pw, axis=1, keepdims=True) + pw_new))

    gates = gcol_ref[0]
    o_ref[0] = gates[:, 0:1] * o_cmp + gates[:, 1:2] * o_sel + gates[:, 2:3] * o_win

    last = lax.broadcasted_iota(jnp.int32, (KV_WIDTH, w), 1) == w - 1
    kwo_ref[0] = jnp.where(last, kwc_ref[0], pltpu.roll(kwt, w - 1, axis=1))
    vwo_ref[0] = jnp.where(last, vwc_ref[0], pltpu.roll(vwt, w - 1, axis=1))


def _nsa_sample(table, kcache, vcache, q8, gcol, kc_sum, vc_sum, kwt, vwt, ks_new, vs_new, kw_new, vw_new):
    bs, n_pages = table.shape
    past = n_pages * PAGE_SIZE
    w = kwt.shape[2]
    ncp = kc_sum.shape[1]
    nbp = -(-(past // SEL_BLOCK) // LANES) * LANES
    per_b = lambda a: pl.BlockSpec((1,) + a.shape[1:], lambda b, tbl: (b,) + (0,) * (a.ndim - 1))
    row = lambda a: a.reshape(bs, 1, KV_WIDTH)
    col = lambda a: a.reshape(bs, KV_WIDTH, 1)
    operands = (q8, gcol, kc_sum, vc_sum, kwt, vwt, row(ks_new), row(vs_new), row(kw_new), row(vw_new),
                col(kw_new), col(vw_new))
    grid_spec = pltpu.PrefetchScalarGridSpec(
        num_scalar_prefetch=1, grid=(bs,),
        in_specs=[pl.BlockSpec(memory_space=pl.ANY), pl.BlockSpec(memory_space=pl.ANY)]
        + [per_b(a) for a in operands],
        out_specs=[pl.BlockSpec((1, NSA_HEADS, KV_WIDTH), lambda b, tbl: (b, 0, 0)),
                   pl.BlockSpec((1, KV_WIDTH, w), lambda b, tbl: (b, 0, 0)),
                   pl.BlockSpec((1, KV_WIDTH, w), lambda b, tbl: (b, 0, 0))],
        scratch_shapes=[pltpu.VMEM((2, 2, n_pages, KV_WIDTH, PAGE_SIZE), F32), pltpu.VMEM((nbp, past), BF16),
                        pltpu.VMEM((SUBLANES + ncp, NSA_KV_HEADS), F32), pltpu.SemaphoreType.DMA((2, 2))])
    return pl.pallas_call(
        _nsa_sample_kernel, grid_spec=grid_spec,
        out_shape=[jax.ShapeDtypeStruct((bs, NSA_HEADS, KV_WIDTH), F32),
                   jax.ShapeDtypeStruct((bs, KV_WIDTH, w), F32), jax.ShapeDtypeStruct((bs, KV_WIDTH, w), F32)],
        compiler_params=_params("arbitrary"), name="nsa_sample")(table, kcache, vcache, *operands)


def _hgrn_sample_kernel(h4_ref, s_ref, lb_ref, ng_ref, o_ref, so_ref):
    nb = h4_ref.shape[0]
    hd = HGRN_HEAD_DIM
    hw = HGRN_WIDTH
    hq = h4_ref[:, 0:hw]
    lb = lb_ref[...]
    f = lb + (1.0 - lb) * jax.nn.sigmoid(h4_ref[:, hw:2 * hw])
    k = 1.0 - f
    q = hq * jax.nn.sigmoid(hq)
    hg = h4_ref[:, 3 * hw:4 * hw]
    for h in range(HGRN_HEADS):
        cols = slice(h * hd, (h + 1) * hd)
        stack = jnp.concatenate([f[:, cols], k[:, cols], q[:, cols],
                                 jnp.zeros((LANES - 3 * nb, hd), F32)], axis=0).T
        outs = []
        for bi in range(nb):
            v_row = h4_ref[bi:bi + 1, 2 * hw + h * hd:2 * hw + (h + 1) * hd]
            s_new = stack[:, bi:bi + 1] * s_ref[bi, h] + stack[:, nb + bi:nb + bi + 1] * v_row
            so_ref[bi, h] = s_new
            outs.append(jnp.sum(stack[:, 2 * nb + bi:2 * nb + bi + 1] * s_new, axis=0, keepdims=True))
        o = jnp.concatenate(outs, axis=0)
        o = o * lax.rsqrt(jnp.mean(o * o, axis=-1, keepdims=True) + LN_EPS) * ng_ref[...]
        o_ref[:, cols] = o * (hg[:, cols] * jax.nn.sigmoid(hg[:, cols]))


def _hgrn_sample(h4, s0, lb, norm_g):
    bs = h4.shape[0]
    nb = SUBLANES
    st_spec = pl.BlockSpec((nb, HGRN_HEADS, HGRN_HEAD_DIM, HGRN_HEAD_DIM), lambda i: (i, 0, 0, 0))
    return pl.pallas_call(
        _hgrn_sample_kernel, grid=(bs // nb,),
        in_specs=[pl.BlockSpec((nb, 4 * HGRN_WIDTH), lambda i: (i, 0)), st_spec,
                  pl.BlockSpec((1, HGRN_WIDTH), lambda i: (0, 0)),
                  pl.BlockSpec((1, HGRN_HEAD_DIM), lambda i: (0, 0))],
        out_specs=[pl.BlockSpec((nb, HGRN_WIDTH), lambda i: (i, 0)), st_spec],
        out_shape=[jax.ShapeDtypeStruct((bs, HGRN_WIDTH), F32), jax.ShapeDtypeStruct(s0.shape, F32)],
        compiler_params=_params("parallel"), name="hgrn_sample")(h4, s0, lb, norm_g)


def kernel(x_prompt, x_sample, cache_k_cmp, cache_v_cmp, cache_k_sel, cache_v_sel, cache_k_win, cache_v_win, state_hgrn, page_table, w_in, cmp_pe, cmp_w1, cmp_b1, cmp_w2, cmp_b2, hgrn_gamma, hgrn_norm, w_branch_a, w_branch_b, w_out, ln1_g, ln1_b, w_router, b_router, w_gate_up, b_gate_up, w_down, b_down, ln2_g, ln2_b):
    l = 0
    bp, t, _ = x_prompt.shape
    bs = x_sample.shape[0]
    past = page_table.shape[1] * PAGE_SIZE
    win_keep = cache_k_win.shape[2]

    lower = jnp.cumsum(jax.nn.softmax(hgrn_gamma.astype(F32), axis=0), axis=0)[l][None, :]
    norm_g = hgrn_norm[l][None, :]
    w = w_in[l]
    c_g = NSA_WIDTH + 6 * KV_WIDTH
    n_g = 3 * NSA_HEADS
    w_main = jnp.concatenate([w[:, :c_g], w[:, c_g + n_g:]], axis=1).astype(BF16)
    w_gate_t = jnp.pad(w[:, c_g:c_g + n_g].T, ((0, GATE_ROWS - n_g), (0, 0))).astype(BF16)
    cw = [_compress_weights(cmp_pe[l, n], cmp_w1[l, n], cmp_b1[l, n], cmp_w2[l, n], cmp_b2[l, n]) for n in range(2)]
    wba = w_branch_a[l].astype(BF16)
    wbb = w_branch_b[l].astype(BF16)
    wout = w_out[l].astype(BF16)
    wr_t = w_router[l].T
    wr_hi = wr_t.astype(BF16)
    wr_lo = (wr_t - wr_hi.astype(F32)).astype(BF16)
    br = jnp.broadcast_to(b_router[l].astype(F32)[:, None], (N_EXPERTS, 1))
    g1, b1 = ln1_g[l][None, :], ln1_b[l][None, :]
    g2, b2 = ln2_g[l][None, :], ln2_b[l][None, :]

    def merge_and_dispatch(x2d, o_a, o_b, gab, tm):
        x1, x1b, logit_t = _finish(x2d, o_a, o_b, gab, wba, wbb, wout, g1, b1, wr_hi, wr_lo, br, tm)
        return x1, _moe_dispatch(x1b, logit_t)

    def experts_and_norm(x1, dispatch, tm, after=None):
        return _moe_finish(x1, dispatch, w_gate_up[l], b_gate_up[l], w_down[l], b_down[l], g2, b2, tm, after)

    n = bp * t
    xp = x_prompt.reshape(n, D_MODEL)
    q, kc, vc, ks, vs, kw, vw, h4, gab, gt, *kv_t = _project(xp, w_main, w_gate_t, 256, bp)
    r3 = lambda a: a.reshape(bp, t, a.shape[-1])
    kc_sum = _compress(r3(kc), *cw[0])
    vc_sum = _compress(r3(vc), *cw[1])
    o_nsa = _nsa_prompt(r3(q), gt, kc_sum, vc_sum, r3(ks), r3(vs), r3(kw), r3(vw))
    o_hgrn, p_state = _hgrn_prompt(r3(h4), lower, norm_g, bp, t)
    x1_p, disp_p = merge_and_dispatch(xp, o_nsa.reshape(n, NSA_WIDTH), o_hgrn.reshape(n, HGRN_WIDTH), gab, 256)
    win = min(WINDOW, t)
    t5 = lambda a: jnp.transpose(a.reshape(a.shape[0], NSA_KV_HEADS, NSA_HEAD_DIM, a.shape[2]), (0, 3, 1, 2))[None]
    new_p = (tuple(t5(a) for a in kv_t[:4]) + tuple(t5(a[:, :, t - win:]) for a in kv_t[4:])
             + (p_state[None],))

    page_table, _ = lax.optimization_barrier((page_table, disp_p[-1][:1]))
    xs = x_sample.reshape(bs, D_MODEL)
    q, kc, vc, ks, vs, kw, vw, h4, gab, gt, *_ = _project(xs, w_main, w_gate_t, bs, 1)
    pages = lambda c: jnp.transpose(c[l], (0, 2, 3, 1)).reshape(c.shape[1], KV_WIDTH, PAGE_SIZE)
    band = lambda c: jnp.transpose(c[l], (0, 2, 3, 1)).reshape(bs, KV_WIDTH, win_keep)
    kc_sum, vc_sum = _cmp_pages(page_table, pages(cache_k_cmp), pages(cache_v_cmp), kc, vc, cw[0], cw[1])
    head_eye = jnp.eye(NSA_KV_HEADS, dtype=F32)
    q8 = jnp.einsum('bgrd,gh->bgrhd', q.reshape(bs, NSA_KV_HEADS, NSA_GROUP, NSA_HEAD_DIM),
                    head_eye).reshape(bs, NSA_HEADS, KV_WIDTH)
    gcol = jnp.pad(gt[:3 * NSA_HEADS].T.reshape(bs, 3, NSA_HEADS).transpose(0, 2, 1),
                   ((0, 0), (0, 0), (0, LANES - 3)))
    o8, kw_t, vw_t = _nsa_sample(page_table, pages(cache_k_sel), pages(cache_v_sel), q8, gcol, kc_sum, vc_sum,
                                 band(cache_k_win), band(cache_v_win), ks, vs, kw, vw)
    o_nsa = jnp.einsum('bgrhd,gh->bgrd', o8.reshape(bs, NSA_KV_HEADS, NSA_GROUP, NSA_KV_HEADS, NSA_HEAD_DIM),
                       head_eye).reshape(bs, NSA_WIDTH)
    o_hgrn, s_state = _hgrn_sample(h4, state_hgrn[l], lower, norm_g)
    y_prompt = experts_and_norm(x1_p, disp_p, 256, after=(o8, s_state)).reshape(bp, t, D_MODEL)
    x1_s, disp_s = merge_and_dispatch(xs, o_nsa, o_hgrn, gab, bs)
    y_sample = experts_and_norm(x1_s, disp_s, bs).reshape(bs, 1, D_MODEL)
    s5 = lambda a: a.reshape(1, bs, 1, NSA_KV_HEADS, NSA_HEAD_DIM)
    w5 = lambda a: jnp.transpose(a.reshape(bs, NSA_KV_HEADS, NSA_HEAD_DIM, win_keep), (0, 3, 1, 2))[None]
    new_s = (s5(kc), s5(vc), s5(ks), s5(vs), w5(kw_t), w5(vw_t), s_state[None])

    return (y_prompt, y_sample) + new_p + new_s
```

```python
import functools
import math

import jax
import jax.numpy as jnp
from jax import lax
from jax.experimental import pallas as pl
from jax.experimental.pallas import tpu as pltpu

F32 = jnp.float32
BF16 = jnp.bfloat16

D_MODEL = 1024
PAGE_SIZE = 128
NSA_HEADS = 8
NSA_KV_HEADS = 2
NSA_GROUP = NSA_HEADS // NSA_KV_HEADS
NSA_HEAD_DIM = 64
NSA_WIDTH = NSA_HEADS * NSA_HEAD_DIM
KV_WIDTH = NSA_KV_HEADS * NSA_HEAD_DIM
CMP_BLOCK = 32
CMP_STRIDE = 16
CMP_HIDDEN = 128
SEL_BLOCK = 64
SEL_TOP = 16
N_LOCAL_SEL = 2
FORCE_BONUS = 1.0e4
WINDOW = 512
HGRN_HEADS = 4
HGRN_HEAD_DIM = 128
HGRN_WIDTH = HGRN_HEADS * HGRN_HEAD_DIM
HGRN_CHUNK = 16
N_EXPERTS = 32
TOP_K = 4
D_EXPERT = 1024
SWIGLU_LIMIT = 7.0
SWIGLU_ALPHA = 1.702
DEPTH = 1
DN_ALPHA = (2 * DEPTH) ** 0.25
LN_EPS = 1e-5

LANES = 128
SUBLANES = 8
VMEM_BYTES_V7X = 64 * 1024 * 1024
VMEM_LIMIT = VMEM_BYTES_V7X * 3 // 4

Q_TILE = 128
K_TILE = 128
SEL_SWEEP = 512
HGRN_TILE = 128
MOE_ROWS = 512
ROUTE_TILE = 256
NEG_INF = float("-inf")

_C_Q = 0
_C_KV = _C_Q + NSA_WIDTH
_C_H = _C_KV + 6 * KV_WIDTH
_C_GAB = _C_H + 4 * HGRN_WIDTH
_C_END = _C_GAB + 2 * D_MODEL
GATE_ROWS = 32


def _params(*sem):
    return pltpu.CompilerParams(dimension_semantics=sem, vmem_limit_bytes=VMEM_LIMIT)


def _nt_dot(a, b):
    return lax.dot_general(a, b, (((1,), (1,)), ((), ())), preferred_element_type=F32)


def _dot(a, b):
    return jnp.dot(a, b, preferred_element_type=F32)


def _proj_kernel(x_ref, w_ref, wg_ref, q_ref, kc_ref, vc_ref, ks_ref, vs_ref, kw_ref, vw_ref,
                 h_ref, gab_ref, gt_ref, *kvt_refs):
    x = x_ref[...].astype(BF16)
    q_ref[...] = _dot(x, w_ref[:, _C_Q:_C_KV])
    kv = _dot(x, w_ref[:, _C_KV:_C_H])
    for n, ref in enumerate((kc_ref, vc_ref, ks_ref, vs_ref, kw_ref, vw_ref)):
        ref[...] = kv[:, n * KV_WIDTH:(n + 1) * KV_WIDTH]
    for n, ref in enumerate(kvt_refs):
        for c in range(0, kv.shape[0], LANES):
            ref[0, :, c:c + LANES] = kv[c:c + LANES, n * KV_WIDTH:(n + 1) * KV_WIDTH].T
    h_ref[...] = _dot(x, w_ref[:, _C_H:_C_GAB])
    gab_ref[...] = _dot(x, w_ref[:, _C_GAB:_C_END])
    gt_ref[...] = jax.nn.sigmoid(_nt_dot(wg_ref[...], x))


def _project(x, w_main, w_gate_t, tm, batch):
    n = x.shape[0]
    per_b = n // batch // tm
    row = lambda w: pl.BlockSpec((tm, w), lambda i: (i, 0))
    full = lambda a: pl.BlockSpec(a.shape, lambda i: (0,) * a.ndim)
    out_shape = ([jax.ShapeDtypeStruct((n, NSA_WIDTH), F32)]
                 + [jax.ShapeDtypeStruct((n, KV_WIDTH), F32)] * 6
                 + [jax.ShapeDtypeStruct((n, 4 * HGRN_WIDTH), F32),
                    jax.ShapeDtypeStruct((n, 2 * D_MODEL), F32),
                    jax.ShapeDtypeStruct((GATE_ROWS, n), F32)]
                 + [jax.ShapeDtypeStruct((batch, KV_WIDTH, n // batch), F32)] * 6)
    out_specs = ([row(NSA_WIDTH)] + [row(KV_WIDTH)] * 6 + [row(4 * HGRN_WIDTH), row(2 * D_MODEL),
                 pl.BlockSpec((GATE_ROWS, tm), lambda i: (0, i))]
                 + [pl.BlockSpec((1, KV_WIDTH, tm), lambda i: (i // per_b, 0, i % per_b))] * 6)
    return pl.pallas_call(
        _proj_kernel, grid=(n // tm,),
        in_specs=[row(D_MODEL), full(w_main), full(w_gate_t)],
        out_specs=out_specs, out_shape=out_shape,
        compiler_params=_params("parallel"), name="in_proj")(x, w_main, w_gate_t)


def _gelu_tanh(x):
    return 0.5 * x * (1.0 + jnp.tanh(math.sqrt(2.0 / math.pi) * (x + 0.044715 * (x * x * x))))


def _compress_tail(f, bias_ref, w2_ref, b2_ref):
    j = f.shape[0]
    outs = []
    for g in range(NSA_KV_HEADS):
        base = g * 2 * CMP_HIDDEN
        first = f[:, base:base + CMP_HIDDEN]
        second = f[:, base + CMP_HIDDEN:base + 2 * CMP_HIDDEN]
        nxt = pltpu.roll(second, j - 1, axis=0)
        h = _gelu_tanh(first + nxt + bias_ref[g:g + 1, :])
        outs.append(_dot(h.astype(BF16), w2_ref[...]) + b2_ref[...])
    return jnp.concatenate(outs, axis=1)


def _compress_kernel(c_ref, w1_ref, bias_ref, w2_ref, b2_ref, o_ref):
    c = c_ref[0].astype(BF16)
    o_ref[0] = _compress_tail(_dot(c, w1_ref[...]), bias_ref, w2_ref, b2_ref)


def _compress(rows, w1full, bias, w2, b2):
    b, t, _ = rows.shape
    j = t // CMP_STRIDE
    c = rows.reshape(b, j, CMP_STRIDE * KV_WIDTH)
    full = lambda a: pl.BlockSpec(a.shape, lambda i: (0,) * a.ndim)
    return pl.pallas_call(
        _compress_kernel, grid=(b,),
        in_specs=[pl.BlockSpec((1, j, CMP_STRIDE * KV_WIDTH), lambda i: (i, 0, 0)),
                  full(w1full), full(bias), full(w2), full(b2)],
        out_specs=pl.BlockSpec((1, j, KV_WIDTH), lambda i: (i, 0, 0)),
        out_shape=jax.ShapeDtypeStruct((b, j, KV_WIDTH), F32),
        compiler_params=_params("parallel"), name="compress")(c, w1full, bias, w2, b2)


def _compress_weights(pe, w1, b1, w2, b2):
    hd, hid = NSA_HEAD_DIM, CMP_HIDDEN
    halves = w1.reshape(2, CMP_STRIDE, hd, hid)
    eye = jnp.eye(NSA_KV_HEADS, dtype=w1.dtype)
    w1full = jnp.einsum('apdh,kg->pkdgah', halves, eye).reshape(
        CMP_STRIDE * KV_WIDTH, NSA_KV_HEADS * 2 * hid).astype(BF16)
    bias = jnp.einsum('pd,pdh->h', pe, w1, precision=lax.Precision.HIGHEST) + b1
    bias = jnp.broadcast_to(bias[None, :], (SUBLANES, hid))
    return w1full, bias, w2.astype(BF16), b2.reshape(1, hd)


def _nsa_prompt_kernel(q_ref, gt_ref, kc_ref, vc_ref, ks_ref, vs_ref, kw_ref, vw_ref, o_ref,
                       ks16_ref, vst16_ref, kw16_ref, vwt16_ref, imp_ref, score_ref, selb_ref, seloff_ref,
                       m_ref, l_ref, acc_ref, sc_a_ref, sc_b_ref, mt_a_ref, mt_b_ref):
    g = pl.program_id(1)
    i = pl.program_id(2)
    t0 = pl.multiple_of(i * Q_TILE, Q_TILE)
    rows = NSA_GROUP * Q_TILE
    nc = kc_ref.shape[1]
    ns = score_ref.shape[0]
    t_len = ks_ref.shape[1]
    hd = NSA_HEAD_DIM

    @pl.when((g == 0) & (i == 0))
    def _():
        def cast_step(c, carry):
            r0 = pl.multiple_of(c * K_TILE, K_TILE)
            ks16_ref[pl.ds(r0, K_TILE), :] = ks_ref[0, pl.ds(r0, K_TILE), :].astype(BF16)
            kw16_ref[pl.ds(r0, K_TILE), :] = kw_ref[0, pl.ds(r0, K_TILE), :].astype(BF16)
            vst16_ref[:, pl.ds(r0, K_TILE)] = vs_ref[0, pl.ds(r0, K_TILE), :].T.astype(BF16)
            vwt16_ref[:, pl.ds(r0, K_TILE)] = vw_ref[0, pl.ds(r0, K_TILE), :].T.astype(BF16)
            return carry
        lax.fori_loop(0, t_len // K_TILE, cast_step, 0)

    q = q_ref[0] * (hd ** -0.5 * math.log2(math.e))
    lane_head = lax.broadcasted_iota(jnp.int32, (Q_TILE, KV_WIDTH), 1) // NSA_HEAD_DIM
    parts = []
    for r in range(NSA_GROUP):
        qr = q[:, r * NSA_HEAD_DIM:(r + 1) * NSA_HEAD_DIM]
        parts.append(jnp.where(lane_head == g, jnp.concatenate([qr, qr], axis=1), 0.0))
    qs = jnp.concatenate(parts, axis=0).astype(BF16)

    tok = t0 + lax.broadcasted_iota(jnp.int32, (1, Q_TILE), 1)

    def tile4(bias):
        return jnp.concatenate([bias] * NSA_GROUP, axis=1)

    head_rows = pl.ds(pl.multiple_of(g * hd, hd), hd)

    wk = WINDOW + Q_TILE
    w0 = pl.multiple_of(jnp.maximum(t0 - WINDOW, 0), Q_TILE)
    s = _nt_dot(kc_ref[0].astype(BF16), qs)
    sw = _nt_dot(kw16_ref[pl.ds(w0, wk), :], qs)
    sd = _nt_dot(ks16_ref[pl.ds(t0, Q_TILE), :], qs)

    last_ok = jnp.minimum(jnp.right_shift(tok - (CMP_BLOCK - 1), 4), nc - 2)
    n_idx = lax.broadcasted_iota(jnp.int32, (nc, Q_TILE), 0)
    s = s + tile4(jnp.where(n_idx <= last_ok, 0.0, NEG_INF))
    m = jnp.max(s, axis=0, keepdims=True)
    m = jnp.where(m > NEG_INF, m, 0.0)
    e = jnp.exp2(s - m)
    p = e / jnp.maximum(jnp.sum(e, axis=0, keepdims=True), 1e-30)

    key_w = w0 + lax.broadcasted_iota(jnp.int32, (wk, Q_TILE), 0)
    sw = sw + tile4(jnp.where(lax.bitcast_convert_type(tok - key_w, jnp.uint32) < WINDOW, 0.0, NEG_INF))
    pw = jnp.exp2(sw - jnp.max(sw, axis=0, keepdims=True))

    key_d = t0 + lax.broadcasted_iota(jnp.int32, (Q_TILE, Q_TILE), 0)
    sd = sd + tile4(jnp.where(key_d <= tok, 0.0, NEG_INF))
    md = jnp.max(sd, axis=0, keepdims=True)
    pd = jnp.exp2(sd - md)

    o_cmp = _dot(vc_ref[0].T.astype(BF16), p.astype(BF16))
    o_cmp = jnp.where(g == 0, o_cmp[:hd], o_cmp[hd:])
    o_win = (_dot(vwt16_ref[head_rows, pl.ds(w0, wk)], pw.astype(BF16))
             / jnp.sum(pw, axis=0, keepdims=True))
    m_ref[...] = md
    l_ref[...] = jnp.sum(pd, axis=0, keepdims=True)
    acc_ref[...] = _dot(vst16_ref[head_rows, pl.ds(t0, Q_TILE)], pd.astype(BF16))

    imp = p[:, 0:Q_TILE]
    for r in range(1, NSA_GROUP):
        imp = imp + p[:, r * Q_TILE:(r + 1) * Q_TILE]
    imp_ref[0:SUBLANES, :] = jnp.zeros((SUBLANES, Q_TILE), F32)
    imp_ref[SUBLANES:SUBLANES + nc, :] = imp
    ratio = SEL_BLOCK // CMP_STRIDE
    p_slc = imp_ref[pl.ds(SUBLANES - 1, ns, stride=ratio), :]
    for d in range(ratio):
        p_slc = p_slc + imp_ref[pl.ds(SUBLANES + d, ns, stride=ratio), :]
    blk = lax.broadcasted_iota(jnp.int32, (ns, Q_TILE), 0)
    cur = tok // SEL_BLOCK
    forced = (blk == 0) | (blk > cur - N_LOCAL_SEL)
    score = jnp.where(blk <= cur, p_slc + jnp.where(forced, FORCE_BONUS, 0.0), -FORCE_BONUS)
    score_ref[...] = score

    def rank_step(jp, rank):
        row = score_ref[pl.ds(jp, 1), :]
        beats = (row > score) | ((row == score) & (jp < blk))
        return rank + beats.astype(F32)
    n_top = min(SEL_TOP, ns)
    visible = blk <= cur

    @pl.when(t0 + Q_TILE <= n_top * SEL_BLOCK)
    def _():
        selb_ref[...] = jnp.where(visible, 0.0, NEG_INF)

    @pl.when(t0 + Q_TILE > n_top * SEL_BLOCK)
    def _():
        rank = lax.fori_loop(0, ns, rank_step, jnp.zeros((ns, Q_TILE), F32), unroll=8)
        selb_ref[...] = jnp.where((rank < float(n_top)) & visible, 0.0, NEG_INF)

    first_blk = i * (Q_TILE // SEL_BLOCK)
    seloff_ref[...] = jnp.where(blk < first_blk, selb_ref[...], NEG_INF)
    per_tile = SEL_SWEEP // SEL_BLOCK
    n_sweep = (t0 + SEL_SWEEP - 1) // SEL_SWEEP

    last_tile = t_len // SEL_SWEEP - 1

    def score_tile(kt, s_ref, mx_ref):
        k0 = pl.multiple_of(kt * SEL_SWEEP, SEL_SWEEP)
        bias = jnp.concatenate(
            [jnp.broadcast_to(seloff_ref[pl.ds(kt * per_tile + j, 1), :], (SEL_BLOCK, Q_TILE))
             for j in range(per_tile)], axis=0)
        sc = _nt_dot(ks16_ref[pl.ds(k0, SEL_SWEEP), :], qs) + tile4(bias)
        s_ref[...] = sc
        mx_ref[...] = jnp.max(sc, axis=0, keepdims=True)

    def consume_tile(kt, s_ref, mx_ref):
        k0 = pl.multiple_of(kt * SEL_SWEEP, SEL_SWEEP)
        m_new = jnp.maximum(m_ref[...], mx_ref[...])
        alpha = jnp.exp2(m_ref[...] - m_new)
        pp = jnp.exp2(s_ref[...] - m_new)
        l_ref[...] = alpha * l_ref[...] + jnp.sum(pp, axis=0, keepdims=True)
        acc_ref[...] = alpha * acc_ref[...] + _dot(vst16_ref[head_rows, pl.ds(k0, SEL_SWEEP)], pp.astype(BF16))
        m_ref[...] = m_new

    @pl.when(n_sweep > 0)
    def _():
        score_tile(0, sc_a_ref, mt_a_ref)

    def sel_step(kp, carry):
        kt = kp * 2
        score_tile(jnp.minimum(kt + 1, last_tile), sc_b_ref, mt_b_ref)
        consume_tile(kt, sc_a_ref, mt_a_ref)
        score_tile(jnp.minimum(kt + 2, last_tile), sc_a_ref, mt_a_ref)
        consume_tile(jnp.minimum(kt + 1, last_tile), sc_b_ref, mt_b_ref)
        return carry
    lax.fori_loop(0, (n_sweep + 1) // 2, sel_step, 0)
    o_sel = acc_ref[...] / l_ref[...]

    def gate(branch):
        gr = gt_ref[pl.ds(branch * NSA_HEADS + g * NSA_GROUP, NSA_GROUP), :]
        return jnp.concatenate([gr[r:r + 1, :] for r in range(NSA_GROUP)], axis=1)
    o_t = gate(0) * o_cmp + gate(1) * o_sel + gate(2) * o_win
    outs = []
    for r in range(0, NSA_GROUP, 2):
        pair = jnp.concatenate([o_t[:, r * Q_TILE:(r + 1) * Q_TILE],
                                o_t[:, (r + 1) * Q_TILE:(r + 2) * Q_TILE]], axis=0)
        outs.append(pair.T)
    o_ref[0] = jnp.concatenate(outs, axis=1)


def _nsa_prompt(q, gt, kc_sum, vc_sum, ks, vs, kw, vw):
    b, t, _ = q.shape
    assert t % (2 * SEL_SWEEP) == 0 and t >= WINDOW + Q_TILE, t
    nt = t // Q_TILE
    nc = kc_sum.shape[1]
    ns = t // SEL_BLOCK
    rows = NSA_GROUP * Q_TILE
    per_b = lambda a: pl.BlockSpec((1,) + a.shape[1:], lambda bi, g, i: (bi, 0, 0))
    return pl.pallas_call(
        _nsa_prompt_kernel, grid=(b, NSA_KV_HEADS, nt),
        in_specs=[pl.BlockSpec((1, Q_TILE, NSA_WIDTH // NSA_KV_HEADS), lambda bi, g, i: (bi, i, g)),
                  pl.BlockSpec((GATE_ROWS, Q_TILE), lambda bi, g, i: (0, bi * nt + i)),
                  per_b(kc_sum), per_b(vc_sum), per_b(ks), per_b(vs), per_b(kw), per_b(vw)],
        out_specs=pl.BlockSpec((1, Q_TILE, NSA_WIDTH // NSA_KV_HEADS), lambda bi, g, i: (bi, i, g)),
        out_shape=jax.ShapeDtypeStruct((b, t, NSA_WIDTH), F32),
        scratch_shapes=[pltpu.VMEM((t, KV_WIDTH), BF16),
                        pltpu.VMEM((KV_WIDTH, t), BF16),
                        pltpu.VMEM((t, KV_WIDTH), BF16),
                        pltpu.VMEM((KV_WIDTH, t), BF16),
                        pltpu.VMEM((SUBLANES + nc, Q_TILE), F32),
                        pltpu.VMEM((ns, Q_TILE), F32),
                        pltpu.VMEM((ns, Q_TILE), F32),
                        pltpu.VMEM((ns, Q_TILE), F32),
                        pltpu.VMEM((1, rows), F32),
                        pltpu.VMEM((1, rows), F32),
                        pltpu.VMEM((NSA_HEAD_DIM, rows), F32),
                        pltpu.VMEM((SEL_SWEEP, rows), F32),
                        pltpu.VMEM((SEL_SWEEP, rows), F32),
                        pltpu.VMEM((1, rows), F32),
                        pltpu.VMEM((1, rows), F32)],
        compiler_params=_params("arbitrary", "arbitrary", "arbitrary"),
        name="nsa_prompt")(q, gt, kc_sum, vc_sum, ks, vs, kw, vw)


def _hgrn_prompt_kernel(hq_ref, hf_ref, hi_ref, hg_ref, lb_ref, ng_ref, o_ref, st_out_ref, st_ref):
    c = pl.program_id(1)
    n = HGRN_TILE
    sub = HGRN_CHUNK
    hd = HGRN_HEAD_DIM

    @pl.when(c == 0)
    def _():
        st_ref[...] = jnp.zeros(st_ref.shape, F32)

    pos = lax.broadcasted_iota(jnp.int32, (n, hd), 0) % sub
    rc = lax.broadcasted_iota(jnp.int32, (n, n), 0)
    cc = lax.broadcasted_iota(jnp.int32, (n, n), 1)
    intra = (rc // sub == cc // sub) & (cc <= rc)
    tok_chunk = lax.broadcasted_iota(jnp.int32, (hd, n), 1) // sub

    for h in range(HGRN_HEADS):
        cols = slice(h * hd, (h + 1) * hd)
        lb = lb_ref[:, cols]
        f = lb + (1.0 - lb) * jax.nn.sigmoid(hf_ref[0, :, cols])
        logf = jnp.log(f)
        b = logf
        suf = logf
        sh = 1
        while sh < sub:
            b = b + jnp.where(pos >= sh, pltpu.roll(b, sh, axis=0), 0.0)
            suf = suf + jnp.where(pos + sh < sub, pltpu.roll(suf, n - sh, axis=0), 0.0)
            sh *= 2
        hq = hq_ref[0, :, cols]
        k = 1.0 - f
        q_dec = (hq * jax.nn.sigmoid(hq) * jnp.exp(b)).astype(BF16)
        k_inv = (k * jnp.exp(-b)).astype(BF16)
        k_end = (k * jnp.exp(suf - logf)).astype(BF16)
        v = hi_ref[0, :, cols]

        a = jnp.where(intra, _nt_dot(q_dec, k_inv), 0.0)
        o = _dot(a.astype(BF16), v.astype(BF16))

        vt = v.T
        kv_t = [_dot(jnp.where(tok_chunk == ci, vt, 0.0).astype(BF16), k_end)
                for ci in range(n // sub)]
        states = [st_ref[h]]
        for ci in range(n // sub):
            states.append(states[-1] * jnp.exp(suf[ci * sub:ci * sub + 1, :]) + kv_t[ci])
        st = states[-1]
        st_ref[h] = st
        inter = [_nt_dot(q_dec[ci * sub:(ci + 1) * sub, :], states[ci].astype(BF16))
                 for ci in range(n // sub)]
        o = o + jnp.concatenate(inter, axis=0)
        o = o * lax.rsqrt(jnp.mean(o * o, axis=-1, keepdims=True) + LN_EPS) * ng_ref[...]
        hg = hg_ref[0, :, cols]
        o_ref[0, :, cols] = o * (hg * jax.nn.sigmoid(hg))

    @pl.when(c == pl.num_programs(1) - 1)
    def _():
        for h in range(HGRN_HEADS):
            st_out_ref[0, h] = st_ref[h].T


def _hgrn_prompt(h4, lb, norm_g, b, t):
    nchunk = t // HGRN_TILE
    part = lambda p: pl.BlockSpec((1, HGRN_TILE, HGRN_WIDTH), lambda bi, c: (bi, c, p))
    return pl.pallas_call(
        _hgrn_prompt_kernel, grid=(b, nchunk),
        in_specs=[part(0), part(1), part(2), part(3),
                  pl.BlockSpec((1, HGRN_WIDTH), lambda bi, c: (0, 0)),
                  pl.BlockSpec((1, HGRN_HEAD_DIM), lambda bi, c: (0, 0))],
        out_specs=[pl.BlockSpec((1, HGRN_TILE, HGRN_WIDTH), lambda bi, c: (bi, c, 0)),
                   pl.BlockSpec((1, HGRN_HEADS, HGRN_HEAD_DIM, HGRN_HEAD_DIM), lambda bi, c: (bi, 0, 0, 0))],
        out_shape=[jax.ShapeDtypeStruct((b, t, HGRN_WIDTH), F32),
                   jax.ShapeDtypeStruct((b, HGRN_HEADS, HGRN_HEAD_DIM, HGRN_HEAD_DIM), F32)],
        scratch_shapes=[pltpu.VMEM((HGRN_HEADS, HGRN_HEAD_DIM, HGRN_HEAD_DIM), F32)],
        compiler_params=_params("parallel", "arbitrary"),
        name="hgrn_prompt")(h4, h4, h4, h4, lb, norm_g)


def _layer_norm(y, g, b):
    mu = jnp.mean(y, axis=-1, keepdims=True)
    d = y - mu
    var = jnp.mean(d * d, axis=-1, keepdims=True)
    return d * lax.rsqrt(var + LN_EPS) * g + b


def _finish_kernel(x_ref, oa_ref, ob_ref, gab_ref, wba_ref, wbb_ref, wout_ref, g_ref, b_ref,
                   wr_hi_ref, wr_lo_ref, br_ref, x1_ref, x1b_ref, lt_ref):
    a = _dot(oa_ref[...].astype(BF16), wba_ref[...])
    bb = _dot(ob_ref[...].astype(BF16), wbb_ref[...])
    mix = jax.nn.sigmoid(gab_ref[:, :D_MODEL]) * a + jax.nn.sigmoid(gab_ref[:, D_MODEL:]) * bb
    y = DN_ALPHA * x_ref[...] + _dot(mix.astype(BF16), wout_ref[...])
    x1 = _layer_norm(y, g_ref[...], b_ref[...])
    x1_ref[...] = x1
    hi = x1.astype(BF16)
    x1b_ref[...] = hi
    lo = (x1 - hi.astype(F32)).astype(BF16)
    lt_ref[...] = (_nt_dot(wr_hi_ref[...], hi) + _nt_dot(wr_hi_ref[...], lo)
                   + _nt_dot(wr_lo_ref[...], hi) + br_ref[...])


def _finish(x, oa, ob, gab, wba, wbb, wout, g1, b1, wr_hi, wr_lo, br, tm):
    n = x.shape[0]
    row = lambda w: pl.BlockSpec((tm, w), lambda i: (i, 0))
    full = lambda a: pl.BlockSpec(a.shape, lambda i: (0,) * a.ndim)
    return pl.pallas_call(
        _finish_kernel, grid=(n // tm,),
        in_specs=[row(D_MODEL), row(NSA_WIDTH), row(HGRN_WIDTH), row(2 * D_MODEL),
                  full(wba), full(wbb), full(wout), full(g1), full(b1), full(wr_hi), full(wr_lo), full(br)],
        out_specs=[row(D_MODEL), row(D_MODEL), pl.BlockSpec((N_EXPERTS, tm), lambda i: (0, i))],
        out_shape=[jax.ShapeDtypeStruct((n, D_MODEL), F32), jax.ShapeDtypeStruct((n, D_MODEL), BF16),
                   jax.ShapeDtypeStruct((N_EXPERTS, n), F32)],
        compiler_params=_params("parallel"), name="merge_ln_router")(
            x, oa, ob, gab, wba, wbb, wout, g1, b1, wr_hi, wr_lo, br)


def _route_kernel(lt_ref, tri_ref, e_ref, w_ref, pos_ref, cnt_ref, carry_ref):
    i = pl.program_id(0)

    @pl.when(i == 0)
    def _():
        carry_ref[...] = jnp.zeros(carry_ref.shape, F32)

    logit = lt_ref[...]
    tn = logit.shape[1]
    eid = lax.broadcasted_iota(jnp.int32, (N_EXPERTS, tn), 0)
    rank = jnp.zeros((N_EXPERTS, tn), F32)
    for ep in range(N_EXPERTS):
        row = logit[ep:ep + 1, :]
        rank = rank + ((row > logit) | ((row == logit) & (ep < eid))).astype(F32)
    sel = rank < float(TOP_K)
    top = jnp.max(logit, axis=0, keepdims=True)
    ex = jnp.where(sel, jnp.exp(logit - top), 0.0)
    wgt = ex / jnp.sum(ex, axis=0, keepdims=True)
    self = sel.astype(F32)
    incl = _dot(self.astype(BF16), tri_ref[...])
    pos = carry_ref[:, 0:1] + incl - self
    carry_ref[...] = carry_ref[...] + jnp.sum(self, axis=1, keepdims=True)
    eid_f = eid.astype(F32)
    for kk in range(TOP_K):
        pick = sel & (rank == float(kk))
        e_ref[kk:kk + 1, :] = jnp.sum(jnp.where(pick, eid_f, 0.0), axis=0, keepdims=True).astype(jnp.int32)
        w_ref[kk:kk + 1, :] = jnp.sum(jnp.where(pick, wgt, 0.0), axis=0, keepdims=True)
        pos_ref[kk:kk + 1, :] = jnp.sum(jnp.where(pick, pos, 0.0), axis=0, keepdims=True).astype(jnp.int32)
    cnt_ref[...] = carry_ref[...]


def _route(logit_t):
    n = logit_t.shape[1]
    tn = ROUTE_TILE if n % ROUTE_TILE == 0 else n
    tri = (lax.broadcasted_iota(jnp.int32, (tn, tn), 0) <= lax.broadcasted_iota(jnp.int32, (tn, tn), 1)).astype(BF16)
    col = lambda r: pl.BlockSpec((r, tn), lambda i: (0, i))
    return pl.pallas_call(
        _route_kernel, grid=(n // tn,),
        in_specs=[col(N_EXPERTS), pl.BlockSpec((tn, tn), lambda i: (0, 0))],
        out_specs=[col(TOP_K), col(TOP_K), col(TOP_K), pl.BlockSpec((N_EXPERTS, LANES), lambda i: (0, 0))],
        out_shape=[jax.ShapeDtypeStruct((TOP_K, n), jnp.int32), jax.ShapeDtypeStruct((TOP_K, n), F32),
                   jax.ShapeDtypeStruct((TOP_K, n), jnp.int32), jax.ShapeDtypeStruct((N_EXPERTS, LANES), F32)],
        scratch_shapes=[pltpu.VMEM((N_EXPERTS, LANES), F32)],
        compiler_params=_params("arbitrary"), name="route")(logit_t, tri)


def _expert_kernel(blk_e_ref, nblk_ref, xb_ref, wgu_ref, bgu_ref, wd_ref, bd_ref, y_ref, wgu_b, wd_b):
    i = pl.program_id(0)
    live = i < nblk_ref[0]

    @pl.when(live & ((i == 0) | (blk_e_ref[i] != blk_e_ref[jnp.maximum(i - 1, 0)])))
    def _():
        wgu_b[...] = wgu_ref[0].astype(BF16)
        wd_b[...] = wd_ref[0].astype(BF16)

    @pl.when(live)
    def _():
        gu = _dot(xb_ref[...], wgu_b[...]) + bgu_ref[0]
        gate = jnp.minimum(gu[:, :D_EXPERT], SWIGLU_LIMIT)
        up = jnp.clip(gu[:, D_EXPERT:], -SWIGLU_LIMIT, SWIGLU_LIMIT)
        h = gate * jax.nn.sigmoid(SWIGLU_ALPHA * gate) * (up + 1.0)
        y_ref[...] = _dot(h.astype(BF16), wd_b[...]) + bd_ref[0]

    @pl.when(i >= nblk_ref[0])
    def _():
        y_ref[...] = jnp.zeros(y_ref.shape, F32)


def _experts(xb, blk_e, nblk, w_gu, b_gu, w_down, b_down):
    m = xb.shape[0]
    grid_spec = pltpu.PrefetchScalarGridSpec(
        num_scalar_prefetch=2, grid=(m // MOE_ROWS,),
        in_specs=[pl.BlockSpec((MOE_ROWS, D_MODEL), lambda i, be, nb: (i, 0)),
                  pl.BlockSpec((1, D_MODEL, 2 * D_EXPERT), lambda i, be, nb: (be[i], 0, 0)),
                  pl.BlockSpec((1, 1, 2 * D_EXPERT), lambda i, be, nb: (be[i], 0, 0)),
                  pl.BlockSpec((1, D_EXPERT, D_MODEL), lambda i, be, nb: (be[i], 0, 0)),
                  pl.BlockSpec((1, 1, D_MODEL), lambda i, be, nb: (be[i], 0, 0))],
        out_specs=pl.BlockSpec((MOE_ROWS, D_MODEL), lambda i, be, nb: (i, 0)),
        scratch_shapes=[pltpu.VMEM((D_MODEL, 2 * D_EXPERT), BF16), pltpu.VMEM((D_EXPERT, D_MODEL), BF16)])
    return pl.pallas_call(
        _expert_kernel, grid_spec=grid_spec,
        out_shape=jax.ShapeDtypeStruct((m, D_MODEL), F32),
        compiler_params=_params("arbitrary"), name="experts")(
            blk_e, nblk, xb, w_gu, b_gu.reshape(N_EXPERTS, 1, 2 * D_EXPERT),
            w_down, b_down.reshape(N_EXPERTS, 1, D_MODEL))


def _final_ln_kernel(x1_ref, w_ref, y0_ref, y1_ref, y2_ref, y3_ref, g_ref, b_ref, o_ref):
    w = w_ref[...]
    y = y0_ref[...] * w[:, 0:1]
    for kk, y_ref in enumerate((y1_ref, y2_ref, y3_ref), start=1):
        y = y + y_ref[...] * w[:, kk:kk + 1]
    o_ref[...] = _layer_norm(DN_ALPHA * x1_ref[...] + y, g_ref[...], b_ref[...])


def _final_ln(x1, w_tok, ys, g2, b2, tm):
    n = x1.shape[0]
    row = pl.BlockSpec((tm, D_MODEL), lambda i: (i, 0))
    vec = pl.BlockSpec((1, D_MODEL), lambda i: (0, 0))
    return pl.pallas_call(
        _final_ln_kernel, grid=(n // tm,),
        in_specs=[row, pl.BlockSpec((tm, TOP_K), lambda i: (i, 0))] + [row] * TOP_K + [vec, vec],
        out_specs=row, out_shape=jax.ShapeDtypeStruct((n, D_MODEL), F32),
        compiler_params=_params("parallel"), name="final_ln")(x1, w_tok, *ys, g2, b2)


def _moe_dispatch(x1b, logit_t):
    n = x1b.shape[0]
    top_e, top_w, top_pos, counts = _route(logit_t)
    counts = counts[:, 0].astype(jnp.int32)
    padded = (counts + MOE_ROWS - 1) // MOE_ROWS * MOE_ROWS
    pad_end = jnp.cumsum(padded)
    pad_start = pad_end - padded
    n_blocks = -(-(n * TOP_K + N_EXPERTS * (MOE_ROWS - 1)) // MOE_ROWS)
    experts = jnp.arange(N_EXPERTS, dtype=jnp.int32)
    start_of = jnp.sum(jnp.where(top_e[..., None] == experts, pad_start, 0), axis=-1)
    dest = start_of + top_pos
    tok = jnp.broadcast_to(jnp.arange(n, dtype=jnp.int32)[None, :], dest.shape)
    slot_tok = jnp.zeros((n_blocks * MOE_ROWS,), jnp.int32).at[dest.reshape(-1)].set(tok.reshape(-1))
    xb = x1b[slot_tok]
    blk_row0 = jnp.arange(n_blocks, dtype=jnp.int32) * MOE_ROWS
    blk_e = jnp.minimum(jnp.sum((pad_end[None, :] <= blk_row0[:, None]).astype(jnp.int32), axis=1),
                        N_EXPERTS - 1)
    nblk = (pad_end[-1:] // MOE_ROWS).astype(jnp.int32)
    return xb, blk_e, nblk, dest, top_w, slot_tok


def _moe_finish(x1, dispatch, w_gu, b_gu, w_down, b_down, g2, b2, tm, after=None):
    xb, blk_e, nblk, dest, top_w, _ = dispatch
    if after is not None:
        nblk, _ = lax.optimization_barrier((nblk, after))
    yb = _experts(xb, blk_e, nblk, w_gu, b_gu, w_down, b_down)
    return _final_ln(x1, top_w.T, [yb[dest[kk]] for kk in range(TOP_K)], g2, b2, tm)


def _page_copy(cache_hbm, table_ref, bi, p, buf_ref, slot, c, sem_ref):
    return pltpu.make_async_copy(cache_hbm.at[table_ref[bi, p]], buf_ref.at[slot, c, p], sem_ref.at[slot, c])


def _fetch_pages(caches, table_ref, buf_ref, sem_ref):
    b = pl.program_id(0)
    n_pages = table_ref.shape[1]
    slot = b % 2

    def start(bi, sl):
        def body(p, carry):
            for c, cache in enumerate(caches):
                _page_copy(cache, table_ref, bi, p, buf_ref, sl, c, sem_ref).start()
            return carry
        lax.fori_loop(0, n_pages, body, 0)

    @pl.when(b == 0)
    def _():
        start(0, 0)

    @pl.when(b + 1 < pl.num_programs(0))
    def _():
        start(b + 1, 1 - slot)

    def wait(p, carry):
        for c, cache in enumerate(caches):
            _page_copy(cache, table_ref, b, p, buf_ref, slot, c, sem_ref).wait()
        return carry
    lax.fori_loop(0, n_pages, wait, 0)
    return slot


def _cmp_pages_kernel(table_ref, kcache, vcache, knew_ref, vnew_ref,
                      w1k_ref, bk_ref, w2k_ref, b2k_ref, w1v_ref, bv_ref, w2v_ref, b2v_ref,
                      ko_ref, vo_ref, buf_ref, rows_ref, sem_ref):
    slot = _fetch_pages((kcache, vcache), table_ref, buf_ref, sem_ref)
    n_pages = table_ref.shape[1]
    past = n_pages * PAGE_SIZE
    n_rows = rows_ref.shape[0]
    j = n_rows // CMP_STRIDE
    first_row = lax.broadcasted_iota(jnp.int32, (n_rows - past, KV_WIDTH), 0) == 0
    plan = ((knew_ref, w1k_ref, bk_ref, w2k_ref, b2k_ref, ko_ref),
            (vnew_ref, w1v_ref, bv_ref, w2v_ref, b2v_ref, vo_ref))
    for c, (new_ref, w1_ref, bias_ref, w2_ref, b2_ref, o_ref) in enumerate(plan):
        def to_rows(p, carry):
            off = pl.multiple_of(p * PAGE_SIZE, PAGE_SIZE)
            rows_ref[pl.ds(off, PAGE_SIZE), :] = buf_ref[slot, c, p].T
            return carry
        lax.fori_loop(0, n_pages, to_rows, 0, unroll=8)
        rows_ref[past:, :] = jnp.where(first_row, new_ref[0], 0.0)
        f = None
        for p in range(0, CMP_STRIDE, 2):
            x = jnp.concatenate([rows_ref[pl.ds(p, j, stride=CMP_STRIDE), :],
                                 rows_ref[pl.ds(p + 1, j, stride=CMP_STRIDE), :]], axis=1).astype(BF16)
            part = _dot(x, w1_ref[p * KV_WIDTH:(p + 2) * KV_WIDTH, :])
            f = part if f is None else f + part
        o_ref[0, 0:j, :] = _compress_tail(f, bias_ref, w2_ref, b2_ref)
        o_ref[0, j:, :] = jnp.zeros((o_ref.shape[1] - j, KV_WIDTH), F32)


def _cmp_pages(table, kcache, vcache, knew, vnew, cw_k, cw_v):
    bs, n_pages = table.shape
    past = n_pages * PAGE_SIZE
    n_rows = past + SUBLANES * CMP_STRIDE
    j = n_rows // CMP_STRIDE
    jp = -(-j // LANES) * LANES
    full = lambda a: pl.BlockSpec(a.shape, lambda b, tbl: (0,) * a.ndim)
    new = pl.BlockSpec((1, 1, KV_WIDTH), lambda b, tbl: (b, 0, 0))
    out = pl.BlockSpec((1, jp, KV_WIDTH), lambda b, tbl: (b, 0, 0))
    grid_spec = pltpu.PrefetchScalarGridSpec(
        num_scalar_prefetch=1, grid=(bs,),
        in_specs=[pl.BlockSpec(memory_space=pl.ANY), pl.BlockSpec(memory_space=pl.ANY), new, new]
        + [full(a) for a in cw_k] + [full(a) for a in cw_v],
        out_specs=[out, out],
        scratch_shapes=[pltpu.VMEM((2, 2, n_pages, KV_WIDTH, PAGE_SIZE), F32),
                        pltpu.VMEM((n_rows, KV_WIDTH), F32), pltpu.SemaphoreType.DMA((2, 2))])
    return pl.pallas_call(
        _cmp_pages_kernel, grid_spec=grid_spec,
        out_shape=[jax.ShapeDtypeStruct((bs, jp, KV_WIDTH), F32)] * 2,
        compiler_params=_params("arbitrary"), name="cmp_pages")(
            table, kcache, vcache, knew.reshape(bs, 1, KV_WIDTH), vnew.reshape(bs, 1, KV_WIDTH), *cw_k, *cw_v)


def _nsa_sample_kernel(table_ref, kcache, vcache, q8_ref, gcol_ref, kc_ref, vc_ref, kwt_ref, vwt_ref,
                       ksn_ref, vsn_ref, kwn_ref, vwn_ref, kwc_ref, vwc_ref,
                       o_ref, kwo_ref, vwo_ref, buf_ref, expand_ref, imp_ref, sem_ref):
    b = pl.program_id(0)
    slot = _fetch_pages((kcache, vcache), table_ref, buf_ref, sem_ref)
    past = table_ref.shape[1] * PAGE_SIZE
    q_pos = past
    sel_len = -(-(past + 1) // SEL_BLOCK) * SEL_BLOCK
    nc = sel_len // CMP_STRIDE - 1
    ns = sel_len // SEL_BLOCK
    ncp = kc_ref.shape[1]
    nsp = -(-ns // SUBLANES) * SUBLANES
    nsq = -(-nsp // LANES) * LANES
    nbp = expand_ref.shape[0]
    rows8 = NSA_HEADS
    cur = q_pos // SEL_BLOCK
    n_top = min(SEL_TOP, ns)

    @pl.when(b == 0)
    def _():
        blk_i = lax.broadcasted_iota(jnp.int32, (nbp, past), 0)
        key_i = lax.broadcasted_iota(jnp.int32, (nbp, past), 1)
        expand_ref[...] = (blk_i == key_i // SEL_BLOCK).astype(BF16)

    q8 = q8_ref[0] * (NSA_HEAD_DIM ** -0.5 * math.log2(math.e))
    q8b = q8.astype(BF16)

    st = _nt_dot(kc_ref[0].astype(BF16), q8b)
    n_idx = lax.broadcasted_iota(jnp.int32, (ncp, rows8), 0)
    st = jnp.where((n_idx * CMP_STRIDE + (CMP_BLOCK - 1) <= q_pos) & (n_idx < nc), st, NEG_INF)
    mc = jnp.max(st, axis=0, keepdims=True)
    mc = jnp.where(mc > NEG_INF, mc, 0.0)
    ec = jnp.exp2(st - mc)
    pt = ec / jnp.maximum(jnp.sum(ec, axis=0, keepdims=True), 1e-30)
    o_cmp_t = _dot(vc_ref[0].T.astype(BF16), pt.astype(BF16))
    o_cmp = jnp.concatenate([o_cmp_t, jnp.zeros((KV_WIDTH, LANES - rows8), F32)], axis=1).T[0:rows8, :]

    imp = jnp.concatenate([jnp.sum(pt[:, g * NSA_GROUP:(g + 1) * NSA_GROUP], axis=1, keepdims=True)
                           for g in range(NSA_KV_HEADS)], axis=1)
    imp_ref[0:SUBLANES, :] = jnp.zeros((SUBLANES, NSA_KV_HEADS), F32)
    imp_ref[SUBLANES:SUBLANES + ncp, :] = imp
    ratio = SEL_BLOCK // CMP_STRIDE
    p_slc = imp_ref[pl.ds(SUBLANES - 1, nsp, stride=ratio), :]
    for d in range(ratio):
        p_slc = p_slc + imp_ref[pl.ds(SUBLANES + d, nsp, stride=ratio), :]
    blk = lax.broadcasted_iota(jnp.int32, (nsp, NSA_KV_HEADS), 0)
    forced = (blk == 0) | (blk > cur - N_LOCAL_SEL)
    score = jnp.where(blk <= cur, p_slc + jnp.where(forced, FORCE_BONUS, 0.0), -FORCE_BONUS)
    low = -2.0 * FORCE_BONUS
    score_pad = jnp.concatenate([score, jnp.full((nsp, LANES - NSA_KV_HEADS), low, F32)], axis=1)
    score_pad = jnp.concatenate([score_pad, jnp.full((nsq - nsp, LANES), low, F32)], axis=0)
    score_rows = score_pad.T
    jp = lax.broadcasted_iota(jnp.int32, (nsp, nsq), 0)
    jj = lax.broadcasted_iota(jnp.int32, (nsp, nsq), 1)
    sel_rows = []
    for g in range(NSA_KV_HEADS):
        col = score[:, g:g + 1]
        row = score_rows[g:g + 1, :]
        beats = (col > row) | ((col == row) & (jp < jj))
        rank = jnp.sum(beats.astype(F32), axis=0, keepdims=True)
        sel_rows.append(((rank < float(n_top)) & (jj[0:1, :] <= cur)).astype(F32))
    row_head = lax.broadcasted_iota(jnp.int32, (rows8, nsq), 0) // NSA_GROUP
    sel8 = jnp.where(row_head == 0, sel_rows[0], sel_rows[1])

    chosen = _dot(sel8[:, 0:nbp].astype(BF16), expand_ref[...])
    n_pages = table_ref.shape[1]
    kt16 = jnp.concatenate([buf_ref[slot, 0, p].astype(BF16) for p in range(n_pages)], axis=1)
    vt16 = jnp.concatenate([buf_ref[slot, 1, p].astype(BF16) for p in range(n_pages)], axis=1)
    s = _dot(q8b, kt16) + jnp.where(chosen > 0.5, 0.0, NEG_INF)
    s_new = jnp.sum(q8 * ksn_ref[0], axis=1, keepdims=True)
    m = jnp.maximum(jnp.max(s, axis=1, keepdims=True), s_new)
    p = jnp.exp2(s - m)
    p_new = jnp.exp2(s_new - m)
    o_sel = ((_nt_dot(p.astype(BF16), vt16) + p_new * vsn_ref[0])
             / (jnp.sum(p, axis=1, keepdims=True) + p_new))

    kwt = kwt_ref[0]
    vwt = vwt_ref[0]
    w = kwt.shape[1]
    lane = lax.broadcasted_iota(jnp.int32, (rows8, w), 1)
    w_pos = past - w + lane
    sw = jnp.where((w_pos > q_pos - WINDOW) & (w_pos >= 0), _dot(q8b, kwt.astype(BF16)), NEG_INF)
    sw_new = jnp.sum(q8 * kwn_ref[0], axis=1, keepdims=True)
    mw = jnp.maximum(jnp.max(sw, axis=1, keepdims=True), sw_new)
    pw = jnp.exp2(sw - mw)
    pw_new = jnp.exp2(sw_new - mw)
    o_win = ((_nt_dot(pw.astype(BF16), vwt.astype(BF16)) + pw_new * vwn_ref[0])
             / (jnp.sum(pw, axis=1, keepdims=True) + pw_new))

    gates = gcol_ref[0]
    o_ref[0] = gates[:, 0:1] * o_cmp + gates[:, 1:2] * o_sel + gates[:, 2:3] * o_win

    last = lax.broadcasted_iota(jnp.int32, (KV_WIDTH, w), 1) == w - 1
    kwo_ref[0] = jnp.where(last, kwc_ref[0], pltpu.roll(kwt, w - 1, axis=1))
    vwo_ref[0] = jnp.where(last, vwc_ref[0], pltpu.roll(vwt, w - 1, axis=1))


def _nsa_sample(table, kcache, vcache, q8, gcol, kc_sum, vc_sum, kwt, vwt, ks_new, vs_new, kw_new, vw_new):
    bs, n_pages = table.shape
    past = n_pages * PAGE_SIZE
    w = kwt.shape[2]
    ncp = kc_sum.shape[1]
    nbp = -(-(past // SEL_BLOCK) // LANES) * LANES
    per_b = lambda a: pl.BlockSpec((1,) + a.shape[1:], lambda b, tbl: (b,) + (0,) * (a.ndim - 1))
    row = lambda a: a.reshape(bs, 1, KV_WIDTH)
    col = lambda a: a.reshape(bs, KV_WIDTH, 1)
    operands = (q8, gcol, kc_sum, vc_sum, kwt, vwt, row(ks_new), row(vs_new), row(kw_new), row(vw_new),
                col(kw_new), col(vw_new))
    grid_spec = pltpu.PrefetchScalarGridSpec(
        num_scalar_prefetch=1, grid=(bs,),
        in_specs=[pl.BlockSpec(memory_space=pl.ANY), pl.BlockSpec(memory_space=pl.ANY)]
        + [per_b(a) for a in operands],
        out_specs=[pl.BlockSpec((1, NSA_HEADS, KV_WIDTH), lambda b, tbl: (b, 0, 0)),
                   pl.BlockSpec((1, KV_WIDTH, w), lambda b, tbl: (b, 0, 0)),
                   pl.BlockSpec((1, KV_WIDTH, w), lambda b, tbl: (b, 0, 0))],
        scratch_shapes=[pltpu.VMEM((2, 2, n_pages, KV_WIDTH, PAGE_SIZE), F32), pltpu.VMEM((nbp, past), BF16),
                        pltpu.VMEM((SUBLANES + ncp, NSA_KV_HEADS), F32), pltpu.SemaphoreType.DMA((2, 2))])
    return pl.pallas_call(
        _nsa_sample_kernel, grid_spec=grid_spec,
        out_shape=[jax.ShapeDtypeStruct((bs, NSA_HEADS, KV_WIDTH), F32),
                   jax.ShapeDtypeStruct((bs, KV_WIDTH, w), F32), jax.ShapeDtypeStruct((bs, KV_WIDTH, w), F32)],
        compiler_params=_params("arbitrary"), name="nsa_sample")(table, kcache, vcache, *operands)


def _hgrn_sample_kernel(h4_ref, s_ref, lb_ref, ng_ref, o_ref, so_ref):
    nb = h4_ref.shape[0]
    hd = HGRN_HEAD_DIM
    hw = HGRN_WIDTH
    hq = h4_ref[:, 0:hw]
    lb = lb_ref[...]
    f = lb + (1.0 - lb) * jax.nn.sigmoid(h4_ref[:, hw:2 * hw])
    k = 1.0 - f
    q = hq * jax.nn.sigmoid(hq)
    hg = h4_ref[:, 3 * hw:4 * hw]
    for h in range(HGRN_HEADS):
        cols = slice(h * hd, (h + 1) * hd)
        stack = jnp.concatenate([f[:, cols], k[:, cols], q[:, cols],
                                 jnp.zeros((LANES - 3 * nb, hd), F32)], axis=0).T
        outs = []
        for bi in range(nb):
            v_row = h4_ref[bi:bi + 1, 2 * hw + h * hd:2 * hw + (h + 1) * hd]
            s_new = stack[:, bi:bi + 1] * s_ref[bi, h] + stack[:, nb + bi:nb + bi + 1] * v_row
            so_ref[bi, h] = s_new
            outs.append(jnp.sum(stack[:, 2 * nb + bi:2 * nb + bi + 1] * s_new, axis=0, keepdims=True))
        o = jnp.concatenate(outs, axis=0)
        o = o * lax.rsqrt(jnp.mean(o * o, axis=-1, keepdims=True) + LN_EPS) * ng_ref[...]
        o_ref[:, cols] = o * (hg[:, cols] * jax.nn.sigmoid(hg[:, cols]))


def _hgrn_sample(h4, s0, lb, norm_g):
    bs = h4.shape[0]
    nb = SUBLANES
    st_spec = pl.BlockSpec((nb, HGRN_HEADS, HGRN_HEAD_DIM, HGRN_HEAD_DIM), lambda i: (i, 0, 0, 0))
    return pl.pallas_call(
        _hgrn_sample_kernel, grid=(bs // nb,),
        in_specs=[pl.BlockSpec((nb, 4 * HGRN_WIDTH), lambda i: (i, 0)), st_spec,
                  pl.BlockSpec((1, HGRN_WIDTH), lambda i: (0, 0)),
                  pl.BlockSpec((1, HGRN_HEAD_DIM), lambda i: (0, 0))],
        out_specs=[pl.BlockSpec((nb, HGRN_WIDTH), lambda i: (i, 0)), st_spec],
        out_shape=[jax.ShapeDtypeStruct((bs, HGRN_WIDTH), F32), jax.ShapeDtypeStruct(s0.shape, F32)],
        compiler_params=_params("parallel"), name="hgrn_sample")(h4, s0, lb, norm_g)


def kernel(x_prompt, x_sample, cache_k_cmp, cache_v_cmp, cache_k_sel, cache_v_sel, cache_k_win, cache_v_win, state_hgrn, page_table, w_in, cmp_pe, cmp_w1, cmp_b1, cmp_w2, cmp_b2, hgrn_gamma, hgrn_norm, w_branch_a, w_branch_b, w_out, ln1_g, ln1_b, w_router, b_router, w_gate_up, b_gate_up, w_down, b_down, ln2_g, ln2_b):
    l = 0
    bp, t, _ = x_prompt.shape
    bs = x_sample.shape[0]
    past = page_table.shape[1] * PAGE_SIZE
    win_keep = cache_k_win.shape[2]

    lower = jnp.cumsum(jax.nn.softmax(hgrn_gamma.astype(F32), axis=0), axis=0)[l][None, :]
    norm_g = hgrn_norm[l][None, :]
    w = w_in[l]
    c_g = NSA_WIDTH + 6 * KV_WIDTH
    n_g = 3 * NSA_HEADS
    w_main = jnp.concatenate([w[:, :c_g], w[:, c_g + n_g:]], axis=1).astype(BF16)
    w_gate_t = jnp.pad(w[:, c_g:c_g + n_g].T, ((0, GATE_ROWS - n_g), (0, 0))).astype(BF16)
    cw = [_compress_weights(cmp_pe[l, n], cmp_w1[l, n], cmp_b1[l, n], cmp_w2[l, n], cmp_b2[l, n]) for n in range(2)]
    wba = w_branch_a[l].astype(BF16)
    wbb = w_branch_b[l].astype(BF16)
    wout = w_out[l].astype(BF16)
    wr_t = w_router[l].T
    wr_hi = wr_t.astype(BF16)
    wr_lo = (wr_t - wr_hi.astype(F32)).astype(BF16)
    br = jnp.broadcast_to(b_router[l].astype(F32)[:, None], (N_EXPERTS, 1))
    g1, b1 = ln1_g[l][None, :], ln1_b[l][None, :]
    g2, b2 = ln2_g[l][None, :], ln2_b[l][None, :]

    def merge_and_dispatch(x2d, o_a, o_b, gab, tm):
        x1, x1b, logit_t = _finish(x2d, o_a, o_b, gab, wba, wbb, wout, g1, b1, wr_hi, wr_lo, br, tm)
        return x1, _moe_dispatch(x1b, logit_t)

    def experts_and_norm(x1, dispatch, tm, after=None):
        return _moe_finish(x1, dispatch, w_gate_up[l], b_gate_up[l], w_down[l], b_down[l], g2, b2, tm, after)

    n = bp * t
    xp = x_prompt.reshape(n, D_MODEL)
    q, kc, vc, ks, vs, kw, vw, h4, gab, gt, *kv_t = _project(xp, w_main, w_gate_t, 256, bp)
    r3 = lambda a: a.reshape(bp, t, a.shape[-1])
    kc_sum = _compress(r3(kc), *cw[0])
    vc_sum = _compress(r3(vc), *cw[1])
    o_nsa = _nsa_prompt(r3(q), gt, kc_sum, vc_sum, r3(ks), r3(vs), r3(kw), r3(vw))
    o_hgrn, p_state = _hgrn_prompt(r3(h4), lower, norm_g, bp, t)
    x1_p, disp_p = merge_and_dispatch(xp, o_nsa.reshape(n, NSA_WIDTH), o_hgrn.reshape(n, HGRN_WIDTH), gab, 256)
    win = min(WINDOW, t)
    t5 = lambda a: jnp.transpose(a.reshape(a.shape[0], NSA_KV_HEADS, NSA_HEAD_DIM, a.shape[2]), (0, 3, 1, 2))[None]
    new_p = (tuple(t5(a) for a in kv_t[:4]) + tuple(t5(a[:, :, t - win:]) for a in kv_t[4:])
             + (p_state[None],))

    page_table, _ = lax.optimization_barrier((page_table, disp_p[-1][:1]))
    xs = x_sample.reshape(bs, D_MODEL)
    q, kc, vc, ks, vs, kw, vw, h4, gab, gt, *_ = _project(xs, w_main, w_gate_t, bs, 1)
    pages = lambda c: jnp.transpose(c[l], (0, 2, 3, 1)).reshape(c.shape[1], KV_WIDTH, PAGE_SIZE)
    band = lambda c: jnp.transpose(c[l], (0, 2, 3, 1)).reshape(bs, KV_WIDTH, win_keep)
    kc_sum, vc_sum = _cmp_pages(page_table, pages(cache_k_cmp), pages(cache_v_cmp), kc, vc, cw[0], cw[1])
    head_eye = jnp.eye(NSA_KV_HEADS, dtype=F32)
    q8 = jnp.einsum('bgrd,gh->bgrhd', q.reshape(bs, NSA_KV_HEADS, NSA_GROUP, NSA_HEAD_DIM),
                    head_eye).reshape(bs, NSA_HEADS, KV_WIDTH)
    gcol = jnp.pad(gt[:3 * NSA_HEADS].T.reshape(bs, 3, NSA_HEADS).transpose(0, 2, 1),
                   ((0, 0), (0, 0), (0, LANES - 3)))
    o8, kw_t, vw_t = _nsa_sample(page_table, pages(cache_k_sel), pages(cache_v_sel), q8, gcol, kc_sum, vc_sum,
                                 band(cache_k_win), band(cache_v_win), ks, vs, kw, vw)
    o_nsa = jnp.einsum('bgrhd,gh->bgrd', o8.reshape(bs, NSA_KV_HEADS, NSA_GROUP, NSA_KV_HEADS, NSA_HEAD_DIM),
                       head_eye).reshape(bs, NSA_WIDTH)
    o_hgrn, s_state = _hgrn_sample(h4, state_hgrn[l], lower, norm_g)
    y_prompt = experts_and_norm(x1_p, disp_p, 256, after=(o8, s_state)).reshape(bp, t, D_MODEL)
    x1_s, disp_s = merge_and_dispatch(xs, o_nsa, o_hgrn, gab, bs)
    y_sample = experts_and_norm(x1_s, disp_s, bs).reshape(bs, 1, D_MODEL)
    s5 = lambda a: a.reshape(1, bs, 1, NSA_KV_HEADS, NSA_HEAD_DIM)
    w5 = lambda a: jnp.transpose(a.reshape(bs, NSA_KV_HEADS, NSA_HEAD_DIM, win_keep), (0, 3, 1, 2))[None]
    new_s = (s5(kc), s5(vc), s5(ks), s5(vs), w5(kw_t), w5(vw_t), s_state[None])

    return (y_prompt, y_sample) + new_p + new_s
```

```python
import functools
import math

import jax
import jax.numpy as jnp
from jax import lax
from jax.experimental import pallas as pl
from jax.experimental.pallas import tpu as pltpu

F32 = jnp.float32
BF16 = jnp.bfloat16

D_MODEL = 1024
PAGE_SIZE = 128
NSA_HEADS = 8
NSA_KV_HEADS = 2
NSA_GROUP = NSA_HEADS // NSA_KV_HEADS
NSA_HEAD_DIM = 64
NSA_WIDTH = NSA_HEADS * NSA_HEAD_DIM
KV_WIDTH = NSA_KV_HEADS * NSA_HEAD_DIM
CMP_BLOCK = 32
CMP_STRIDE = 16
CMP_HIDDEN = 128
SEL_BLOCK = 64
SEL_TOP = 16
N_LOCAL_SEL = 2
FORCE_BONUS = 1.0e4
WINDOW = 512
HGRN_HEADS = 4
HGRN_HEAD_DIM = 128
HGRN_WIDTH = HGRN_HEADS * HGRN_HEAD_DIM
HGRN_CHUNK = 16
N_EXPERTS = 32
TOP_K = 4
D_EXPERT = 1024
SWIGLU_LIMIT = 7.0
SWIGLU_ALPHA = 1.702
DEPTH = 1
DN_ALPHA = (2 * DEPTH) ** 0.25
LN_EPS = 1e-5

LANES = 128
SUBLANES = 8
VMEM_BYTES_V7X = 64 * 1024 * 1024
VMEM_LIMIT = VMEM_BYTES_V7X * 3 // 4

Q_TILE = 128
K_TILE = 128
SEL_SWEEP = 512
HGRN_TILE = 128
MOE_ROWS = 512
ROUTE_TILE = 256
NEG_INF = float("-inf")

_C_Q = 0
_C_KV = _C_Q + NSA_WIDTH
_C_H = _C_KV + 6 * KV_WIDTH
_C_GAB = _C_H + 4 * HGRN_WIDTH
_C_END = _C_GAB + 2 * D_MODEL
GATE_ROWS = 32


def _params(*sem):
    return pltpu.CompilerParams(dimension_semantics=sem, vmem_limit_bytes=VMEM_LIMIT)


def _nt_dot(a, b):
    return lax.dot_general(a, b, (((1,), (1,)), ((), ())), preferred_element_type=F32)


def _dot(a, b):
    return jnp.dot(a, b, preferred_element_type=F32)


def _proj_kernel(x_ref, w_ref, wg_ref, q_ref, kc_ref, vc_ref, ks_ref, vs_ref, kw_ref, vw_ref,
                 h_ref, gab_ref, gt_ref, *kvt_refs):
    x = x_ref[...].astype(BF16)
    q_ref[...] = _dot(x, w_ref[:, _C_Q:_C_KV])
    kv = _dot(x, w_ref[:, _C_KV:_C_H])
    for n, ref in enumerate((kc_ref, vc_ref, ks_ref, vs_ref, kw_ref, vw_ref)):
        ref[...] = kv[:, n * KV_WIDTH:(n + 1) * KV_WIDTH]
    for n, ref in enumerate(kvt_refs):
        for c in range(0, kv.shape[0], LANES):
            ref[0, :, c:c + LANES] = kv[c:c + LANES, n * KV_WIDTH:(n + 1) * KV_WIDTH].T
    h_ref[...] = _dot(x, w_ref[:, _C_H:_C_GAB])
    gab_ref[...] = _dot(x, w_ref[:, _C_GAB:_C_END])
    gt_ref[...] = jax.nn.sigmoid(_nt_dot(wg_ref[...], x))


def _project(x, w_main, w_gate_t, tm, batch):
    n = x.shape[0]
    per_b = n // batch // tm
    row = lambda w: pl.BlockSpec((tm, w), lambda i: (i, 0))
    full = lambda a: pl.BlockSpec(a.shape, lambda i: (0,) * a.ndim)
    out_shape = ([jax.ShapeDtypeStruct((n, NSA_WIDTH), F32)]
                 + [jax.ShapeDtypeStruct((n, KV_WIDTH), F32)] * 6
                 + [jax.ShapeDtypeStruct((n, 4 * HGRN_WIDTH), F32),
                    jax.ShapeDtypeStruct((n, 2 * D_MODEL), F32),
                    jax.ShapeDtypeStruct((GATE_ROWS, n), F32)]
                 + [jax.ShapeDtypeStruct((batch, KV_WIDTH, n // batch), F32)] * 6)
    out_specs = ([row(NSA_WIDTH)] + [row(KV_WIDTH)] * 6 + [row(4 * HGRN_WIDTH), row(2 * D_MODEL),
                 pl.BlockSpec((GATE_ROWS, tm), lambda i: (0, i))]
                 + [pl.BlockSpec((1, KV_WIDTH, tm), lambda i: (i // per_b, 0, i % per_b))] * 6)
    return pl.pallas_call(
        _proj_kernel, grid=(n // tm,),
        in_specs=[row(D_MODEL), full(w_main), full(w_gate_t)],
        out_specs=out_specs, out_shape=out_shape,
        compiler_params=_params("parallel"), name="in_proj")(x, w_main, w_gate_t)


def _gelu_tanh(x):
    return 0.5 * x * (1.0 + jnp.tanh(math.sqrt(2.0 / math.pi) * (x + 0.044715 * (x * x * x))))


def _compress_tail(f, bias_ref, w2_ref, b2_ref):
    j = f.shape[0]
    outs = []
    for g in range(NSA_KV_HEADS):
        base = g * 2 * CMP_HIDDEN
        first = f[:, base:base + CMP_HIDDEN]
        second = f[:, base + CMP_HIDDEN:base + 2 * CMP_HIDDEN]
        nxt = pltpu.roll(second, j - 1, axis=0)
        h = _gelu_tanh(first + nxt + bias_ref[g:g + 1, :])
        outs.append(_dot(h.astype(BF16), w2_ref[...]) + b2_ref[...])
    return jnp.concatenate(outs, axis=1)


def _compress_kernel(c_ref, w1_ref, bias_ref, w2_ref, b2_ref, o_ref):
    c = c_ref[0].astype(BF16)
    o_ref[0] = _compress_tail(_dot(c, w1_ref[...]), bias_ref, w2_ref, b2_ref)


def _compress(rows, w1full, bias, w2, b2):
    b, t, _ = rows.shape
    j = t // CMP_STRIDE
    c = rows.reshape(b, j, CMP_STRIDE * KV_WIDTH)
    full = lambda a: pl.BlockSpec(a.shape, lambda i: (0,) * a.ndim)
    return pl.pallas_call(
        _compress_kernel, grid=(b,),
        in_specs=[pl.BlockSpec((1, j, CMP_STRIDE * KV_WIDTH), lambda i: (i, 0, 0)),
                  full(w1full), full(bias), full(w2), full(b2)],
        out_specs=pl.BlockSpec((1, j, KV_WIDTH), lambda i: (i, 0, 0)),
        out_shape=jax.ShapeDtypeStruct((b, j, KV_WIDTH), F32),
        compiler_params=_params("parallel"), name="compress")(c, w1full, bias, w2, b2)


def _compress_weights(pe, w1, b1, w2, b2):
    hd, hid = NSA_HEAD_DIM, CMP_HIDDEN
    halves = w1.reshape(2, CMP_STRIDE, hd, hid)
    eye = jnp.eye(NSA_KV_HEADS, dtype=w1.dtype)
    w1full = jnp.einsum('apdh,kg->pkdgah', halves, eye).reshape(
        CMP_STRIDE * KV_WIDTH, NSA_KV_HEADS * 2 * hid).astype(BF16)
    bias = jnp.einsum('pd,pdh->h', pe, w1, precision=lax.Precision.HIGHEST) + b1
    bias = jnp.broadcast_to(bias[None, :], (SUBLANES, hid))
    return w1full, bias, w2.astype(BF16), b2.reshape(1, hd)


def _nsa_prompt_kernel(q_ref, gt_ref, kc_ref, vc_ref, ks_ref, vs_ref, kw_ref, vw_ref, o_ref,
                       ks16_ref, vst16_ref, kw16_ref, vwt16_ref, imp_ref, score_ref, selb_ref, seloff_ref,
                       m_ref, l_ref, acc_ref, sc_a_ref, sc_b_ref, mt_a_ref, mt_b_ref):
    g = pl.program_id(1)
    i = pl.program_id(2)
    t0 = pl.multiple_of(i * Q_TILE, Q_TILE)
    rows = NSA_GROUP * Q_TILE
    nc = kc_ref.shape[1]
    ns = score_ref.shape[0]
    t_len = ks_ref.shape[1]
    hd = NSA_HEAD_DIM

    @pl.when((g == 0) & (i == 0))
    def _():
        def cast_step(c, carry):
            r0 = pl.multiple_of(c * K_TILE, K_TILE)
            ks16_ref[pl.ds(r0, K_TILE), :] = ks_ref[0, pl.ds(r0, K_TILE), :].astype(BF16)
            kw16_ref[pl.ds(r0, K_TILE), :] = kw_ref[0, pl.ds(r0, K_TILE), :].astype(BF16)
            vst16_ref[:, pl.ds(r0, K_TILE)] = vs_ref[0, pl.ds(r0, K_TILE), :].T.astype(BF16)
            vwt16_ref[:, pl.ds(r0, K_TILE)] = vw_ref[0, pl.ds(r0, K_TILE), :].T.astype(BF16)
            return carry
        lax.fori_loop(0, t_len // K_TILE, cast_step, 0)

    q = q_ref[0] * (hd ** -0.5 * math.log2(math.e))
    lane_head = lax.broadcasted_iota(jnp.int32, (Q_TILE, KV_WIDTH), 1) // NSA_HEAD_DIM
    parts = []
    for r in range(NSA_GROUP):
        qr = q[:, r * NSA_HEAD_DIM:(r + 1) * NSA_HEAD_DIM]
        parts.append(jnp.where(lane_head == g, jnp.concatenate([qr, qr], axis=1), 0.0))
    qs = jnp.concatenate(parts, axis=0).astype(BF16)

    tok = t0 + lax.broadcasted_iota(jnp.int32, (1, Q_TILE), 1)

    def tile4(bias):
        return jnp.concatenate([bias] * NSA_GROUP, axis=1)

    head_rows = pl.ds(pl.multiple_of(g * hd, hd), hd)

    wk = WINDOW + Q_TILE
    w0 = pl.multiple_of(jnp.maximum(t0 - WINDOW, 0), Q_TILE)
    s = _nt_dot(kc_ref[0].astype(BF16), qs)
    sw = _nt_dot(kw16_ref[pl.ds(w0, wk), :], qs)
    sd = _nt_dot(ks16_ref[pl.ds(t0, Q_TILE), :], qs)

    last_ok = jnp.minimum(jnp.right_shift(tok - (CMP_BLOCK - 1), 4), nc - 2)
    n_idx = lax.broadcasted_iota(jnp.int32, (nc, Q_TILE), 0)
    s = s + tile4(jnp.where(n_idx <= last_ok, 0.0, NEG_INF))
    m = jnp.max(s, axis=0, keepdims=True)
    m = jnp.where(m > NEG_INF, m, 0.0)
    e = jnp.exp2(s - m)
    p = e / jnp.maximum(jnp.sum(e, axis=0, keepdims=True), 1e-30)

    key_w = w0 + lax.broadcasted_iota(jnp.int32, (wk, Q_TILE), 0)
    sw = sw + tile4(jnp.where(lax.bitcast_convert_type(tok - key_w, jnp.uint32) < WINDOW, 0.0, NEG_INF))
    pw = jnp.exp2(sw - jnp.max(sw, axis=0, keepdims=True))

    key_d = t0 + lax.broadcasted_iota(jnp.int32, (Q_TILE, Q_TILE), 0)
    sd = sd + tile4(jnp.where(key_d <= tok, 0.0, NEG_INF))
    md = jnp.max(sd, axis=0, keepdims=True)
    pd = jnp.exp2(sd - md)

    o_cmp = _dot(vc_ref[0].T.astype(BF16), p.astype(BF16))
    o_cmp = jnp.where(g == 0, o_cmp[:hd], o_cmp[hd:])
    def pv_and_sum(vt_ref, k0, size, probs):
        vt = jnp.concatenate([vt_ref[head_rows, pl.ds(k0, size)], jnp.ones((2 * SUBLANES, size), BF16)], axis=0)
        r = _dot(vt, probs.astype(BF16))
        return r[:hd], r[hd:hd + 1]

    pv_w, l_w = pv_and_sum(vwt16_ref, w0, wk, pw)
    o_win = pv_w / l_w
    pv_d, l_d = pv_and_sum(vst16_ref, t0, Q_TILE, pd)
    m_ref[...] = md
    l_ref[...] = l_d
    acc_ref[...] = pv_d

    imp = p[:, 0:Q_TILE]
    for r in range(1, NSA_GROUP):
        imp = imp + p[:, r * Q_TILE:(r + 1) * Q_TILE]
    imp_ref[0:SUBLANES, :] = jnp.zeros((SUBLANES, Q_TILE), F32)
    imp_ref[SUBLANES:SUBLANES + nc, :] = imp
    ratio = SEL_BLOCK // CMP_STRIDE
    p_slc = imp_ref[pl.ds(SUBLANES - 1, ns, stride=ratio), :]
    for d in range(ratio):
        p_slc = p_slc + imp_ref[pl.ds(SUBLANES + d, ns, stride=ratio), :]
    blk = lax.broadcasted_iota(jnp.int32, (ns, Q_TILE), 0)
    cur = tok // SEL_BLOCK
    forced = (blk == 0) | (blk > cur - N_LOCAL_SEL)
    score = jnp.where(blk <= cur, p_slc + jnp.where(forced, FORCE_BONUS, 0.0), -FORCE_BONUS)
    score_ref[...] = score

    def rank_step(jp, rank):
        row = score_ref[pl.ds(jp, 1), :]
        beats = (row > score) | ((row == score) & (jp < blk))
        return rank + beats.astype(F32)
    n_top = min(SEL_TOP, ns)
    visible = blk <= cur

    @pl.when(t0 + Q_TILE <= n_top * SEL_BLOCK)
    def _():
        selb_ref[...] = jnp.where(visible, 0.0, NEG_INF)

    @pl.when(t0 + Q_TILE > n_top * SEL_BLOCK)
    def _():
        rank = lax.fori_loop(0, ns, rank_step, jnp.zeros((ns, Q_TILE), F32), unroll=8)
        selb_ref[...] = jnp.where((rank < float(n_top)) & visible, 0.0, NEG_INF)

    first_blk = i * (Q_TILE // SEL_BLOCK)
    seloff_ref[...] = jnp.where(blk < first_blk, selb_ref[...], NEG_INF)
    per_tile = SEL_SWEEP // SEL_BLOCK
    n_sweep = (t0 + SEL_SWEEP - 1) // SEL_SWEEP

    last_tile = t_len // SEL_SWEEP - 1

    def score_tile(kt, s_ref, mx_ref):
        k0 = pl.multiple_of(kt * SEL_SWEEP, SEL_SWEEP)
        bias = jnp.concatenate(
            [jnp.broadcast_to(seloff_ref[pl.ds(kt * per_tile + j, 1), :], (SEL_BLOCK, Q_TILE))
             for j in range(per_tile)], axis=0)
        sc = _nt_dot(ks16_ref[pl.ds(k0, SEL_SWEEP), :], qs) + tile4(bias)
        s_ref[...] = sc
        mx_ref[...] = jnp.max(sc, axis=0, keepdims=True)

    def consume_tile(kt, s_ref, mx_ref):
        k0 = pl.multiple_of(kt * SEL_SWEEP, SEL_SWEEP)
        m_new = jnp.maximum(m_ref[...], mx_ref[...])
        alpha = jnp.exp2(m_ref[...] - m_new)
        pv, l_new = pv_and_sum(vst16_ref, k0, SEL_SWEEP, jnp.exp2(s_ref[...] - m_new))
        l_ref[...] = alpha * l_ref[...] + l_new
        acc_ref[...] = alpha * acc_ref[...] + pv
        m_ref[...] = m_new

    @pl.when(n_sweep > 0)
    def _():
        score_tile(0, sc_a_ref, mt_a_ref)

    def sel_step(kp, carry):
        kt = kp * 2
        score_tile(jnp.minimum(kt + 1, last_tile), sc_b_ref, mt_b_ref)
        consume_tile(kt, sc_a_ref, mt_a_ref)
        score_tile(jnp.minimum(kt + 2, last_tile), sc_a_ref, mt_a_ref)
        consume_tile(jnp.minimum(kt + 1, last_tile), sc_b_ref, mt_b_ref)
        return carry
    lax.fori_loop(0, (n_sweep + 1) // 2, sel_step, 0)
    o_sel = acc_ref[...] / l_ref[...]

    def gate(branch):
        gr = gt_ref[pl.ds(branch * NSA_HEADS + g * NSA_GROUP, NSA_GROUP), :]
        return jnp.concatenate([gr[r:r + 1, :] for r in range(NSA_GROUP)], axis=1)
    o_t = gate(0) * o_cmp + gate(1) * o_sel + gate(2) * o_win
    outs = []
    for r in range(0, NSA_GROUP, 2):
        pair = jnp.concatenate([o_t[:, r * Q_TILE:(r + 1) * Q_TILE],
                                o_t[:, (r + 1) * Q_TILE:(r + 2) * Q_TILE]], axis=0)
        outs.append(pair.T)
    o_ref[0] = jnp.concatenate(outs, axis=1)


def _nsa_prompt(q, gt, kc_sum, vc_sum, ks, vs, kw, vw):
    b, t, _ = q.shape
    assert t % (2 * SEL_SWEEP) == 0 and t >= WINDOW + Q_TILE, t
    nt = t // Q_TILE
    nc = kc_sum.shape[1]
    ns = t // SEL_BLOCK
    rows = NSA_GROUP * Q_TILE
    per_b = lambda a: pl.BlockSpec((1,) + a.shape[1:], lambda bi, g, i: (bi, 0, 0))
    return pl.pallas_call(
        _nsa_prompt_kernel, grid=(b, NSA_KV_HEADS, nt),
        in_specs=[pl.BlockSpec((1, Q_TILE, NSA_WIDTH // NSA_KV_HEADS), lambda bi, g, i: (bi, i, g)),
                  pl.BlockSpec((GATE_ROWS, Q_TILE), lambda bi, g, i: (0, bi * nt + i)),
                  per_b(kc_sum), per_b(vc_sum), per_b(ks), per_b(vs), per_b(kw), per_b(vw)],
        out_specs=pl.BlockSpec((1, Q_TILE, NSA_WIDTH // NSA_KV_HEADS), lambda bi, g, i: (bi, i, g)),
        out_shape=jax.ShapeDtypeStruct((b, t, NSA_WIDTH), F32),
        scratch_shapes=[pltpu.VMEM((t, KV_WIDTH), BF16),
                        pltpu.VMEM((KV_WIDTH, t), BF16),
                        pltpu.VMEM((t, KV_WIDTH), BF16),
                        pltpu.VMEM((KV_WIDTH, t), BF16),
                        pltpu.VMEM((SUBLANES + nc, Q_TILE), F32),
                        pltpu.VMEM((ns, Q_TILE), F32),
                        pltpu.VMEM((ns, Q_TILE), F32),
                        pltpu.VMEM((ns, Q_TILE), F32),
                        pltpu.VMEM((1, rows), F32),
                        pltpu.VMEM((1, rows), F32),
                        pltpu.VMEM((NSA_HEAD_DIM, rows), F32),
                        pltpu.VMEM((SEL_SWEEP, rows), F32),
                        pltpu.VMEM((SEL_SWEEP, rows), F32),
                        pltpu.VMEM((1, rows), F32),
                        pltpu.VMEM((1, rows), F32)],
        compiler_params=_params("arbitrary", "arbitrary", "arbitrary"),
        name="nsa_prompt")(q, gt, kc_sum, vc_sum, ks, vs, kw, vw)


def _hgrn_prompt_kernel(hq_ref, hf_ref, hi_ref, hg_ref, lb_ref, ng_ref, o_ref, st_out_ref, st_ref):
    c = pl.program_id(1)
    n = HGRN_TILE
    sub = HGRN_CHUNK
    hd = HGRN_HEAD_DIM

    @pl.when(c == 0)
    def _():
        st_ref[...] = jnp.zeros(st_ref.shape, F32)

    pos = lax.broadcasted_iota(jnp.int32, (n, hd), 0) % sub
    rc = lax.broadcasted_iota(jnp.int32, (n, n), 0)
    cc = lax.broadcasted_iota(jnp.int32, (n, n), 1)
    intra = (rc // sub == cc // sub) & (cc <= rc)
    tok_chunk = lax.broadcasted_iota(jnp.int32, (hd, n), 1) // sub

    for h in range(HGRN_HEADS):
        cols = slice(h * hd, (h + 1) * hd)
        lb = lb_ref[:, cols]
        f = lb + (1.0 - lb) * jax.nn.sigmoid(hf_ref[0, :, cols])
        logf = jnp.log(f)
        b = logf
        suf = logf
        sh = 1
        while sh < sub:
            b = b + jnp.where(pos >= sh, pltpu.roll(b, sh, axis=0), 0.0)
            suf = suf + jnp.where(pos + sh < sub, pltpu.roll(suf, n - sh, axis=0), 0.0)
            sh *= 2
        hq = hq_ref[0, :, cols]
        k = 1.0 - f
        q_dec = (hq * jax.nn.sigmoid(hq) * jnp.exp(b)).astype(BF16)
        k_inv = (k * jnp.exp(-b)).astype(BF16)
        k_end = (k * jnp.exp(suf - logf)).astype(BF16)
        v = hi_ref[0, :, cols]

        a = jnp.where(intra, _nt_dot(q_dec, k_inv), 0.0)
        o = _dot(a.astype(BF16), v.astype(BF16))

        vt = v.T
        kv_t = [_dot(jnp.where(tok_chunk == ci, vt, 0.0).astype(BF16), k_end)
                for ci in range(n // sub)]
        states = [st_ref[h]]
        for ci in range(n // sub):
            states.append(states[-1] * jnp.exp(suf[ci * sub:ci * sub + 1, :]) + kv_t[ci])
        st = states[-1]
        st_ref[h] = st
        inter = [_nt_dot(q_dec[ci * sub:(ci + 1) * sub, :], states[ci].astype(BF16))
                 for ci in range(n // sub)]
        o = o + jnp.concatenate(inter, axis=0)
        o = o * lax.rsqrt(jnp.mean(o * o, axis=-1, keepdims=True) + LN_EPS) * ng_ref[...]
        hg = hg_ref[0, :, cols]
        o_ref[0, :, cols] = o * (hg * jax.nn.sigmoid(hg))

    @pl.when(c == pl.num_programs(1) - 1)
    def _():
        for h in range(HGRN_HEADS):
            st_out_ref[0, h] = st_ref[h].T


def _hgrn_prompt(h4, lb, norm_g, b, t):
    nchunk = t // HGRN_TILE
    part = lambda p: pl.BlockSpec((1, HGRN_TILE, HGRN_WIDTH), lambda bi, c: (bi, c, p))
    return pl.pallas_call(
        _hgrn_prompt_kernel, grid=(b, nchunk),
        in_specs=[part(0), part(1), part(2), part(3),
                  pl.BlockSpec((1, HGRN_WIDTH), lambda bi, c: (0, 0)),
                  pl.BlockSpec((1, HGRN_HEAD_DIM), lambda bi, c: (0, 0))],
        out_specs=[pl.BlockSpec((1, HGRN_TILE, HGRN_WIDTH), lambda bi, c: (bi, c, 0)),
                   pl.BlockSpec((1, HGRN_HEADS, HGRN_HEAD_DIM, HGRN_HEAD_DIM), lambda bi, c: (bi, 0, 0, 0))],
        out_shape=[jax.ShapeDtypeStruct((b, t, HGRN_WIDTH), F32),
                   jax.ShapeDtypeStruct((b, HGRN_HEADS, HGRN_HEAD_DIM, HGRN_HEAD_DIM), F32)],
        scratch_shapes=[pltpu.VMEM((HGRN_HEADS, HGRN_HEAD_DIM, HGRN_HEAD_DIM), F32)],
        compiler_params=_params("parallel", "arbitrary"),
        name="hgrn_prompt")(h4, h4, h4, h4, lb, norm_g)


def _layer_norm(y, g, b):
    mu = jnp.mean(y, axis=-1, keepdims=True)
    d = y - mu
    var = jnp.mean(d * d, axis=-1, keepdims=True)
    return d * lax.rsqrt(var + LN_EPS) * g + b


def _finish_kernel(x_ref, oa_ref, ob_ref, gab_ref, wba_ref, wbb_ref, wout_ref, g_ref, b_ref,
                   wr_hi_ref, wr_lo_ref, br_ref, x1_ref, x1b_ref, lt_ref):
    tm = x_ref.shape[0]
    halves = [slice(0, tm // 2), slice(tm // 2, tm)] if tm % (2 * LANES) == 0 else [slice(0, tm)]
    ab = [(_dot(oa_ref[r, :].astype(BF16), wba_ref[...]), _dot(ob_ref[r, :].astype(BF16), wbb_ref[...]))
          for r in halves]
    mixes = [(jax.nn.sigmoid(gab_ref[r, :D_MODEL]) * a + jax.nn.sigmoid(gab_ref[r, D_MODEL:]) * bb).astype(BF16)
             for r, (a, bb) in zip(halves, ab)]
    outs = [_dot(mix, wout_ref[...]) for mix in mixes]
    his, los = [], []
    for r, out in zip(halves, outs):
        x1 = _layer_norm(DN_ALPHA * x_ref[r, :] + out, g_ref[...], b_ref[...])
        x1_ref[r, :] = x1
        hi = x1.astype(BF16)
        x1b_ref[r, :] = hi
        his.append(hi)
        los.append((x1 - hi.astype(F32)).astype(BF16))
    for r, hi, lo in zip(halves, his, los):
        lt_ref[:, r] = (_nt_dot(wr_hi_ref[...], hi) + _nt_dot(wr_hi_ref[...], lo)
                        + _nt_dot(wr_lo_ref[...], hi) + br_ref[...])


def _finish(x, oa, ob, gab, wba, wbb, wout, g1, b1, wr_hi, wr_lo, br, tm):
    n = x.shape[0]
    row = lambda w: pl.BlockSpec((tm, w), lambda i: (i, 0))
    full = lambda a: pl.BlockSpec(a.shape, lambda i: (0,) * a.ndim)
    return pl.pallas_call(
        _finish_kernel, grid=(n // tm,),
        in_specs=[row(D_MODEL), row(NSA_WIDTH), row(HGRN_WIDTH), row(2 * D_MODEL),
                  full(wba), full(wbb), full(wout), full(g1), full(b1), full(wr_hi), full(wr_lo), full(br)],
        out_specs=[row(D_MODEL), row(D_MODEL), pl.BlockSpec((N_EXPERTS, tm), lambda i: (0, i))],
        out_shape=[jax.ShapeDtypeStruct((n, D_MODEL), F32), jax.ShapeDtypeStruct((n, D_MODEL), BF16),
                   jax.ShapeDtypeStruct((N_EXPERTS, n), F32)],
        compiler_params=_params("parallel"), name="merge_ln_router")(
            x, oa, ob, gab, wba, wbb, wout, g1, b1, wr_hi, wr_lo, br)


def _route_kernel(lt_ref, tri_ref, e_ref, w_ref, pos_ref, cnt_ref, carry_ref):
    i = pl.program_id(0)

    @pl.when(i == 0)
    def _():
        carry_ref[...] = jnp.zeros(carry_ref.shape, F32)

    logit = lt_ref[...]
    tn = logit.shape[1]
    eid = lax.broadcasted_iota(jnp.int32, (N_EXPERTS, tn), 0)
    rank = jnp.zeros((N_EXPERTS, tn), F32)
    for ep in range(N_EXPERTS):
        row = logit[ep:ep + 1, :]
        rank = rank + ((row > logit) | ((row == logit) & (ep < eid))).astype(F32)
    sel = rank < float(TOP_K)
    top = jnp.max(logit, axis=0, keepdims=True)
    ex = jnp.where(sel, jnp.exp(logit - top), 0.0)
    wgt = ex / jnp.sum(ex, axis=0, keepdims=True)
    self = sel.astype(F32)
    incl = _dot(self.astype(BF16), tri_ref[...])
    pos = carry_ref[:, 0:1] + incl - self
    carry_ref[...] = carry_ref[...] + jnp.sum(self, axis=1, keepdims=True)
    eid_f = eid.astype(F32)
    for kk in range(TOP_K):
        pick = sel & (rank == float(kk))
        e_ref[kk:kk + 1, :] = jnp.sum(jnp.where(pick, eid_f, 0.0), axis=0, keepdims=True).astype(jnp.int32)
        w_ref[kk:kk + 1, :] = jnp.sum(jnp.where(pick, wgt, 0.0), axis=0, keepdims=True)
        pos_ref[kk:kk + 1, :] = jnp.sum(jnp.where(pick, pos, 0.0), axis=0, keepdims=True).astype(jnp.int32)
    cnt_ref[...] = carry_ref[...]


def _route(logit_t):
    n = logit_t.shape[1]
    tn = ROUTE_TILE if n % ROUTE_TILE == 0 else n
    tri = (lax.broadcasted_iota(jnp.int32, (tn, tn), 0) <= lax.broadcasted_iota(jnp.int32, (tn, tn), 1)).astype(BF16)
    col = lambda r: pl.BlockSpec((r, tn), lambda i: (0, i))
    return pl.pallas_call(
        _route_kernel, grid=(n // tn,),
        in_specs=[col(N_EXPERTS), pl.BlockSpec((tn, tn), lambda i: (0, 0))],
        out_specs=[col(TOP_K), col(TOP_K), col(TOP_K), pl.BlockSpec((N_EXPERTS, LANES), lambda i: (0, 0))],
        out_shape=[jax.ShapeDtypeStruct((TOP_K, n), jnp.int32), jax.ShapeDtypeStruct((TOP_K, n), F32),
                   jax.ShapeDtypeStruct((TOP_K, n), jnp.int32), jax.ShapeDtypeStruct((N_EXPERTS, LANES), F32)],
        scratch_shapes=[pltpu.VMEM((N_EXPERTS, LANES), F32)],
        compiler_params=_params("arbitrary"), name="route")(logit_t, tri)


def _expert_kernel(blk_e_ref, nblk_ref, xb_ref, wgu_ref, bgu_ref, wd_ref, bd_ref, y_ref, wgu_b, wd_b):
    i = pl.program_id(0)
    live = i < nblk_ref[0]

    @pl.when(live & ((i == 0) | (blk_e_ref[i] != blk_e_ref[jnp.maximum(i - 1, 0)])))
    def _():
        wgu_b[...] = wgu_ref[0].astype(BF16)
        wd_b[...] = wd_ref[0].astype(BF16)

    @pl.when(live)
    def _():
        gu = _dot(xb_ref[...], wgu_b[...]) + bgu_ref[0]
        gate = jnp.minimum(gu[:, :D_EXPERT], SWIGLU_LIMIT)
        up = jnp.clip(gu[:, D_EXPERT:], -SWIGLU_LIMIT, SWIGLU_LIMIT)
        h = gate * jax.nn.sigmoid(SWIGLU_ALPHA * gate) * (up + 1.0)
        y_ref[...] = _dot(h.astype(BF16), wd_b[...]) + bd_ref[0]

    @pl.when(i >= nblk_ref[0])
    def _():
        y_ref[...] = jnp.zeros(y_ref.shape, F32)


def _experts(xb, blk_e, nblk, w_gu, b_gu, w_down, b_down):
    m = xb.shape[0]
    grid_spec = pltpu.PrefetchScalarGridSpec(
        num_scalar_prefetch=2, grid=(m // MOE_ROWS,),
        in_specs=[pl.BlockSpec((MOE_ROWS, D_MODEL), lambda i, be, nb: (i, 0)),
                  pl.BlockSpec((1, D_MODEL, 2 * D_EXPERT), lambda i, be, nb: (be[i], 0, 0)),
                  pl.BlockSpec((1, 1, 2 * D_EXPERT), lambda i, be, nb: (be[i], 0, 0)),
                  pl.BlockSpec((1, D_EXPERT, D_MODEL), lambda i, be, nb: (be[i], 0, 0)),
                  pl.BlockSpec((1, 1, D_MODEL), lambda i, be, nb: (be[i], 0, 0))],
        out_specs=pl.BlockSpec((MOE_ROWS, D_MODEL), lambda i, be, nb: (i, 0)),
        scratch_shapes=[pltpu.VMEM((D_MODEL, 2 * D_EXPERT), BF16), pltpu.VMEM((D_EXPERT, D_MODEL), BF16)])
    return pl.pallas_call(
        _expert_kernel, grid_spec=grid_spec,
        out_shape=jax.ShapeDtypeStruct((m, D_MODEL), F32),
        compiler_params=_params("arbitrary"), name="experts")(
            blk_e, nblk, xb, w_gu, b_gu.reshape(N_EXPERTS, 1, 2 * D_EXPERT),
            w_down, b_down.reshape(N_EXPERTS, 1, D_MODEL))


def _final_ln_kernel(x1_ref, w_ref, y0_ref, y1_ref, y2_ref, y3_ref, g_ref, b_ref, o_ref):
    w = w_ref[...]
    y = y0_ref[...] * w[:, 0:1]
    for kk, y_ref in enumerate((y1_ref, y2_ref, y3_ref), start=1):
        y = y + y_ref[...] * w[:, kk:kk + 1]
    o_ref[...] = _layer_norm(DN_ALPHA * x1_ref[...] + y, g_ref[...], b_ref[...])


def _final_ln(x1, w_tok, ys, g2, b2, tm):
    n = x1.shape[0]
    row = pl.BlockSpec((tm, D_MODEL), lambda i: (i, 0))
    vec = pl.BlockSpec((1, D_MODEL), lambda i: (0, 0))
    return pl.pallas_call(
        _final_ln_kernel, grid=(n // tm,),
        in_specs=[row, pl.BlockSpec((tm, TOP_K), lambda i: (i, 0))] + [row] * TOP_K + [vec, vec],
        out_specs=row, out_shape=jax.ShapeDtypeStruct((n, D_MODEL), F32),
        compiler_params=_params("parallel"), name="final_ln")(x1, w_tok, *ys, g2, b2)


def _moe_dispatch(x1b, logit_t):
    n = x1b.shape[0]
    top_e, top_w, top_pos, counts = _route(logit_t)
    counts = counts[:, 0].astype(jnp.int32)
    padded = (counts + MOE_ROWS - 1) // MOE_ROWS * MOE_ROWS
    pad_end = jnp.cumsum(padded)
    pad_start = pad_end - padded
    n_blocks = -(-(n * TOP_K + N_EXPERTS * (MOE_ROWS - 1)) // MOE_ROWS)
    experts = jnp.arange(N_EXPERTS, dtype=jnp.int32)
    start_of = jnp.sum(jnp.where(top_e[..., None] == experts, pad_start, 0), axis=-1)
    dest = start_of + top_pos
    tok = jnp.broadcast_to(jnp.arange(n, dtype=jnp.int32)[None, :], dest.shape)
    slot_tok = jnp.zeros((n_blocks * MOE_ROWS,), jnp.int32).at[dest.reshape(-1)].set(tok.reshape(-1))
    xb = x1b[slot_tok]
    blk_row0 = jnp.arange(n_blocks, dtype=jnp.int32) * MOE_ROWS
    blk_e = jnp.minimum(jnp.sum((pad_end[None, :] <= blk_row0[:, None]).astype(jnp.int32), axis=1),
                        N_EXPERTS - 1)
    nblk = (pad_end[-1:] // MOE_ROWS).astype(jnp.int32)
    return xb, blk_e, nblk, dest, top_w, slot_tok


def _moe_finish(x1, dispatch, w_gu, b_gu, w_down, b_down, g2, b2, tm, after=None):
    xb, blk_e, nblk, dest, top_w, _ = dispatch
    if after is not None:
        nblk, _ = lax.optimization_barrier((nblk, after))
    yb = _experts(xb, blk_e, nblk, w_gu, b_gu, w_down, b_down)
    return _final_ln(x1, top_w.T, [yb[dest[kk]] for kk in range(TOP_K)], g2, b2, tm)


def _page_copy(cache_hbm, table_ref, bi, p, buf_ref, slot, c, sem_ref):
    return pltpu.make_async_copy(cache_hbm.at[table_ref[bi, p]], buf_ref.at[slot, c, p], sem_ref.at[slot, c])


def _fetch_pages(caches, table_ref, buf_ref, sem_ref):
    b = pl.program_id(0)
    n_pages = table_ref.shape[1]
    slot = b % 2

    def start(bi, sl):
        def body(p, carry):
            for c, cache in enumerate(caches):
                _page_copy(cache, table_ref, bi, p, buf_ref, sl, c, sem_ref).start()
            return carry
        lax.fori_loop(0, n_pages, body, 0)

    @pl.when(b == 0)
    def _():
        start(0, 0)

    @pl.when(b + 1 < pl.num_programs(0))
    def _():
        start(b + 1, 1 - slot)

    def wait(p, carry):
        for c, cache in enumerate(caches):
            _page_copy(cache, table_ref, b, p, buf_ref, slot, c, sem_ref).wait()
        return carry
    lax.fori_loop(0, n_pages, wait, 0)
    return slot


def _cmp_pages_kernel(table_ref, kcache, vcache, knew_ref, vnew_ref,
                      w1k_ref, bk_ref, w2k_ref, b2k_ref, w1v_ref, bv_ref, w2v_ref, b2v_ref,
                      ko_ref, vo_ref, buf_ref, rows_ref, sem_ref):
    slot = _fetch_pages((kcache, vcache), table_ref, buf_ref, sem_ref)
    n_pages = table_ref.shape[1]
    past = n_pages * PAGE_SIZE
    n_rows = rows_ref.shape[0]
    j = n_rows // CMP_STRIDE
    first_row = lax.broadcasted_iota(jnp.int32, (n_rows - past, KV_WIDTH), 0) == 0
    plan = ((knew_ref, w1k_ref, bk_ref, w2k_ref, b2k_ref, ko_ref),
            (vnew_ref, w1v_ref, bv_ref, w2v_ref, b2v_ref, vo_ref))
    for c, (new_ref, w1_ref, bias_ref, w2_ref, b2_ref, o_ref) in enumerate(plan):
        def to_rows(p, carry):
            off = pl.multiple_of(p * PAGE_SIZE, PAGE_SIZE)
            rows_ref[pl.ds(off, PAGE_SIZE), :] = buf_ref[slot, c, p].T
            return carry
        lax.fori_loop(0, n_pages, to_rows, 0, unroll=8)
        rows_ref[past:, :] = jnp.where(first_row, new_ref[0], 0.0)
        f = None
        for p in range(0, CMP_STRIDE, 2):
            x = jnp.concatenate([rows_ref[pl.ds(p, j, stride=CMP_STRIDE), :],
                                 rows_ref[pl.ds(p + 1, j, stride=CMP_STRIDE), :]], axis=1).astype(BF16)
            part = _dot(x, w1_ref[p * KV_WIDTH:(p + 2) * KV_WIDTH, :])
            f = part if f is None else f + part
        o_ref[0, 0:j, :] = _compress_tail(f, bias_ref, w2_ref, b2_ref)
        o_ref[0, j:, :] = jnp.zeros((o_ref.shape[1] - j, KV_WIDTH), F32)


def _cmp_pages(table, kcache, vcache, knew, vnew, cw_k, cw_v):
    bs, n_pages = table.shape
    past = n_pages * PAGE_SIZE
    n_rows = past + SUBLANES * CMP_STRIDE
    j = n_rows // CMP_STRIDE
    jp = -(-j // LANES) * LANES
    full = lambda a: pl.BlockSpec(a.shape, lambda b, tbl: (0,) * a.ndim)
    new = pl.BlockSpec((1, 1, KV_WIDTH), lambda b, tbl: (b, 0, 0))
    out = pl.BlockSpec((1, jp, KV_WIDTH), lambda b, tbl: (b, 0, 0))
    grid_spec = pltpu.PrefetchScalarGridSpec(
        num_scalar_prefetch=1, grid=(bs,),
        in_specs=[pl.BlockSpec(memory_space=pl.ANY), pl.BlockSpec(memory_space=pl.ANY), new, new]
        + [full(a) for a in cw_k] + [full(a) for a in cw_v],
        out_specs=[out, out],
        scratch_shapes=[pltpu.VMEM((2, 2, n_pages, KV_WIDTH, PAGE_SIZE), F32),
                        pltpu.VMEM((n_rows, KV_WIDTH), F32), pltpu.SemaphoreType.DMA((2, 2))])
    return pl.pallas_call(
        _cmp_pages_kernel, grid_spec=grid_spec,
        out_shape=[jax.ShapeDtypeStruct((bs, jp, KV_WIDTH), F32)] * 2,
        compiler_params=_params("arbitrary"), name="cmp_pages")(
            table, kcache, vcache, knew.reshape(bs, 1, KV_WIDTH), vnew.reshape(bs, 1, KV_WIDTH), *cw_k, *cw_v)


def _nsa_sample_kernel(table_ref, kcache, vcache, q8_ref, gcol_ref, kc_ref, vc_ref, kwt_ref, vwt_ref,
                       ksn_ref, vsn_ref, kwn_ref, vwn_ref, kwc_ref, vwc_ref,
                       o_ref, kwo_ref, vwo_ref, buf_ref, expand_ref, imp_ref, sem_ref):
    b = pl.program_id(0)
    slot = _fetch_pages((kcache, vcache), table_ref, buf_ref, sem_ref)
    past = table_ref.shape[1] * PAGE_SIZE
    q_pos = past
    sel_len = -(-(past + 1) // SEL_BLOCK) * SEL_BLOCK
    nc = sel_len // CMP_STRIDE - 1
    ns = sel_len // SEL_BLOCK
    ncp = kc_ref.shape[1]
    nsp = -(-ns // SUBLANES) * SUBLANES
    nsq = -(-nsp // LANES) * LANES
    nbp = expand_ref.shape[0]
    rows8 = NSA_HEADS
    cur = q_pos // SEL_BLOCK
    n_top = min(SEL_TOP, ns)

    @pl.when(b == 0)
    def _():
        blk_i = lax.broadcasted_iota(jnp.int32, (nbp, past), 0)
        key_i = lax.broadcasted_iota(jnp.int32, (nbp, past), 1)
        expand_ref[...] = (blk_i == key_i // SEL_BLOCK).astype(BF16)

    q8 = q8_ref[0] * (NSA_HEAD_DIM ** -0.5 * math.log2(math.e))
    q8b = q8.astype(BF16)

    st = _nt_dot(kc_ref[0].astype(BF16), q8b)
    n_idx = lax.broadcasted_iota(jnp.int32, (ncp, rows8), 0)
    st = jnp.where((n_idx * CMP_STRIDE + (CMP_BLOCK - 1) <= q_pos) & (n_idx < nc), st, NEG_INF)
    mc = jnp.max(st, axis=0, keepdims=True)
    mc = jnp.where(mc > NEG_INF, mc, 0.0)
    ec = jnp.exp2(st - mc)
    pt = ec / jnp.maximum(jnp.sum(ec, axis=0, keepdims=True), 1e-30)
    o_cmp_t = _dot(vc_ref[0].T.astype(BF16), pt.astype(BF16))
    o_cmp = jnp.concatenate([o_cmp_t, jnp.zeros((KV_WIDTH, LANES - rows8), F32)], axis=1).T[0:rows8, :]

    imp = jnp.concatenate([jnp.sum(pt[:, g * NSA_GROUP:(g + 1) * NSA_GROUP], axis=1, keepdims=True)
                           for g in range(NSA_KV_HEADS)], axis=1)
    imp_ref[0:SUBLANES, :] = jnp.zeros((SUBLANES, NSA_KV_HEADS), F32)
    imp_ref[SUBLANES:SUBLANES + ncp, :] = imp
    ratio = SEL_BLOCK // CMP_STRIDE
    p_slc = imp_ref[pl.ds(SUBLANES - 1, nsp, stride=ratio), :]
    for d in range(ratio):
        p_slc = p_slc + imp_ref[pl.ds(SUBLANES + d, nsp, stride=ratio), :]
    blk = lax.broadcasted_iota(jnp.int32, (nsp, NSA_KV_HEADS), 0)
    forced = (blk == 0) | (blk > cur - N_LOCAL_SEL)
    score = jnp.where(blk <= cur, p_slc + jnp.where(forced, FORCE_BONUS, 0.0), -FORCE_BONUS)
    low = -2.0 * FORCE_BONUS
    score_pad = jnp.concatenate([score, jnp.full((nsp, LANES - NSA_KV_HEADS), low, F32)], axis=1)
    score_pad = jnp.concatenate([score_pad, jnp.full((nsq - nsp, LANES), low, F32)], axis=0)
    score_rows = score_pad.T
    jp = lax.broadcasted_iota(jnp.int32, (nsp, nsq), 0)
    jj = lax.broadcasted_iota(jnp.int32, (nsp, nsq), 1)
    sel_rows = []
    for g in range(NSA_KV_HEADS):
        col = score[:, g:g + 1]
        row = score_rows[g:g + 1, :]
        beats = (col > row) | ((col == row) & (jp < jj))
        rank = jnp.sum(beats.astype(F32), axis=0, keepdims=True)
        sel_rows.append(((rank < float(n_top)) & (jj[0:1, :] <= cur)).astype(F32))
    row_head = lax.broadcasted_iota(jnp.int32, (rows8, nsq), 0) // NSA_GROUP
    sel8 = jnp.where(row_head == 0, sel_rows[0], sel_rows[1])

    chosen = _dot(sel8[:, 0:nbp].astype(BF16), expand_ref[...])
    n_pages = table_ref.shape[1]
    kt16 = jnp.concatenate([buf_ref[slot, 0, p].astype(BF16) for p in range(n_pages)], axis=1)
    vt16 = jnp.concatenate([buf_ref[slot, 1, p].astype(BF16) for p in range(n_pages)], axis=1)
    s = _dot(q8b, kt16) + jnp.where(chosen > 0.5, 0.0, NEG_INF)
    s_new = jnp.sum(q8 * ksn_ref[0], axis=1, keepdims=True)
    m = jnp.maximum(jnp.max(s, axis=1, keepdims=True), s_new)
    p = jnp.exp2(s - m)
    p_new = jnp.exp2(s_new - m)
    o_sel = ((_nt_dot(p.astype(BF16), vt16) + p_new * vsn_ref[0])
             / (jnp.sum(p, axis=1, keepdims=True) + p_new))

    kwt = kwt_ref[0]
    vwt = vwt_ref[0]
    w = kwt.shape[1]
    lane = lax.broadcasted_iota(jnp.int32, (rows8, w), 1)
    w_pos = past - w + lane
    sw = jnp.where((w_pos > q_pos - WINDOW) & (w_pos >= 0), _dot(q8b, kwt.astype(BF16)), NEG_INF)
    sw_new = jnp.sum(q8 * kwn_ref[0], axis=1, keepdims=True)
    mw = jnp.maximum(jnp.max(sw, axis=1, keepdims=True), sw_new)
    pw = jnp.exp2(sw - mw)
    pw_new = jnp.exp2(sw_new - mw)
    o_win = ((_nt_dot(pw.astype(BF16), vwt.astype(BF16)) + pw_new * vwn_ref[0])
             / (jnp.sum(pw, axis=1, keepdims=True) + pw_new))

    gates = gcol_ref[0]
    o_ref[0] = gates[:, 0:1] * o_cmp + gates[:, 1:2] * o_sel + gates[:, 2:3] * o_win

    last = lax.broadcasted_iota(jnp.int32, (KV_WIDTH, w), 1) == w - 1
    kwo_ref[0] = jnp.where(last, kwc_ref[0], pltpu.roll(kwt, w - 1, axis=1))
    vwo_ref[0] = jnp.where(last, vwc_ref[0], pltpu.roll(vwt, w - 1, axis=1))


def _nsa_sample(table, kcache, vcache, q8, gcol, kc_sum, vc_sum, kwt, vwt, ks_new, vs_new, kw_new, vw_new):
    bs, n_pages = table.shape
    past = n_pages * PAGE_SIZE
    w = kwt.shape[2]
    ncp = kc_sum.shape[1]
    nbp = -(-(past // SEL_BLOCK) // LANES) * LANES
    per_b = lambda a: pl.BlockSpec((1,) + a.shape[1:], lambda b, tbl: (b,) + (0,) * (a.ndim - 1))
    row = lambda a: a.reshape(bs, 1, KV_WIDTH)
    col = lambda a: a.reshape(bs, KV_WIDTH, 1)
    operands = (q8, gcol, kc_sum, vc_sum, kwt, vwt, row(ks_new), row(vs_new), row(kw_new), row(vw_new),
                col(kw_new), col(vw_new))
    grid_spec = pltpu.PrefetchScalarGridSpec(
        num_scalar_prefetch=1, grid=(bs,),
        in_specs=[pl.BlockSpec(memory_space=pl.ANY), pl.BlockSpec(memory_space=pl.ANY)]
        + [per_b(a) for a in operands],
        out_specs=[pl.BlockSpec((1, NSA_HEADS, KV_WIDTH), lambda b, tbl: (b, 0, 0)),
                   pl.BlockSpec((1, KV_WIDTH, w), lambda b, tbl: (b, 0, 0)),
                   pl.BlockSpec((1, KV_WIDTH, w), lambda b, tbl: (b, 0, 0))],
        scratch_shapes=[pltpu.VMEM((2, 2, n_pages, KV_WIDTH, PAGE_SIZE), F32), pltpu.VMEM((nbp, past), BF16),
                        pltpu.VMEM((SUBLANES + ncp, NSA_KV_HEADS), F32), pltpu.SemaphoreType.DMA((2, 2))])
    return pl.pallas_call(
        _nsa_sample_kernel, grid_spec=grid_spec,
        out_shape=[jax.ShapeDtypeStruct((bs, NSA_HEADS, KV_WIDTH), F32),
                   jax.ShapeDtypeStruct((bs, KV_WIDTH, w), F32), jax.ShapeDtypeStruct((bs, KV_WIDTH, w), F32)],
        compiler_params=_params("arbitrary"), name="nsa_sample")(table, kcache, vcache, *operands)


def _hgrn_sample_kernel(h4_ref, s_ref, lb_ref, ng_ref, o_ref, so_ref):
    nb = h4_ref.shape[0]
    hd = HGRN_HEAD_DIM
    hw = HGRN_WIDTH
    hq = h4_ref[:, 0:hw]
    lb = lb_ref[...]
    f = lb + (1.0 - lb) * jax.nn.sigmoid(h4_ref[:, hw:2 * hw])
    k = 1.0 - f
    q = hq * jax.nn.sigmoid(hq)
    hg = h4_ref[:, 3 * hw:4 * hw]
    for h in range(HGRN_HEADS):
        cols = slice(h * hd, (h + 1) * hd)
        stack = jnp.concatenate([f[:, cols], k[:, cols], q[:, cols],
                                 jnp.zeros((LANES - 3 * nb, hd), F32)], axis=0).T
        outs = []
        for bi in range(nb):
            v_row = h4_ref[bi:bi + 1, 2 * hw + h * hd:2 * hw + (h + 1) * hd]
            s_new = stack[:, bi:bi + 1] * s_ref[bi, h] + stack[:, nb + bi:nb + bi + 1] * v_row
            so_ref[bi, h] = s_new
            outs.append(jnp.sum(stack[:, 2 * nb + bi:2 * nb + bi + 1] * s_new, axis=0, keepdims=True))
        o = jnp.concatenate(outs, axis=0)
        o = o * lax.rsqrt(jnp.mean(o * o, axis=-1, keepdims=True) + LN_EPS) * ng_ref[...]
        o_ref[:, cols] = o * (hg[:, cols] * jax.nn.sigmoid(hg[:, cols]))


def _hgrn_sample(h4, s0, lb, norm_g):
    bs = h4.shape[0]
    nb = SUBLANES
    st_spec = pl.BlockSpec((nb, HGRN_HEADS, HGRN_HEAD_DIM, HGRN_HEAD_DIM), lambda i: (i, 0, 0, 0))
    return pl.pallas_call(
        _hgrn_sample_kernel, grid=(bs // nb,),
        in_specs=[pl.BlockSpec((nb, 4 * HGRN_WIDTH), lambda i: (i, 0)), st_spec,
                  pl.BlockSpec((1, HGRN_WIDTH), lambda i: (0, 0)),
                  pl.BlockSpec((1, HGRN_HEAD_DIM), lambda i: (0, 0))],
        out_specs=[pl.BlockSpec((nb, HGRN_WIDTH), lambda i: (i, 0)), st_spec],
        out_shape=[jax.ShapeDtypeStruct((bs, HGRN_WIDTH), F32), jax.ShapeDtypeStruct(s0.shape, F32)],
        compiler_params=_params("parallel"), name="hgrn_sample")(h4, s0, lb, norm_g)


def kernel(x_prompt, x_sample, cache_k_cmp, cache_v_cmp, cache_k_sel, cache_v_sel, cache_k_win, cache_v_win, state_hgrn, page_table, w_in, cmp_pe, cmp_w1, cmp_b1, cmp_w2, cmp_b2, hgrn_gamma, hgrn_norm, w_branch_a, w_branch_b, w_out, ln1_g, ln1_b, w_router, b_router, w_gate_up, b_gate_up, w_down, b_down, ln2_g, ln2_b):
    l = 0
    bp, t, _ = x_prompt.shape
    bs = x_sample.shape[0]
    past = page_table.shape[1] * PAGE_SIZE
    win_keep = cache_k_win.shape[2]

    lower = jnp.cumsum(jax.nn.softmax(hgrn_gamma.astype(F32), axis=0), axis=0)[l][None, :]
    norm_g = hgrn_norm[l][None, :]
    w = w_in[l]
    c_g = NSA_WIDTH + 6 * KV_WIDTH
    n_g = 3 * NSA_HEADS
    w_main = jnp.concatenate([w[:, :c_g], w[:, c_g + n_g:]], axis=1).astype(BF16)
    w_gate_t = jnp.pad(w[:, c_g:c_g + n_g].T, ((0, GATE_ROWS - n_g), (0, 0))).astype(BF16)
    cw = [_compress_weights(cmp_pe[l, n], cmp_w1[l, n], cmp_b1[l, n], cmp_w2[l, n], cmp_b2[l, n]) for n in range(2)]
    wba = w_branch_a[l].astype(BF16)
    wbb = w_branch_b[l].astype(BF16)
    wout = w_out[l].astype(BF16)
    wr_t = w_router[l].T
    wr_hi = wr_t.astype(BF16)
    wr_lo = (wr_t - wr_hi.astype(F32)).astype(BF16)
    br = jnp.broadcast_to(b_router[l].astype(F32)[:, None], (N_EXPERTS, 1))
    g1, b1 = ln1_g[l][None, :], ln1_b[l][None, :]
    g2, b2 = ln2_g[l][None, :], ln2_b[l][None, :]

    def merge_and_dispatch(x2d, o_a, o_b, gab, tm):
        x1, x1b, logit_t = _finish(x2d, o_a, o_b, gab, wba, wbb, wout, g1, b1, wr_hi, wr_lo, br, tm)
        return x1, _moe_dispatch(x1b, logit_t)

    def experts_and_norm(x1, dispatch, tm, after=None):
        return _moe_finish(x1, dispatch, w_gate_up[l], b_gate_up[l], w_down[l], b_down[l], g2, b2, tm, after)

    n = bp * t
    xp = x_prompt.reshape(n, D_MODEL)
    q, kc, vc, ks, vs, kw, vw, h4, gab, gt, *kv_t = _project(xp, w_main, w_gate_t, 256, bp)
    r3 = lambda a: a.reshape(bp, t, a.shape[-1])
    kc_sum = _compress(r3(kc), *cw[0])
    vc_sum = _compress(r3(vc), *cw[1])
    o_nsa = _nsa_prompt(r3(q), gt, kc_sum, vc_sum, r3(ks), r3(vs), r3(kw), r3(vw))
    o_hgrn, p_state = _hgrn_prompt(r3(h4), lower, norm_g, bp, t)
    x1_p, disp_p = merge_and_dispatch(xp, o_nsa.reshape(n, NSA_WIDTH), o_hgrn.reshape(n, HGRN_WIDTH), gab, 256)
    win = min(WINDOW, t)
    t5 = lambda a: jnp.transpose(a.reshape(a.shape[0], NSA_KV_HEADS, NSA_HEAD_DIM, a.shape[2]), (0, 3, 1, 2))[None]
    new_p = (tuple(t5(a) for a in kv_t[:4]) + tuple(t5(a[:, :, t - win:]) for a in kv_t[4:])
             + (p_state[None],))

    page_table, _ = lax.optimization_barrier((page_table, disp_p[-1][:1]))
    xs = x_sample.reshape(bs, D_MODEL)
    q, kc, vc, ks, vs, kw, vw, h4, gab, gt, *_ = _project(xs, w_main, w_gate_t, bs, 1)
    pages = lambda c: jnp.transpose(c[l], (0, 2, 3, 1)).reshape(c.shape[1], KV_WIDTH, PAGE_SIZE)
    band = lambda c: jnp.transpose(c[l], (0, 2, 3, 1)).reshape(bs, KV_WIDTH, win_keep)
    kc_sum, vc_sum = _cmp_pages(page_table, pages(cache_k_cmp), pages(cache_v_cmp), kc, vc, cw[0], cw[1])
    head_eye = jnp.eye(NSA_KV_HEADS, dtype=F32)
    q8 = jnp.einsum('bgrd,gh->bgrhd', q.reshape(bs, NSA_KV_HEADS, NSA_GROUP, NSA_HEAD_DIM),
                    head_eye).reshape(bs, NSA_HEADS, KV_WIDTH)
    gcol = jnp.pad(gt[:3 * NSA_HEADS].T.reshape(bs, 3, NSA_HEADS).transpose(0, 2, 1),
                   ((0, 0), (0, 0), (0, LANES - 3)))
    o8, kw_t, vw_t = _nsa_sample(page_table, pages(cache_k_sel), pages(cache_v_sel), q8, gcol, kc_sum, vc_sum,
                                 band(cache_k_win), band(cache_v_win), ks, vs, kw, vw)
    o_nsa = jnp.einsum('bgrhd,gh->bgrd', o8.reshape(bs, NSA_KV_HEADS, NSA_GROUP, NSA_KV_HEADS, NSA_HEAD_DIM),
                       head_eye).reshape(bs, NSA_WIDTH)
    o_hgrn, s_state = _hgrn_sample(h4, state_hgrn[l], lower, norm_g)
    y_prompt = experts_and_norm(x1_p, disp_p, 256, after=(o8, s_state)).reshape(bp, t, D_MODEL)
    x1_s, disp_s = merge_and_dispatch(xs, o_nsa, o_hgrn, gab, bs)
    y_sample = experts_and_norm(x1_s, disp_s, bs).reshape(bs, 1, D_MODEL)
    s5 = lambda a: a.reshape(1, bs, 1, NSA_KV_HEADS, NSA_HEAD_DIM)
    w5 = lambda a: jnp.transpose(a.reshape(bs, NSA_KV_HEADS, NSA_HEAD_DIM, win_keep), (0, 3, 1, 2))[None]
    new_s = (s5(kc), s5(vc), s5(ks), s5(vs), w5(kw_t), w5(vw_t), s_state[None])

    return (y_prompt, y_sample) + new_p + new_s
```

```python
import functools
import math

import jax
import jax.numpy as jnp
from jax import lax
from jax.experimental import pallas as pl
from jax.experimental.pallas import tpu as pltpu

F32 = jnp.float32
BF16 = jnp.bfloat16

D_MODEL = 1024
PAGE_SIZE = 128
NSA_HEADS = 8
NSA_KV_HEADS = 2
NSA_GROUP = NSA_HEADS // NSA_KV_HEADS
NSA_HEAD_DIM = 64
NSA_WIDTH = NSA_HEADS * NSA_HEAD_DIM
KV_WIDTH = NSA_KV_HEADS * NSA_HEAD_DIM
CMP_BLOCK = 32
CMP_STRIDE = 16
CMP_HIDDEN = 128
SEL_BLOCK = 64
SEL_TOP = 16
N_LOCAL_SEL = 2
FORCE_BONUS = 1.0e4
WINDOW = 512
HGRN_HEADS = 4
HGRN_HEAD_DIM = 128
HGRN_WIDTH = HGRN_HEADS * HGRN_HEAD_DIM
HGRN_CHUNK = 16
N_EXPERTS = 32
TOP_K = 4
D_EXPERT = 1024
SWIGLU_LIMIT = 7.0
SWIGLU_ALPHA = 1.702
DEPTH = 1
DN_ALPHA = (2 * DEPTH) ** 0.25
LN_EPS = 1e-5

LANES = 128
SUBLANES = 8
VMEM_BYTES_V7X = 64 * 1024 * 1024
VMEM_LIMIT = VMEM_BYTES_V7X * 3 // 4

Q_TILE = 128
K_TILE = 128
SEL_SWEEP = 512
HGRN_TILE = 128
MOE_ROWS = 512
ROUTE_TILE = 256
NEG_INF = float("-inf")

_C_Q = 0
_C_KV = _C_Q + NSA_WIDTH
_C_H = _C_KV + 6 * KV_WIDTH
_C_GAB = _C_H + 4 * HGRN_WIDTH
_C_END = _C_GAB + 2 * D_MODEL
GATE_ROWS = 32


def _params(*sem):
    return pltpu.CompilerParams(dimension_semantics=sem, vmem_limit_bytes=VMEM_LIMIT)


def _nt_dot(a, b):
    return lax.dot_general(a, b, (((1,), (1,)), ((), ())), preferred_element_type=F32)


def _dot(a, b):
    return jnp.dot(a, b, preferred_element_type=F32)


def _proj_kernel(x_ref, w_ref, wg_ref, q_ref, kc_ref, vc_ref, ks_ref, vs_ref, kw_ref, vw_ref,
                 h_ref, gab_ref, gt_ref, *kvt_refs):
    x = x_ref[...].astype(BF16)
    q_ref[...] = _dot(x, w_ref[:, _C_Q:_C_KV])
    kv = _dot(x, w_ref[:, _C_KV:_C_H])
    for n, ref in enumerate((kc_ref, vc_ref, ks_ref, vs_ref, kw_ref, vw_ref)):
        ref[...] = kv[:, n * KV_WIDTH:(n + 1) * KV_WIDTH]
    for n, ref in enumerate(kvt_refs):
        for c in range(0, kv.shape[0], LANES):
            ref[0, :, c:c + LANES] = kv[c:c + LANES, n * KV_WIDTH:(n + 1) * KV_WIDTH].T
    h_ref[...] = _dot(x, w_ref[:, _C_H:_C_GAB])
    gab_ref[...] = _dot(x, w_ref[:, _C_GAB:_C_END])
    gt_ref[...] = jax.nn.sigmoid(_nt_dot(wg_ref[...], x))


def _project(x, w_main, w_gate_t, tm, batch):
    n = x.shape[0]
    per_b = n // batch // tm
    row = lambda w: pl.BlockSpec((tm, w), lambda i: (i, 0))
    full = lambda a: pl.BlockSpec(a.shape, lambda i: (0,) * a.ndim)
    out_shape = ([jax.ShapeDtypeStruct((n, NSA_WIDTH), F32)]
                 + [jax.ShapeDtypeStruct((n, KV_WIDTH), F32)] * 6
                 + [jax.ShapeDtypeStruct((n, 4 * HGRN_WIDTH), F32),
                    jax.ShapeDtypeStruct((n, 2 * D_MODEL), F32),
                    jax.ShapeDtypeStruct((GATE_ROWS, n), F32)]
                 + [jax.ShapeDtypeStruct((batch, KV_WIDTH, n // batch), F32)] * 6)
    out_specs = ([row(NSA_WIDTH)] + [row(KV_WIDTH)] * 6 + [row(4 * HGRN_WIDTH), row(2 * D_MODEL),
                 pl.BlockSpec((GATE_ROWS, tm), lambda i: (0, i))]
                 + [pl.BlockSpec((1, KV_WIDTH, tm), lambda i: (i // per_b, 0, i % per_b))] * 6)
    return pl.pallas_call(
        _proj_kernel, grid=(n // tm,),
        in_specs=[row(D_MODEL), full(w_main), full(w_gate_t)],
        out_specs=out_specs, out_shape=out_shape,
        compiler_params=_params("parallel"), name="in_proj")(x, w_main, w_gate_t)


def _gelu_tanh(x):
    return 0.5 * x * (1.0 + jnp.tanh(math.sqrt(2.0 / math.pi) * (x + 0.044715 * (x * x * x))))


def _compress_tail(f, bias_ref, w2_ref, b2_ref):
    j = f.shape[0]
    outs = []
    for g in range(NSA_KV_HEADS):
        base = g * 2 * CMP_HIDDEN
        first = f[:, base:base + CMP_HIDDEN]
        second = f[:, base + CMP_HIDDEN:base + 2 * CMP_HIDDEN]
        nxt = pltpu.roll(second, j - 1, axis=0)
        h = _gelu_tanh(first + nxt + bias_ref[g:g + 1, :])
        outs.append(_dot(h.astype(BF16), w2_ref[...]) + b2_ref[...])
    return jnp.concatenate(outs, axis=1)


def _compress_kernel(c_ref, w1_ref, bias_ref, w2_ref, b2_ref, o_ref):
    c = c_ref[0].astype(BF16)
    o_ref[0] = _compress_tail(_dot(c, w1_ref[...]), bias_ref, w2_ref, b2_ref)


def _compress(rows, w1full, bias, w2, b2):
    b, t, _ = rows.shape
    j = t // CMP_STRIDE
    c = rows.reshape(b, j, CMP_STRIDE * KV_WIDTH)
    full = lambda a: pl.BlockSpec(a.shape, lambda i: (0,) * a.ndim)
    return pl.pallas_call(
        _compress_kernel, grid=(b,),
        in_specs=[pl.BlockSpec((1, j, CMP_STRIDE * KV_WIDTH), lambda i: (i, 0, 0)),
                  full(w1full), full(bias), full(w2), full(b2)],
        out_specs=pl.BlockSpec((1, j, KV_WIDTH), lambda i: (i, 0, 0)),
        out_shape=jax.ShapeDtypeStruct((b, j, KV_WIDTH), F32),
        compiler_params=_params("parallel"), name="compress")(c, w1full, bias, w2, b2)


def _compress_weights(pe, w1, b1, w2, b2):
    hd, hid = NSA_HEAD_DIM, CMP_HIDDEN
    halves = w1.reshape(2, CMP_STRIDE, hd, hid)
    eye = jnp.eye(NSA_KV_HEADS, dtype=w1.dtype)
    w1full = jnp.einsum('apdh,kg->pkdgah', halves, eye).reshape(
        CMP_STRIDE * KV_WIDTH, NSA_KV_HEADS * 2 * hid).astype(BF16)
    bias = jnp.einsum('pd,pdh->h', pe, w1, precision=lax.Precision.HIGHEST) + b1
    bias = jnp.broadcast_to(bias[None, :], (SUBLANES, hid))
    return w1full, bias, w2.astype(BF16), b2.reshape(1, hd)


def _nsa_prompt_kernel(q_ref, gt_ref, kc_ref, vc_ref, ks_ref, vs_ref, kw_ref, vw_ref, o_ref,
                       ks16_ref, vst16_ref, kw16_ref, vwt16_ref, imp_ref, score_ref, selb_ref, seloff_ref,
                       m_ref, l_ref, acc_ref, sc_a_ref, sc_b_ref, mt_a_ref, mt_b_ref):
    g = pl.program_id(1)
    i = pl.program_id(2)
    t0 = pl.multiple_of(i * Q_TILE, Q_TILE)
    rows = NSA_GROUP * Q_TILE
    nc = kc_ref.shape[1]
    ns = score_ref.shape[0]
    t_len = ks_ref.shape[1]
    hd = NSA_HEAD_DIM

    @pl.when((g == 0) & (i == 0))
    def _():
        def cast_step(c, carry):
            r0 = pl.multiple_of(c * K_TILE, K_TILE)
            ks16_ref[pl.ds(r0, K_TILE), :] = ks_ref[0, pl.ds(r0, K_TILE), :].astype(BF16)
            kw16_ref[pl.ds(r0, K_TILE), :] = kw_ref[0, pl.ds(r0, K_TILE), :].astype(BF16)
            vst16_ref[:, pl.ds(r0, K_TILE)] = vs_ref[0, pl.ds(r0, K_TILE), :].T.astype(BF16)
            vwt16_ref[:, pl.ds(r0, K_TILE)] = vw_ref[0, pl.ds(r0, K_TILE), :].T.astype(BF16)
            return carry
        lax.fori_loop(0, t_len // K_TILE, cast_step, 0)

    q = q_ref[0] * (hd ** -0.5 * math.log2(math.e))
    lane_head = lax.broadcasted_iota(jnp.int32, (Q_TILE, KV_WIDTH), 1) // NSA_HEAD_DIM
    parts = []
    for r in range(NSA_GROUP):
        qr = q[:, r * NSA_HEAD_DIM:(r + 1) * NSA_HEAD_DIM]
        parts.append(jnp.where(lane_head == g, jnp.concatenate([qr, qr], axis=1), 0.0))
    qs = jnp.concatenate(parts, axis=0).astype(BF16)

    tok = t0 + lax.broadcasted_iota(jnp.int32, (1, Q_TILE), 1)

    def tile4(bias):
        return jnp.concatenate([bias] * NSA_GROUP, axis=1)

    head_rows = pl.ds(pl.multiple_of(g * hd, hd), hd)

    wk = WINDOW + Q_TILE
    w0 = pl.multiple_of(jnp.maximum(t0 - WINDOW, 0), Q_TILE)
    s = _nt_dot(kc_ref[0].astype(BF16), qs)
    sw = _nt_dot(kw16_ref[pl.ds(w0, wk), :], qs)
    sd = _nt_dot(ks16_ref[pl.ds(t0, Q_TILE), :], qs)

    last_ok = jnp.minimum(jnp.right_shift(tok - (CMP_BLOCK - 1), 4), nc - 2)
    n_idx = lax.broadcasted_iota(jnp.int32, (nc, Q_TILE), 0)
    s = s + tile4(jnp.where(n_idx <= last_ok, 0.0, NEG_INF))
    m = jnp.max(s, axis=0, keepdims=True)
    m = jnp.where(m > NEG_INF, m, 0.0)
    e = jnp.exp2(s - m)
    p = e / jnp.maximum(jnp.sum(e, axis=0, keepdims=True), 1e-30)

    key_w = w0 + lax.broadcasted_iota(jnp.int32, (wk, Q_TILE), 0)
    sw = sw + tile4(jnp.where(lax.bitcast_convert_type(tok - key_w, jnp.uint32) < WINDOW, 0.0, NEG_INF))
    pw = jnp.exp2(sw - jnp.max(sw, axis=0, keepdims=True))

    key_d = t0 + lax.broadcasted_iota(jnp.int32, (Q_TILE, Q_TILE), 0)
    sd = sd + tile4(jnp.where(key_d <= tok, 0.0, NEG_INF))
    md = jnp.max(sd, axis=0, keepdims=True)
    pd = jnp.exp2(sd - md)

    o_cmp = _dot(vc_ref[0].T.astype(BF16), p.astype(BF16))
    o_cmp = jnp.where(g == 0, o_cmp[:hd], o_cmp[hd:])
    def pv_and_sum(vt_ref, k0, size, probs):
        vt = jnp.concatenate([vt_ref[head_rows, pl.ds(k0, size)], jnp.ones((2 * SUBLANES, size), BF16)], axis=0)
        r = _dot(vt, probs.astype(BF16))
        return r[:hd], r[hd:hd + 1]

    pv_w, l_w = pv_and_sum(vwt16_ref, w0, wk, pw)
    o_win = pv_w / l_w
    pv_d, l_d = pv_and_sum(vst16_ref, t0, Q_TILE, pd)
    m_ref[...] = md
    l_ref[...] = l_d
    acc_ref[...] = pv_d

    imp = p[:, 0:Q_TILE]
    for r in range(1, NSA_GROUP):
        imp = imp + p[:, r * Q_TILE:(r + 1) * Q_TILE]
    imp_ref[0:SUBLANES, :] = jnp.zeros((SUBLANES, Q_TILE), F32)
    imp_ref[SUBLANES:SUBLANES + nc, :] = imp
    ratio = SEL_BLOCK // CMP_STRIDE
    p_slc = imp_ref[pl.ds(SUBLANES - 1, ns, stride=ratio), :]
    for d in range(ratio):
        p_slc = p_slc + imp_ref[pl.ds(SUBLANES + d, ns, stride=ratio), :]
    blk = lax.broadcasted_iota(jnp.int32, (ns, Q_TILE), 0)
    cur = tok // SEL_BLOCK
    forced = (blk == 0) | (blk > cur - N_LOCAL_SEL)
    score = jnp.where(blk <= cur, p_slc + jnp.where(forced, FORCE_BONUS, 0.0), -FORCE_BONUS)
    key = lax.bitcast_convert_type(score, jnp.int32)
    score_ref[...] = key

    def rank_step(jp, rank):
        row = score_ref[pl.ds(jp, 1), :]
        return rank + jnp.where(row > jnp.where(jp < blk, key - 1, key), 1.0, 0.0)
    n_top = min(SEL_TOP, ns)
    visible = blk <= cur

    @pl.when(t0 + Q_TILE <= n_top * SEL_BLOCK)
    def _():
        selb_ref[...] = jnp.where(visible, 0.0, NEG_INF)

    @pl.when(t0 + Q_TILE > n_top * SEL_BLOCK)
    def _():
        rank = lax.fori_loop(0, ns, rank_step, jnp.zeros((ns, Q_TILE), F32), unroll=8)
        selb_ref[...] = jnp.where((rank < float(n_top)) & visible, 0.0, NEG_INF)

    first_blk = i * (Q_TILE // SEL_BLOCK)
    seloff_ref[...] = jnp.where(blk < first_blk, selb_ref[...], NEG_INF)
    per_tile = SEL_SWEEP // SEL_BLOCK
    n_sweep = (t0 + SEL_SWEEP - 1) // SEL_SWEEP

    last_tile = t_len // SEL_SWEEP - 1

    def score_tile(kt, s_ref, mx_ref):
        k0 = pl.multiple_of(kt * SEL_SWEEP, SEL_SWEEP)
        bias = jnp.concatenate(
            [jnp.broadcast_to(seloff_ref[pl.ds(kt * per_tile + j, 1), :], (SEL_BLOCK, Q_TILE))
             for j in range(per_tile)], axis=0)
        sc = _nt_dot(ks16_ref[pl.ds(k0, SEL_SWEEP), :], qs) + tile4(bias)
        s_ref[...] = sc
        mx_ref[...] = jnp.max(sc, axis=0, keepdims=True)

    def consume_tile(kt, s_ref, mx_ref):
        k0 = pl.multiple_of(kt * SEL_SWEEP, SEL_SWEEP)
        m_new = jnp.maximum(m_ref[...], mx_ref[...])
        alpha = jnp.exp2(m_ref[...] - m_new)
        pv, l_new = pv_and_sum(vst16_ref, k0, SEL_SWEEP, jnp.exp2(s_ref[...] - m_new))
        l_ref[...] = alpha * l_ref[...] + l_new
        acc_ref[...] = alpha * acc_ref[...] + pv
        m_ref[...] = m_new

    @pl.when(n_sweep > 0)
    def _():
        score_tile(0, sc_a_ref, mt_a_ref)

    def sel_step(kp, carry):
        kt = kp * 2
        score_tile(jnp.minimum(kt + 1, last_tile), sc_b_ref, mt_b_ref)
        consume_tile(kt, sc_a_ref, mt_a_ref)
        score_tile(jnp.minimum(kt + 2, last_tile), sc_a_ref, mt_a_ref)
        consume_tile(jnp.minimum(kt + 1, last_tile), sc_b_ref, mt_b_ref)
        return carry
    lax.fori_loop(0, (n_sweep + 1) // 2, sel_step, 0)
    o_sel = acc_ref[...] / l_ref[...]

    def gate(branch):
        gr = gt_ref[pl.ds(branch * NSA_HEADS + g * NSA_GROUP, NSA_GROUP), :]
        return jnp.concatenate([gr[r:r + 1, :] for r in range(NSA_GROUP)], axis=1)
    o_t = gate(0) * o_cmp + gate(1) * o_sel + gate(2) * o_win
    outs = []
    for r in range(0, NSA_GROUP, 2):
        pair = jnp.concatenate([o_t[:, r * Q_TILE:(r + 1) * Q_TILE],
                                o_t[:, (r + 1) * Q_TILE:(r + 2) * Q_TILE]], axis=0)
        outs.append(pair.T)
    o_ref[0] = jnp.concatenate(outs, axis=1)


def _nsa_prompt(q, gt, kc_sum, vc_sum, ks, vs, kw, vw):
    b, t, _ = q.shape
    assert t % (2 * SEL_SWEEP) == 0 and t >= WINDOW + Q_TILE, t
    nt = t // Q_TILE
    nc = kc_sum.shape[1]
    ns = t // SEL_BLOCK
    rows = NSA_GROUP * Q_TILE
    per_b = lambda a: pl.BlockSpec((1,) + a.shape[1:], lambda bi, g, i: (bi, 0, 0))
    return pl.pallas_call(
        _nsa_prompt_kernel, grid=(b, NSA_KV_HEADS, nt),
        in_specs=[pl.BlockSpec((1, Q_TILE, NSA_WIDTH // NSA_KV_HEADS), lambda bi, g, i: (bi, i, g)),
                  pl.BlockSpec((GATE_ROWS, Q_TILE), lambda bi, g, i: (0, bi * nt + i)),
                  per_b(kc_sum), per_b(vc_sum), per_b(ks), per_b(vs), per_b(kw), per_b(vw)],
        out_specs=pl.BlockSpec((1, Q_TILE, NSA_WIDTH // NSA_KV_HEADS), lambda bi, g, i: (bi, i, g)),
        out_shape=jax.ShapeDtypeStruct((b, t, NSA_WIDTH), F32),
        scratch_shapes=[pltpu.VMEM((t, KV_WIDTH), BF16),
                        pltpu.VMEM((KV_WIDTH, t), BF16),
                        pltpu.VMEM((t, KV_WIDTH), BF16),
                        pltpu.VMEM((KV_WIDTH, t), BF16),
                        pltpu.VMEM((SUBLANES + nc, Q_TILE), F32),
                        pltpu.VMEM((ns, Q_TILE), jnp.int32),
                        pltpu.VMEM((ns, Q_TILE), F32),
                        pltpu.VMEM((ns, Q_TILE), F32),
                        pltpu.VMEM((1, rows), F32),
                        pltpu.VMEM((1, rows), F32),
                        pltpu.VMEM((NSA_HEAD_DIM, rows), F32),
                        pltpu.VMEM((SEL_SWEEP, rows), F32),
                        pltpu.VMEM((SEL_SWEEP, rows), F32),
                        pltpu.VMEM((1, rows), F32),
                        pltpu.VMEM((1, rows), F32)],
        compiler_params=_params("arbitrary", "arbitrary", "arbitrary"),
        name="nsa_prompt")(q, gt, kc_sum, vc_sum, ks, vs, kw, vw)


def _hgrn_prompt_kernel(hq_ref, hf_ref, hi_ref, hg_ref, lb_ref, ng_ref, o_ref, st_out_ref, st_ref):
    c = pl.program_id(1)
    n = HGRN_TILE
    sub = HGRN_CHUNK
    hd = HGRN_HEAD_DIM

    @pl.when(c == 0)
    def _():
        st_ref[...] = jnp.zeros(st_ref.shape, F32)

    pos = lax.broadcasted_iota(jnp.int32, (n, hd), 0) % sub
    rc = lax.broadcasted_iota(jnp.int32, (n, n), 0)
    cc = lax.broadcasted_iota(jnp.int32, (n, n), 1)
    intra = (rc // sub == cc // sub) & (cc <= rc)
    tok_chunk = lax.broadcasted_iota(jnp.int32, (hd, n), 1) // sub

    for h in range(HGRN_HEADS):
        cols = slice(h * hd, (h + 1) * hd)
        lb = lb_ref[:, cols]
        f = lb + (1.0 - lb) * jax.nn.sigmoid(hf_ref[0, :, cols])
        logf = jnp.log(f)
        b = logf
        suf = logf
        sh = 1
        while sh < sub:
            b = b + jnp.where(pos >= sh, pltpu.roll(b, sh, axis=0), 0.0)
            suf = suf + jnp.where(pos + sh < sub, pltpu.roll(suf, n - sh, axis=0), 0.0)
            sh *= 2
        hq = hq_ref[0, :, cols]
        k = 1.0 - f
        q_dec = (hq * jax.nn.sigmoid(hq) * jnp.exp(b)).astype(BF16)
        k_inv = (k * jnp.exp(-b)).astype(BF16)
        k_end = (k * jnp.exp(suf - logf)).astype(BF16)
        v = hi_ref[0, :, cols]

        a = jnp.where(intra, _nt_dot(q_dec, k_inv), 0.0)
        o = _dot(a.astype(BF16), v.astype(BF16))

        vt = v.T
        kv_t = [_dot(jnp.where(tok_chunk == ci, vt, 0.0).astype(BF16), k_end)
                for ci in range(n // sub)]
        states = [st_ref[h]]
        for ci in range(n // sub):
            states.append(states[-1] * jnp.exp(suf[ci * sub:ci * sub + 1, :]) + kv_t[ci])
        st = states[-1]
        st_ref[h] = st
        inter = [_nt_dot(q_dec[ci * sub:(ci + 1) * sub, :], states[ci].astype(BF16))
                 for ci in range(n // sub)]
        o = o + jnp.concatenate(inter, axis=0)
        o = o * lax.rsqrt(jnp.mean(o * o, axis=-1, keepdims=True) + LN_EPS) * ng_ref[...]
        hg = hg_ref[0, :, cols]
        o_ref[0, :, cols] = o * (hg * jax.nn.sigmoid(hg))

    @pl.when(c == pl.num_programs(1) - 1)
    def _():
        for h in range(HGRN_HEADS):
            st_out_ref[0, h] = st_ref[h].T


def _hgrn_prompt(h4, lb, norm_g, b, t):
    nchunk = t // HGRN_TILE
    part = lambda p: pl.BlockSpec((1, HGRN_TILE, HGRN_WIDTH), lambda bi, c: (bi, c, p))
    return pl.pallas_call(
        _hgrn_prompt_kernel, grid=(b, nchunk),
        in_specs=[part(0), part(1), part(2), part(3),
                  pl.BlockSpec((1, HGRN_WIDTH), lambda bi, c: (0, 0)),
                  pl.BlockSpec((1, HGRN_HEAD_DIM), lambda bi, c: (0, 0))],
        out_specs=[pl.BlockSpec((1, HGRN_TILE, HGRN_WIDTH), lambda bi, c: (bi, c, 0)),
                   pl.BlockSpec((1, HGRN_HEADS, HGRN_HEAD_DIM, HGRN_HEAD_DIM), lambda bi, c: (bi, 0, 0, 0))],
        out_shape=[jax.ShapeDtypeStruct((b, t, HGRN_WIDTH), F32),
                   jax.ShapeDtypeStruct((b, HGRN_HEADS, HGRN_HEAD_DIM, HGRN_HEAD_DIM), F32)],
        scratch_shapes=[pltpu.VMEM((HGRN_HEADS, HGRN_HEAD_DIM, HGRN_HEAD_DIM), F32)],
        compiler_params=_params("parallel", "arbitrary"),
        name="hgrn_prompt")(h4, h4, h4, h4, lb, norm_g)


def _layer_norm(y, g, b):
    mu = jnp.mean(y, axis=-1, keepdims=True)
    d = y - mu
    var = jnp.mean(d * d, axis=-1, keepdims=True)
    return d * lax.rsqrt(var + LN_EPS) * g + b


def _finish_kernel(x_ref, oa_ref, ob_ref, gab_ref, wba_ref, wbb_ref, wout_ref, g_ref, b_ref,
                   wr_hi_ref, wr_lo_ref, br_ref, x1_ref, x1b_ref, lt_ref):
    tm = x_ref.shape[0]
    halves = [slice(0, tm // 2), slice(tm // 2, tm)] if tm % (2 * LANES) == 0 else [slice(0, tm)]
    ab = [(_dot(oa_ref[r, :].astype(BF16), wba_ref[...]), _dot(ob_ref[r, :].astype(BF16), wbb_ref[...]))
          for r in halves]
    mixes = [(jax.nn.sigmoid(gab_ref[r, :D_MODEL]) * a + jax.nn.sigmoid(gab_ref[r, D_MODEL:]) * bb).astype(BF16)
             for r, (a, bb) in zip(halves, ab)]
    outs = [_dot(mix, wout_ref[...]) for mix in mixes]
    his, los = [], []
    for r, out in zip(halves, outs):
        x1 = _layer_norm(DN_ALPHA * x_ref[r, :] + out, g_ref[...], b_ref[...])
        x1_ref[r, :] = x1
        hi = x1.astype(BF16)
        x1b_ref[r, :] = hi
        his.append(hi)
        los.append((x1 - hi.astype(F32)).astype(BF16))
    for r, hi, lo in zip(halves, his, los):
        lt_ref[:, r] = (_nt_dot(wr_hi_ref[...], hi) + _nt_dot(wr_hi_ref[...], lo)
                        + _nt_dot(wr_lo_ref[...], hi) + br_ref[...])


def _finish(x, oa, ob, gab, wba, wbb, wout, g1, b1, wr_hi, wr_lo, br, tm):
    n = x.shape[0]
    row = lambda w: pl.BlockSpec((tm, w), lambda i: (i, 0))
    full = lambda a: pl.BlockSpec(a.shape, lambda i: (0,) * a.ndim)
    return pl.pallas_call(
        _finish_kernel, grid=(n // tm,),
        in_specs=[row(D_MODEL), row(NSA_WIDTH), row(HGRN_WIDTH), row(2 * D_MODEL),
                  full(wba), full(wbb), full(wout), full(g1), full(b1), full(wr_hi), full(wr_lo), full(br)],
        out_specs=[row(D_MODEL), row(D_MODEL), pl.BlockSpec((N_EXPERTS, tm), lambda i: (0, i))],
        out_shape=[jax.ShapeDtypeStruct((n, D_MODEL), F32), jax.ShapeDtypeStruct((n, D_MODEL), BF16),
                   jax.ShapeDtypeStruct((N_EXPERTS, n), F32)],
        compiler_params=_params("parallel"), name="merge_ln_router")(
            x, oa, ob, gab, wba, wbb, wout, g1, b1, wr_hi, wr_lo, br)


def _route_kernel(lt_ref, tri_ref, e_ref, w_ref, pos_ref, cnt_ref, carry_ref):
    i = pl.program_id(0)

    @pl.when(i == 0)
    def _():
        carry_ref[...] = jnp.zeros(carry_ref.shape, F32)

    logit = lt_ref[...]
    tn = logit.shape[1]
    eid = lax.broadcasted_iota(jnp.int32, (N_EXPERTS, tn), 0)
    rank = jnp.zeros((N_EXPERTS, tn), F32)
    for ep in range(N_EXPERTS):
        row = logit[ep:ep + 1, :]
        rank = rank + ((row > logit) | ((row == logit) & (ep < eid))).astype(F32)
    sel = rank < float(TOP_K)
    top = jnp.max(logit, axis=0, keepdims=True)
    ex = jnp.where(sel, jnp.exp(logit - top), 0.0)
    wgt = ex / jnp.sum(ex, axis=0, keepdims=True)
    self = sel.astype(F32)
    incl = _dot(self.astype(BF16), tri_ref[...])
    pos = carry_ref[:, 0:1] + incl - self
    carry_ref[...] = carry_ref[...] + jnp.sum(self, axis=1, keepdims=True)
    eid_f = eid.astype(F32)
    for kk in range(TOP_K):
        pick = sel & (rank == float(kk))
        e_ref[kk:kk + 1, :] = jnp.sum(jnp.where(pick, eid_f, 0.0), axis=0, keepdims=True).astype(jnp.int32)
        w_ref[kk:kk + 1, :] = jnp.sum(jnp.where(pick, wgt, 0.0), axis=0, keepdims=True)
        pos_ref[kk:kk + 1, :] = jnp.sum(jnp.where(pick, pos, 0.0), axis=0, keepdims=True).astype(jnp.int32)
    cnt_ref[...] = carry_ref[...]


def _route(logit_t):
    n = logit_t.shape[1]
    tn = ROUTE_TILE if n % ROUTE_TILE == 0 else n
    tri = (lax.broadcasted_iota(jnp.int32, (tn, tn), 0) <= lax.broadcasted_iota(jnp.int32, (tn, tn), 1)).astype(BF16)
    col = lambda r: pl.BlockSpec((r, tn), lambda i: (0, i))
    return pl.pallas_call(
        _route_kernel, grid=(n // tn,),
        in_specs=[col(N_EXPERTS), pl.BlockSpec((tn, tn), lambda i: (0, 0))],
        out_specs=[col(TOP_K), col(TOP_K), col(TOP_K), pl.BlockSpec((N_EXPERTS, LANES), lambda i: (0, 0))],
        out_shape=[jax.ShapeDtypeStruct((TOP_K, n), jnp.int32), jax.ShapeDtypeStruct((TOP_K, n), F32),
                   jax.ShapeDtypeStruct((TOP_K, n), jnp.int32), jax.ShapeDtypeStruct((N_EXPERTS, LANES), F32)],
        scratch_shapes=[pltpu.VMEM((N_EXPERTS, LANES), F32)],
        compiler_params=_params("arbitrary"), name="route")(logit_t, tri)


def _expert_kernel(blk_e_ref, nblk_ref, xb_ref, wgu_ref, bgu_ref, wd_ref, bd_ref, y_ref, wgu_b, wd_b):
    i = pl.program_id(0)
    live = i < nblk_ref[0]

    @pl.when(live & ((i == 0) | (blk_e_ref[i] != blk_e_ref[jnp.maximum(i - 1, 0)])))
    def _():
        wgu_b[...] = wgu_ref[0].astype(BF16)
        wd_b[...] = wd_ref[0].astype(BF16)

    @pl.when(live)
    def _():
        gu = _dot(xb_ref[...], wgu_b[...]) + bgu_ref[0]
        gate = jnp.minimum(gu[:, :D_EXPERT], SWIGLU_LIMIT)
        up = jnp.clip(gu[:, D_EXPERT:], -SWIGLU_LIMIT, SWIGLU_LIMIT)
        h = gate * jax.nn.sigmoid(SWIGLU_ALPHA * gate) * (up + 1.0)
        y_ref[...] = _dot(h.astype(BF16), wd_b[...]) + bd_ref[0]

    @pl.when(i >= nblk_ref[0])
    def _():
        y_ref[...] = jnp.zeros(y_ref.shape, F32)


def _experts(xb, blk_e, nblk, w_gu, b_gu, w_down, b_down):
    m = xb.shape[0]
    grid_spec = pltpu.PrefetchScalarGridSpec(
        num_scalar_prefetch=2, grid=(m // MOE_ROWS,),
        in_specs=[pl.BlockSpec((MOE_ROWS, D_MODEL), lambda i, be, nb: (i, 0)),
                  pl.BlockSpec((1, D_MODEL, 2 * D_EXPERT), lambda i, be, nb: (be[i], 0, 0)),
                  pl.BlockSpec((1, 1, 2 * D_EXPERT), lambda i, be, nb: (be[i], 0, 0)),
                  pl.BlockSpec((1, D_EXPERT, D_MODEL), lambda i, be, nb: (be[i], 0, 0)),
                  pl.BlockSpec((1, 1, D_MODEL), lambda i, be, nb: (be[i], 0, 0))],
        out_specs=pl.BlockSpec((MOE_ROWS, D_MODEL), lambda i, be, nb: (i, 0)),
        scratch_shapes=[pltpu.VMEM((D_MODEL, 2 * D_EXPERT), BF16), pltpu.VMEM((D_EXPERT, D_MODEL), BF16)])
    return pl.pallas_call(
        _expert_kernel, grid_spec=grid_spec,
        out_shape=jax.ShapeDtypeStruct((m, D_MODEL), F32),
        compiler_params=_params("arbitrary"), name="experts")(
            blk_e, nblk, xb, w_gu, b_gu.reshape(N_EXPERTS, 1, 2 * D_EXPERT),
            w_down, b_down.reshape(N_EXPERTS, 1, D_MODEL))


def _final_ln_kernel(x1_ref, w_ref, y0_ref, y1_ref, y2_ref, y3_ref, g_ref, b_ref, o_ref):
    w = w_ref[...]
    y = y0_ref[...] * w[:, 0:1]
    for kk, y_ref in enumerate((y1_ref, y2_ref, y3_ref), start=1):
        y = y + y_ref[...] * w[:, kk:kk + 1]
    o_ref[...] = _layer_norm(DN_ALPHA * x1_ref[...] + y, g_ref[...], b_ref[...])


def _final_ln(x1, w_tok, ys, g2, b2, tm):
    n = x1.shape[0]
    row = pl.BlockSpec((tm, D_MODEL), lambda i: (i, 0))
    vec = pl.BlockSpec((1, D_MODEL), lambda i: (0, 0))
    return pl.pallas_call(
        _final_ln_kernel, grid=(n // tm,),
        in_specs=[row, pl.BlockSpec((tm, TOP_K), lambda i: (i, 0))] + [row] * TOP_K + [vec, vec],
        out_specs=row, out_shape=jax.ShapeDtypeStruct((n, D_MODEL), F32),
        compiler_params=_params("parallel"), name="final_ln")(x1, w_tok, *ys, g2, b2)


def _moe_dispatch(x1b, logit_t):
    n = x1b.shape[0]
    top_e, top_w, top_pos, counts = _route(logit_t)
    counts = counts[:, 0].astype(jnp.int32)
    padded = (counts + MOE_ROWS - 1) // MOE_ROWS * MOE_ROWS
    pad_end = jnp.cumsum(padded)
    pad_start = pad_end - padded
    n_blocks = -(-(n * TOP_K + N_EXPERTS * (MOE_ROWS - 1)) // MOE_ROWS)
    experts = jnp.arange(N_EXPERTS, dtype=jnp.int32)
    start_of = jnp.sum(jnp.where(top_e[..., None] == experts, pad_start, 0), axis=-1)
    dest = start_of + top_pos
    tok = jnp.broadcast_to(jnp.arange(n, dtype=jnp.int32)[None, :], dest.shape)
    slot_tok = jnp.zeros((n_blocks * MOE_ROWS,), jnp.int32).at[dest.reshape(-1)].set(tok.reshape(-1))
    xb = x1b[slot_tok]
    blk_row0 = jnp.arange(n_blocks, dtype=jnp.int32) * MOE_ROWS
    blk_e = jnp.minimum(jnp.sum((pad_end[None, :] <= blk_row0[:, None]).astype(jnp.int32), axis=1),
                        N_EXPERTS - 1)
    nblk = (pad_end[-1:] // MOE_ROWS).astype(jnp.int32)
    return xb, blk_e, nblk, dest, top_w, slot_tok


def _moe_finish(x1, dispatch, w_gu, b_gu, w_down, b_down, g2, b2, tm, after=None):
    xb, blk_e, nblk, dest, top_w, _ = dispatch
    if after is not None:
        nblk, _ = lax.optimization_barrier((nblk, after))
    yb = _experts(xb, blk_e, nblk, w_gu, b_gu, w_down, b_down)
    return _final_ln(x1, top_w.T, [yb[dest[kk]] for kk in range(TOP_K)], g2, b2, tm)


def _page_copy(cache_hbm, table_ref, bi, p, buf_ref, slot, c, sem_ref):
    return pltpu.make_async_copy(cache_hbm.at[table_ref[bi, p]], buf_ref.at[slot, c, p], sem_ref.at[slot, c])


def _fetch_pages(caches, table_ref, buf_ref, sem_ref):
    b = pl.program_id(0)
    n_pages = table_ref.shape[1]
    slot = b % 2

    def start(bi, sl):
        def body(p, carry):
            for c, cache in enumerate(caches):
                _page_copy(cache, table_ref, bi, p, buf_ref, sl, c, sem_ref).start()
            return carry
        lax.fori_loop(0, n_pages, body, 0)

    @pl.when(b == 0)
    def _():
        start(0, 0)

    @pl.when(b + 1 < pl.num_programs(0))
    def _():
        start(b + 1, 1 - slot)

    def wait(p, carry):
        for c, cache in enumerate(caches):
            _page_copy(cache, table_ref, b, p, buf_ref, slot, c, sem_ref).wait()
        return carry
    lax.fori_loop(0, n_pages, wait, 0)
    return slot


def _cmp_pages_kernel(table_ref, kcache, vcache, knew_ref, vnew_ref,
                      w1k_ref, bk_ref, w2k_ref, b2k_ref, w1v_ref, bv_ref, w2v_ref, b2v_ref,
                      ko_ref, vo_ref, buf_ref, rows_ref, sem_ref):
    slot = _fetch_pages((kcache, vcache), table_ref, buf_ref, sem_ref)
    n_pages = table_ref.shape[1]
    past = n_pages * PAGE_SIZE
    n_rows = rows_ref.shape[0]
    j = n_rows // CMP_STRIDE
    first_row = lax.broadcasted_iota(jnp.int32, (n_rows - past, KV_WIDTH), 0) == 0
    plan = ((knew_ref, w1k_ref, bk_ref, w2k_ref, b2k_ref, ko_ref),
            (vnew_ref, w1v_ref, bv_ref, w2v_ref, b2v_ref, vo_ref))
    for c, (new_ref, w1_ref, bias_ref, w2_ref, b2_ref, o_ref) in enumerate(plan):
        def to_rows(p, carry):
            off = pl.multiple_of(p * PAGE_SIZE, PAGE_SIZE)
            rows_ref[pl.ds(off, PAGE_SIZE), :] = buf_ref[slot, c, p].T
            return carry
        lax.fori_loop(0, n_pages, to_rows, 0, unroll=8)
        rows_ref[past:, :] = jnp.where(first_row, new_ref[0], 0.0)
        f = None
        for p in range(0, CMP_STRIDE, 2):
            x = jnp.concatenate([rows_ref[pl.ds(p, j, stride=CMP_STRIDE), :],
                                 rows_ref[pl.ds(p + 1, j, stride=CMP_STRIDE), :]], axis=1).astype(BF16)
            part = _dot(x, w1_ref[p * KV_WIDTH:(p + 2) * KV_WIDTH, :])
            f = part if f is None else f + part
        o_ref[0, 0:j, :] = _compress_tail(f, bias_ref, w2_ref, b2_ref)
        o_ref[0, j:, :] = jnp.zeros((o_ref.shape[1] - j, KV_WIDTH), F32)


def _cmp_pages(table, kcache, vcache, knew, vnew, cw_k, cw_v):
    bs, n_pages = table.shape
    past = n_pages * PAGE_SIZE
    n_rows = past + SUBLANES * CMP_STRIDE
    j = n_rows // CMP_STRIDE
    jp = -(-j // LANES) * LANES
    full = lambda a: pl.BlockSpec(a.shape, lambda b, tbl: (0,) * a.ndim)
    new = pl.BlockSpec((1, 1, KV_WIDTH), lambda b, tbl: (b, 0, 0))
    out = pl.BlockSpec((1, jp, KV_WIDTH), lambda b, tbl: (b, 0, 0))
    grid_spec = pltpu.PrefetchScalarGridSpec(
        num_scalar_prefetch=1, grid=(bs,),
        in_specs=[pl.BlockSpec(memory_space=pl.ANY), pl.BlockSpec(memory_space=pl.ANY), new, new]
        + [full(a) for a in cw_k] + [full(a) for a in cw_v],
        out_specs=[out, out],
        scratch_shapes=[pltpu.VMEM((2, 2, n_pages, KV_WIDTH, PAGE_SIZE), F32),
                        pltpu.VMEM((n_rows, KV_WIDTH), F32), pltpu.SemaphoreType.DMA((2, 2))])
    return pl.pallas_call(
        _cmp_pages_kernel, grid_spec=grid_spec,
        out_shape=[jax.ShapeDtypeStruct((bs, jp, KV_WIDTH), F32)] * 2,
        compiler_params=_params("arbitrary"), name="cmp_pages")(
            table, kcache, vcache, knew.reshape(bs, 1, KV_WIDTH), vnew.reshape(bs, 1, KV_WIDTH), *cw_k, *cw_v)


def _nsa_sample_kernel(table_ref, kcache, vcache, q8_ref, gcol_ref, kc_ref, vc_ref, kwt_ref, vwt_ref,
                       ksn_ref, vsn_ref, kwn_ref, vwn_ref, kwc_ref, vwc_ref,
                       o_ref, kwo_ref, vwo_ref, buf_ref, expand_ref, imp_ref, sem_ref):
    b = pl.program_id(0)
    slot = _fetch_pages((kcache, vcache), table_ref, buf_ref, sem_ref)
    past = table_ref.shape[1] * PAGE_SIZE
    q_pos = past
    sel_len = -(-(past + 1) // SEL_BLOCK) * SEL_BLOCK
    nc = sel_len // CMP_STRIDE - 1
    ns = sel_len // SEL_BLOCK
    ncp = kc_ref.shape[1]
    nsp = -(-ns // SUBLANES) * SUBLANES
    nsq = -(-nsp // LANES) * LANES
    nbp = expand_ref.shape[0]
    rows8 = NSA_HEADS
    cur = q_pos // SEL_BLOCK
    n_top = min(SEL_TOP, ns)

    @pl.when(b == 0)
    def _():
        blk_i = lax.broadcasted_iota(jnp.int32, (nbp, past), 0)
        key_i = lax.broadcasted_iota(jnp.int32, (nbp, past), 1)
        expand_ref[...] = (blk_i == key_i // SEL_BLOCK).astype(BF16)

    q8 = q8_ref[0] * (NSA_HEAD_DIM ** -0.5 * math.log2(math.e))
    q8b = q8.astype(BF16)

    st = _nt_dot(kc_ref[0].astype(BF16), q8b)
    n_idx = lax.broadcasted_iota(jnp.int32, (ncp, rows8), 0)
    st = jnp.where((n_idx * CMP_STRIDE + (CMP_BLOCK - 1) <= q_pos) & (n_idx < nc), st, NEG_INF)
    mc = jnp.max(st, axis=0, keepdims=True)
    mc = jnp.where(mc > NEG_INF, mc, 0.0)
    ec = jnp.exp2(st - mc)
    pt = ec / jnp.maximum(jnp.sum(ec, axis=0, keepdims=True), 1e-30)
    o_cmp_t = _dot(vc_ref[0].T.astype(BF16), pt.astype(BF16))
    o_cmp = jnp.concatenate([o_cmp_t, jnp.zeros((KV_WIDTH, LANES - rows8), F32)], axis=1).T[0:rows8, :]

    imp = jnp.concatenate([jnp.sum(pt[:, g * NSA_GROUP:(g + 1) * NSA_GROUP], axis=1, keepdims=True)
                           for g in range(NSA_KV_HEADS)], axis=1)
    imp_ref[0:SUBLANES, :] = jnp.zeros((SUBLANES, NSA_KV_HEADS), F32)
    imp_ref[SUBLANES:SUBLANES + ncp, :] = imp
    ratio = SEL_BLOCK // CMP_STRIDE
    p_slc = imp_ref[pl.ds(SUBLANES - 1, nsp, stride=ratio), :]
    for d in range(ratio):
        p_slc = p_slc + imp_ref[pl.ds(SUBLANES + d, nsp, stride=ratio), :]
    blk = lax.broadcasted_iota(jnp.int32, (nsp, NSA_KV_HEADS), 0)
    forced = (blk == 0) | (blk > cur - N_LOCAL_SEL)
    score = jnp.where(blk <= cur, p_slc + jnp.where(forced, FORCE_BONUS, 0.0), -FORCE_BONUS)
    low = -2.0 * FORCE_BONUS
    score_pad = jnp.concatenate([score, jnp.full((nsp, LANES - NSA_KV_HEADS), low, F32)], axis=1)
    score_pad = jnp.concatenate([score_pad, jnp.full((nsq - nsp, LANES), low, F32)], axis=0)
    score_rows = score_pad.T
    jp = lax.broadcasted_iota(jnp.int32, (nsp, nsq), 0)
    jj = lax.broadcasted_iota(jnp.int32, (nsp, nsq), 1)
    sel_rows = []
    for g in range(NSA_KV_HEADS):
        col = score[:, g:g + 1]
        row = score_rows[g:g + 1, :]
        beats = (col > row) | ((col == row) & (jp < jj))
        rank = jnp.sum(beats.astype(F32), axis=0, keepdims=True)
        sel_rows.append(((rank < float(n_top)) & (jj[0:1, :] <= cur)).astype(F32))
    row_head = lax.broadcasted_iota(jnp.int32, (rows8, nsq), 0) // NSA_GROUP
    sel8 = jnp.where(row_head == 0, sel_rows[0], sel_rows[1])

    chosen = _dot(sel8[:, 0:nbp].astype(BF16), expand_ref[...])
    n_pages = table_ref.shape[1]
    kt16 = jnp.concatenate([buf_ref[slot, 0, p].astype(BF16) for p in range(n_pages)], axis=1)
    vt16 = jnp.concatenate([buf_ref[slot, 1, p].astype(BF16) for p in range(n_pages)], axis=1)
    s = _dot(q8b, kt16) + jnp.where(chosen > 0.5, 0.0, NEG_INF)
    s_new = jnp.sum(q8 * ksn_ref[0], axis=1, keepdims=True)
    m = jnp.maximum(jnp.max(s, axis=1, keepdims=True), s_new)
    p = jnp.exp2(s - m)
    p_new = jnp.exp2(s_new - m)
    o_sel = ((_nt_dot(p.astype(BF16), vt16) + p_new * vsn_ref[0])
             / (jnp.sum(p, axis=1, keepdims=True) + p_new))

    kwt = kwt_ref[0]
    vwt = vwt_ref[0]
    w = kwt.shape[1]
    lane = lax.broadcasted_iota(jnp.int32, (rows8, w), 1)
    w_pos = past - w + lane
    sw = jnp.where((w_pos > q_pos - WINDOW) & (w_pos >= 0), _dot(q8b, kwt.astype(BF16)), NEG_INF)
    sw_new = jnp.sum(q8 * kwn_ref[0], axis=1, keepdims=True)
    mw = jnp.maximum(jnp.max(sw, axis=1, keepdims=True), sw_new)
    pw = jnp.exp2(sw - mw)
    pw_new = jnp.exp2(sw_new - mw)
    o_win = ((_nt_dot(pw.astype(BF16), vwt.astype(BF16)) + pw_new * vwn_ref[0])
             / (jnp.sum(pw, axis=1, keepdims=True) + pw_new))

    gates = gcol_ref[0]
    o_ref[0] = gates[:, 0:1] * o_cmp + gates[:, 1:2] * o_sel + gates[:, 2:3] * o_win

    last = lax.broadcasted_iota(jnp.int32, (KV_WIDTH, w), 1) == w - 1
    kwo_ref[0] = jnp.where(last, kwc_ref[0], pltpu.roll(kwt, w - 1, axis=1))
    vwo_ref[0] = jnp.where(last, vwc_ref[0], pltpu.roll(vwt, w - 1, axis=1))


def _nsa_sample(table, kcache, vcache, q8, gcol, kc_sum, vc_sum, kwt, vwt, ks_new, vs_new, kw_new, vw_new):
    bs, n_pages = table.shape
    past = n_pages * PAGE_SIZE
    w = kwt.shape[2]
    ncp = kc_sum.shape[1]
    nbp = -(-(past // SEL_BLOCK) // LANES) * LANES
    per_b = lambda a: pl.BlockSpec((1,) + a.shape[1:], lambda b, tbl: (b,) + (0,) * (a.ndim - 1))
    row = lambda a: a.reshape(bs, 1, KV_WIDTH)
    col = lambda a: a.reshape(bs, KV_WIDTH, 1)
    operands = (q8, gcol, kc_sum, vc_sum, kwt, vwt, row(ks_new), row(vs_new), row(kw_new), row(vw_new),
                col(kw_new), col(vw_new))
    grid_spec = pltpu.PrefetchScalarGridSpec(
        num_scalar_prefetch=1, grid=(bs,),
        in_specs=[pl.BlockSpec(memory_space=pl.ANY), pl.BlockSpec(memory_space=pl.ANY)]
        + [per_b(a) for a in operands],
        out_specs=[pl.BlockSpec((1, NSA_HEADS, KV_WIDTH), lambda b, tbl: (b, 0, 0)),
                   pl.BlockSpec((1, KV_WIDTH, w), lambda b, tbl: (b, 0, 0)),
                   pl.BlockSpec((1, KV_WIDTH, w), lambda b, tbl: (b, 0, 0))],
        scratch_shapes=[pltpu.VMEM((2, 2, n_pages, KV_WIDTH, PAGE_SIZE), F32), pltpu.VMEM((nbp, past), BF16),
                        pltpu.VMEM((SUBLANES + ncp, NSA_KV_HEADS), F32), pltpu.SemaphoreType.DMA((2, 2))])
    return pl.pallas_call(
        _nsa_sample_kernel, grid_spec=grid_spec,
        out_shape=[jax.ShapeDtypeStruct((bs, NSA_HEADS, KV_WIDTH), F32),
                   jax.ShapeDtypeStruct((bs, KV_WIDTH, w), F32), jax.ShapeDtypeStruct((bs, KV_WIDTH, w), F32)],
        compiler_params=_params("arbitrary"), name="nsa_sample")(table, kcache, vcache, *operands)


def _hgrn_sample_kernel(h4_ref, s_ref, lb_ref, ng_ref, o_ref, so_ref):
    nb = h4_ref.shape[0]
    hd = HGRN_HEAD_DIM
    hw = HGRN_WIDTH
    hq = h4_ref[:, 0:hw]
    lb = lb_ref[...]
    f = lb + (1.0 - lb) * jax.nn.sigmoid(h4_ref[:, hw:2 * hw])
    k = 1.0 - f
    q = hq * jax.nn.sigmoid(hq)
    hg = h4_ref[:, 3 * hw:4 * hw]
    for h in range(HGRN_HEADS):
        cols = slice(h * hd, (h + 1) * hd)
        stack = jnp.concatenate([f[:, cols], k[:, cols], q[:, cols],
                                 jnp.zeros((LANES - 3 * nb, hd), F32)], axis=0).T
        outs = []
        for bi in range(nb):
            v_row = h4_ref[bi:bi + 1, 2 * hw + h * hd:2 * hw + (h + 1) * hd]
            s_new = stack[:, bi:bi + 1] * s_ref[bi, h] + stack[:, nb + bi:nb + bi + 1] * v_row
            so_ref[bi, h] = s_new
            outs.append(jnp.sum(stack[:, 2 * nb + bi:2 * nb + bi + 1] * s_new, axis=0, keepdims=True))
        o = jnp.concatenate(outs, axis=0)
        o = o * lax.rsqrt(jnp.mean(o * o, axis=-1, keepdims=True) + LN_EPS) * ng_ref[...]
        o_ref[:, cols] = o * (hg[:, cols] * jax.nn.sigmoid(hg[:, cols]))


def _hgrn_sample(h4, s0, lb, norm_g):
    bs = h4.shape[0]
    nb = SUBLANES
    assert bs % nb == 0, bs
    st_spec = pl.BlockSpec((nb, HGRN_HEADS, HGRN_HEAD_DIM, HGRN_HEAD_DIM), lambda i: (i, 0, 0, 0))
    return pl.pallas_call(
        _hgrn_sample_kernel, grid=(bs // nb,),
        in_specs=[pl.BlockSpec((nb, 4 * HGRN_WIDTH), lambda i: (i, 0)), st_spec,
                  pl.BlockSpec((1, HGRN_WIDTH), lambda i: (0, 0)),
                  pl.BlockSpec((1, HGRN_HEAD_DIM), lambda i: (0, 0))],
        out_specs=[pl.BlockSpec((nb, HGRN_WIDTH), lambda i: (i, 0)), st_spec],
        out_shape=[jax.ShapeDtypeStruct((bs, HGRN_WIDTH), F32), jax.ShapeDtypeStruct(s0.shape, F32)],
        compiler_params=_params("parallel"), name="hgrn_sample")(h4, s0, lb, norm_g)


def kernel(x_prompt, x_sample, cache_k_cmp, cache_v_cmp, cache_k_sel, cache_v_sel, cache_k_win, cache_v_win, state_hgrn, page_table, w_in, cmp_pe, cmp_w1, cmp_b1, cmp_w2, cmp_b2, hgrn_gamma, hgrn_norm, w_branch_a, w_branch_b, w_out, ln1_g, ln1_b, w_router, b_router, w_gate_up, b_gate_up, w_down, b_down, ln2_g, ln2_b):
    l = 0
    bp, t, _ = x_prompt.shape
    bs = x_sample.shape[0]
    past = page_table.shape[1] * PAGE_SIZE
    win_keep = cache_k_win.shape[2]

    lower = jnp.cumsum(jax.nn.softmax(hgrn_gamma.astype(F32), axis=0), axis=0)[l][None, :]
    norm_g = hgrn_norm[l][None, :]
    w = w_in[l]
    c_g = NSA_WIDTH + 6 * KV_WIDTH
    n_g = 3 * NSA_HEADS
    w_main = jnp.concatenate([w[:, :c_g], w[:, c_g + n_g:]], axis=1).astype(BF16)
    w_gate_t = jnp.pad(w[:, c_g:c_g + n_g].T, ((0, GATE_ROWS - n_g), (0, 0))).astype(BF16)
    cw = [_compress_weights(cmp_pe[l, n], cmp_w1[l, n], cmp_b1[l, n], cmp_w2[l, n], cmp_b2[l, n]) for n in range(2)]
    wba = w_branch_a[l].astype(BF16)
    wbb = w_branch_b[l].astype(BF16)
    wout = w_out[l].astype(BF16)
    wr_t = w_router[l].T
    wr_hi = wr_t.astype(BF16)
    wr_lo = (wr_t - wr_hi.astype(F32)).astype(BF16)
    br = jnp.broadcast_to(b_router[l].astype(F32)[:, None], (N_EXPERTS, 1))
    g1, b1 = ln1_g[l][None, :], ln1_b[l][None, :]
    g2, b2 = ln2_g[l][None, :], ln2_b[l][None, :]

    def merge_and_dispatch(x2d, o_a, o_b, gab, tm):
        x1, x1b, logit_t = _finish(x2d, o_a, o_b, gab, wba, wbb, wout, g1, b1, wr_hi, wr_lo, br, tm)
        return x1, _moe_dispatch(x1b, logit_t)

    def experts_and_norm(x1, dispatch, tm, after=None):
        return _moe_finish(x1, dispatch, w_gate_up[l], b_gate_up[l], w_down[l], b_down[l], g2, b2, tm, after)

    n = bp * t
    xp = x_prompt.reshape(n, D_MODEL)
    q, kc, vc, ks, vs, kw, vw, h4, gab, gt, *kv_t = _project(xp, w_main, w_gate_t, 256, bp)
    r3 = lambda a: a.reshape(bp, t, a.shape[-1])
    kc_sum = _compress(r3(kc), *cw[0])
    vc_sum = _compress(r3(vc), *cw[1])
    o_nsa = _nsa_prompt(r3(q), gt, kc_sum, vc_sum, r3(ks), r3(vs), r3(kw), r3(vw))
    o_hgrn, p_state = _hgrn_prompt(r3(h4), lower, norm_g, bp, t)
    x1_p, disp_p = merge_and_dispatch(xp, o_nsa.reshape(n, NSA_WIDTH), o_hgrn.reshape(n, HGRN_WIDTH), gab, 256)
    win = min(WINDOW, t)
    t5 = lambda a: jnp.transpose(a.reshape(a.shape[0], NSA_KV_HEADS, NSA_HEAD_DIM, a.shape[2]), (0, 3, 1, 2))[None]
    new_p = (tuple(t5(a) for a in kv_t[:4]) + tuple(t5(a[:, :, t - win:]) for a in kv_t[4:])
             + (p_state[None],))

    page_table, _ = lax.optimization_barrier((page_table, disp_p[-1][:1]))
    xs = x_sample.reshape(bs, D_MODEL)
    q, kc, vc, ks, vs, kw, vw, h4, gab, gt, *_ = _project(xs, w_main, w_gate_t, bs, 1)
    pages = lambda c: jnp.transpose(c[l], (0, 2, 3, 1)).reshape(c.shape[1], KV_WIDTH, PAGE_SIZE)
    band = lambda c: jnp.transpose(c[l], (0, 2, 3, 1)).reshape(bs, KV_WIDTH, win_keep)
    kc_sum, vc_sum = _cmp_pages(page_table, pages(cache_k_cmp), pages(cache_v_cmp), kc, vc, cw[0], cw[1])
    head_eye = jnp.eye(NSA_KV_HEADS, dtype=F32)
    q8 = jnp.einsum('bgrd,gh->bgrhd', q.reshape(bs, NSA_KV_HEADS, NSA_GROUP, NSA_HEAD_DIM),
                    head_eye).reshape(bs, NSA_HEADS, KV_WIDTH)
    gcol = jnp.pad(gt[:3 * NSA_HEADS].T.reshape(bs, 3, NSA_HEADS).transpose(0, 2, 1),
                   ((0, 0), (0, 0), (0, LANES - 3)))
    o8, kw_t, vw_t = _nsa_sample(page_table, pages(cache_k_sel), pages(cache_v_sel), q8, gcol, kc_sum, vc_sum,
                                 band(cache_k_win), band(cache_v_win), ks, vs, kw, vw)
    o_nsa = jnp.einsum('bgrhd,gh->bgrd', o8.reshape(bs, NSA_KV_HEADS, NSA_GROUP, NSA_KV_HEADS, NSA_HEAD_DIM),
                       head_eye).reshape(bs, NSA_WIDTH)
    o_hgrn, s_state = _hgrn_sample(h4, state_hgrn[l], lower, norm_g)
    y_prompt = experts_and_norm(x1_p, disp_p, 256, after=(o8, s_state)).reshape(bp, t, D_MODEL)
    x1_s, disp_s = merge_and_dispatch(xs, o_nsa, o_hgrn, gab, bs)
    y_sample = experts_and_norm(x1_s, disp_s, bs).reshape(bs, 1, D_MODEL)
    s5 = lambda a: a.reshape(1, bs, 1, NSA_KV_HEADS, NSA_HEAD_DIM)
    w5 = lambda a: jnp.transpose(a.reshape(bs, NSA_KV_HEADS, NSA_HEAD_DIM, win_keep), (0, 3, 1, 2))[None]
    new_s = (s5(kc), s5(vc), s5(ks), s5(vs), w5(kw_t), w5(vw_t), s_state[None])

    return (y_prompt, y_sample) + new_p + new_s
```

```python
import functools
import math

import jax
import jax.numpy as jnp
from jax import lax
from jax.experimental import pallas as pl
from jax.experimental.pallas import tpu as pltpu

F32 = jnp.float32
BF16 = jnp.bfloat16

D_MODEL = 1024
PAGE_SIZE = 128
NSA_HEADS = 8
NSA_KV_HEADS = 2
NSA_GROUP = NSA_HEADS // NSA_KV_HEADS
NSA_HEAD_DIM = 64
NSA_WIDTH = NSA_HEADS * NSA_HEAD_DIM
KV_WIDTH = NSA_KV_HEADS * NSA_HEAD_DIM
CMP_BLOCK = 32
CMP_STRIDE = 16
CMP_HIDDEN = 128
SEL_BLOCK = 64
SEL_TOP = 16
N_LOCAL_SEL = 2
FORCE_BONUS = 1.0e4
WINDOW = 512
HGRN_HEADS = 4
HGRN_HEAD_DIM = 128
HGRN_WIDTH = HGRN_HEADS * HGRN_HEAD_DIM
HGRN_CHUNK = 16
N_EXPERTS = 32
TOP_K = 4
D_EXPERT = 1024
SWIGLU_LIMIT = 7.0
SWIGLU_ALPHA = 1.702
DEPTH = 1
DN_ALPHA = (2 * DEPTH) ** 0.25
LN_EPS = 1e-5

LANES = 128
SUBLANES = 8
VMEM_BYTES_V7X = 64 * 1024 * 1024
VMEM_LIMIT = VMEM_BYTES_V7X * 3 // 4

Q_TILE = 128
K_TILE = 128
SEL_SWEEP = 512
HGRN_TILE = 128
MOE_ROWS = 512
ROUTE_TILE = 256
NEG_INF = float("-inf")

_C_Q = 0
_C_KV = _C_Q + NSA_WIDTH
_C_H = _C_KV + 6 * KV_WIDTH
_C_GAB = _C_H + 4 * HGRN_WIDTH
_C_END = _C_GAB + 2 * D_MODEL
GATE_ROWS = 32


def _params(*sem):
    return pltpu.CompilerParams(dimension_semantics=sem, vmem_limit_bytes=VMEM_LIMIT)


def _nt_dot(a, b):
    return lax.dot_general(a, b, (((1,), (1,)), ((), ())), preferred_element_type=F32)


def _dot(a, b):
    return jnp.dot(a, b, preferred_element_type=F32)


def _proj_kernel(x_ref, w_ref, wg_ref, q_ref, kc_ref, vc_ref, ks_ref, vs_ref, kw_ref, vw_ref,
                 h_ref, gab_ref, gt_ref, *kvt_refs):
    x = x_ref[...].astype(BF16)
    q_ref[...] = _dot(x, w_ref[:, _C_Q:_C_KV])
    kv = _dot(x, w_ref[:, _C_KV:_C_H])
    for n, ref in enumerate((kc_ref, vc_ref, ks_ref, vs_ref, kw_ref, vw_ref)):
        ref[...] = kv[:, n * KV_WIDTH:(n + 1) * KV_WIDTH]
    for n, ref in enumerate(kvt_refs):
        for c in range(0, kv.shape[0], LANES):
            ref[0, :, c:c + LANES] = kv[c:c + LANES, n * KV_WIDTH:(n + 1) * KV_WIDTH].T
    h_ref[...] = _dot(x, w_ref[:, _C_H:_C_GAB])
    gab_ref[...] = _dot(x, w_ref[:, _C_GAB:_C_END])
    gt_ref[...] = jax.nn.sigmoid(_nt_dot(wg_ref[...], x))


def _project(x, w_main, w_gate_t, tm, batch):
    n = x.shape[0]
    per_b = n // batch // tm
    row = lambda w: pl.BlockSpec((tm, w), lambda i: (i, 0))
    full = lambda a: pl.BlockSpec(a.shape, lambda i: (0,) * a.ndim)
    out_shape = ([jax.ShapeDtypeStruct((n, NSA_WIDTH), F32)]
                 + [jax.ShapeDtypeStruct((n, KV_WIDTH), F32)] * 6
                 + [jax.ShapeDtypeStruct((n, 4 * HGRN_WIDTH), F32),
                    jax.ShapeDtypeStruct((n, 2 * D_MODEL), F32),
                    jax.ShapeDtypeStruct((GATE_ROWS, n), F32)]
                 + [jax.ShapeDtypeStruct((batch, KV_WIDTH, n // batch), F32)] * 6)
    out_specs = ([row(NSA_WIDTH)] + [row(KV_WIDTH)] * 6 + [row(4 * HGRN_WIDTH), row(2 * D_MODEL),
                 pl.BlockSpec((GATE_ROWS, tm), lambda i: (0, i))]
                 + [pl.BlockSpec((1, KV_WIDTH, tm), lambda i: (i // per_b, 0, i % per_b))] * 6)
    return pl.pallas_call(
        _proj_kernel, grid=(n // tm,),
        in_specs=[row(D_MODEL), full(w_main), full(w_gate_t)],
        out_specs=out_specs, out_shape=out_shape,
        compiler_params=_params("parallel"), name="in_proj")(x, w_main, w_gate_t)


def _gelu_tanh(x):
    return 0.5 * x * (1.0 + jnp.tanh(math.sqrt(2.0 / math.pi) * (x + 0.044715 * (x * x * x))))


def _compress_tail(f, bias_ref, w2_ref, b2_ref):
    j = f.shape[0]
    outs = []
    for g in range(NSA_KV_HEADS):
        base = g * 2 * CMP_HIDDEN
        first = f[:, base:base + CMP_HIDDEN]
        second = f[:, base + CMP_HIDDEN:base + 2 * CMP_HIDDEN]
        nxt = pltpu.roll(second, j - 1, axis=0)
        h = _gelu_tanh(first + nxt + bias_ref[g:g + 1, :])
        outs.append(_dot(h.astype(BF16), w2_ref[...]) + b2_ref[...])
    return jnp.concatenate(outs, axis=1)


def _compress_kernel(c_ref, w1_ref, bias_ref, w2_ref, b2_ref, o_ref):
    c = c_ref[0].astype(BF16)
    o_ref[0] = _compress_tail(_dot(c, w1_ref[...]), bias_ref, w2_ref, b2_ref)


def _compress(rows, w1full, bias, w2, b2):
    b, t, _ = rows.shape
    j = t // CMP_STRIDE
    c = rows.reshape(b, j, CMP_STRIDE * KV_WIDTH)
    full = lambda a: pl.BlockSpec(a.shape, lambda i: (0,) * a.ndim)
    return pl.pallas_call(
        _compress_kernel, grid=(b,),
        in_specs=[pl.BlockSpec((1, j, CMP_STRIDE * KV_WIDTH), lambda i: (i, 0, 0)),
                  full(w1full), full(bias), full(w2), full(b2)],
        out_specs=pl.BlockSpec((1, j, KV_WIDTH), lambda i: (i, 0, 0)),
        out_shape=jax.ShapeDtypeStruct((b, j, KV_WIDTH), F32),
        compiler_params=_params("parallel"), name="compress")(c, w1full, bias, w2, b2)


def _compress_weights(pe, w1, b1, w2, b2):
    hd, hid = NSA_HEAD_DIM, CMP_HIDDEN
    halves = w1.reshape(2, CMP_STRIDE, hd, hid)
    eye = jnp.eye(NSA_KV_HEADS, dtype=w1.dtype)
    w1full = jnp.einsum('apdh,kg->pkdgah', halves, eye).reshape(
        CMP_STRIDE * KV_WIDTH, NSA_KV_HEADS * 2 * hid).astype(BF16)
    bias = jnp.einsum('pd,pdh->h', pe, w1, precision=lax.Precision.HIGHEST) + b1
    bias = jnp.broadcast_to(bias[None, :], (SUBLANES, hid))
    return w1full, bias, w2.astype(BF16), b2.reshape(1, hd)


def _nsa_prompt_kernel(q_ref, gt_ref, kc_ref, vc_ref, ks_ref, vs_ref, kw_ref, vw_ref, o_ref,
                       ks16_ref, vst16_ref, kw16_ref, vwt16_ref, imp_ref, score_ref, selb_ref, seloff_ref,
                       m_ref, l_ref, acc_ref, sc_a_ref, sc_b_ref, mt_a_ref, mt_b_ref):
    g = pl.program_id(1)
    i = pl.program_id(2)
    t0 = pl.multiple_of(i * Q_TILE, Q_TILE)
    rows = NSA_GROUP * Q_TILE
    nc = kc_ref.shape[1]
    ns = score_ref.shape[0]
    t_len = ks_ref.shape[1]
    hd = NSA_HEAD_DIM

    @pl.when((g == 0) & (i == 0))
    def _():
        def cast_step(c, carry):
            r0 = pl.multiple_of(c * K_TILE, K_TILE)
            ks16_ref[pl.ds(r0, K_TILE), :] = ks_ref[0, pl.ds(r0, K_TILE), :].astype(BF16)
            kw16_ref[pl.ds(r0, K_TILE), :] = kw_ref[0, pl.ds(r0, K_TILE), :].astype(BF16)
            vst16_ref[:, pl.ds(r0, K_TILE)] = vs_ref[0, pl.ds(r0, K_TILE), :].T.astype(BF16)
            vwt16_ref[:, pl.ds(r0, K_TILE)] = vw_ref[0, pl.ds(r0, K_TILE), :].T.astype(BF16)
            return carry
        lax.fori_loop(0, t_len // K_TILE, cast_step, 0)

    q = q_ref[0] * (hd ** -0.5 * math.log2(math.e))
    lane_head = lax.broadcasted_iota(jnp.int32, (Q_TILE, KV_WIDTH), 1) // NSA_HEAD_DIM
    parts = []
    for r in range(NSA_GROUP):
        qr = q[:, r * NSA_HEAD_DIM:(r + 1) * NSA_HEAD_DIM]
        parts.append(jnp.where(lane_head == g, jnp.concatenate([qr, qr], axis=1), 0.0))
    qs = jnp.concatenate(parts, axis=0).astype(BF16)

    tok = t0 + lax.broadcasted_iota(jnp.int32, (1, Q_TILE), 1)

    def tile4(bias):
        return jnp.concatenate([bias] * NSA_GROUP, axis=1)

    head_rows = pl.ds(pl.multiple_of(g * hd, hd), hd)

    wk = WINDOW + Q_TILE
    w0 = pl.multiple_of(jnp.maximum(t0 - WINDOW, 0), Q_TILE)
    s = _nt_dot(kc_ref[0].astype(BF16), qs)
    sw = _nt_dot(kw16_ref[pl.ds(w0, wk), :], qs)
    sd = _nt_dot(ks16_ref[pl.ds(t0, Q_TILE), :], qs)
    sc_a_ref[...] = _nt_dot(ks16_ref[0:SEL_SWEEP, :], qs)

    last_ok = jnp.minimum(jnp.right_shift(tok - (CMP_BLOCK - 1), 4), nc - 2)
    n_idx = lax.broadcasted_iota(jnp.int32, (nc, Q_TILE), 0)
    s = s + tile4(jnp.where(n_idx <= last_ok, 0.0, NEG_INF))
    m = jnp.max(s, axis=0, keepdims=True)
    m = jnp.where(m > NEG_INF, m, 0.0)
    e = jnp.exp2(s - m)
    p = e / jnp.maximum(jnp.sum(e, axis=0, keepdims=True), 1e-30)

    key_w = w0 + lax.broadcasted_iota(jnp.int32, (wk, Q_TILE), 0)
    sw = sw + tile4(jnp.where(lax.bitcast_convert_type(tok - key_w, jnp.uint32) < WINDOW, 0.0, NEG_INF))
    pw = jnp.exp2(sw - jnp.max(sw, axis=0, keepdims=True))

    key_d = t0 + lax.broadcasted_iota(jnp.int32, (Q_TILE, Q_TILE), 0)
    sd = sd + tile4(jnp.where(key_d <= tok, 0.0, NEG_INF))
    md = jnp.max(sd, axis=0, keepdims=True)
    pd = jnp.exp2(sd - md)

    o_cmp = _dot(vc_ref[0].T.astype(BF16), p.astype(BF16))
    o_cmp = jnp.where(g == 0, o_cmp[:hd], o_cmp[hd:])
    def pv_and_sum(vt_ref, k0, size, probs):
        vt = jnp.concatenate([vt_ref[head_rows, pl.ds(k0, size)], jnp.ones((2 * SUBLANES, size), BF16)], axis=0)
        r = _dot(vt, probs.astype(BF16))
        return r[:hd], r[hd:hd + 1]

    pv_w, l_w = pv_and_sum(vwt16_ref, w0, wk, pw)
    o_win = pv_w / l_w
    pv_d, l_d = pv_and_sum(vst16_ref, t0, Q_TILE, pd)
    m_ref[...] = md
    l_ref[...] = l_d
    acc_ref[...] = pv_d

    imp = p[:, 0:Q_TILE]
    for r in range(1, NSA_GROUP):
        imp = imp + p[:, r * Q_TILE:(r + 1) * Q_TILE]
    imp_ref[0:SUBLANES, :] = jnp.zeros((SUBLANES, Q_TILE), F32)
    imp_ref[SUBLANES:SUBLANES + nc, :] = imp
    ratio = SEL_BLOCK // CMP_STRIDE
    p_slc = imp_ref[pl.ds(SUBLANES - 1, ns, stride=ratio), :]
    for d in range(ratio):
        p_slc = p_slc + imp_ref[pl.ds(SUBLANES + d, ns, stride=ratio), :]
    blk = lax.broadcasted_iota(jnp.int32, (ns, Q_TILE), 0)
    cur = tok // SEL_BLOCK
    forced = (blk == 0) | (blk > cur - N_LOCAL_SEL)
    score = jnp.where(blk <= cur, p_slc + jnp.where(forced, FORCE_BONUS, 0.0), -FORCE_BONUS)
    key = lax.bitcast_convert_type(score, jnp.int32)
    score_ref[...] = key

    def rank_step(jp, rank):
        row = score_ref[pl.ds(jp, 1), :]
        return rank + jnp.where(row > jnp.where(jp < blk, key - 1, key), 1.0, 0.0)
    n_top = min(SEL_TOP, ns)
    visible = blk <= cur

    @pl.when(t0 + Q_TILE <= n_top * SEL_BLOCK)
    def _():
        selb_ref[...] = jnp.where(visible, 0.0, NEG_INF)

    @pl.when(t0 + Q_TILE > n_top * SEL_BLOCK)
    def _():
        rank = lax.fori_loop(0, ns, rank_step, jnp.zeros((ns, Q_TILE), F32), unroll=8)
        selb_ref[...] = jnp.where((rank < float(n_top)) & visible, 0.0, NEG_INF)

    first_blk = i * (Q_TILE // SEL_BLOCK)
    seloff_ref[...] = jnp.where(blk < first_blk, selb_ref[...], NEG_INF)
    per_tile = SEL_SWEEP // SEL_BLOCK
    n_sweep = (t0 + SEL_SWEEP - 1) // SEL_SWEEP

    last_tile = t_len // SEL_SWEEP - 1

    def mask_tile(kt, raw, s_ref, mx_ref):
        bias = jnp.concatenate(
            [jnp.broadcast_to(seloff_ref[pl.ds(kt * per_tile + j, 1), :], (SEL_BLOCK, Q_TILE))
             for j in range(per_tile)], axis=0)
        sc = raw + tile4(bias)
        s_ref[...] = sc
        mx_ref[...] = jnp.max(sc, axis=0, keepdims=True)

    def score_tile(kt, s_ref, mx_ref):
        k0 = pl.multiple_of(kt * SEL_SWEEP, SEL_SWEEP)
        mask_tile(kt, _nt_dot(ks16_ref[pl.ds(k0, SEL_SWEEP), :], qs), s_ref, mx_ref)

    def consume_tile(kt, s_ref, mx_ref):
        k0 = pl.multiple_of(kt * SEL_SWEEP, SEL_SWEEP)
        m_new = jnp.maximum(m_ref[...], mx_ref[...])
        alpha = jnp.exp2(m_ref[...] - m_new)
        pv, l_new = pv_and_sum(vst16_ref, k0, SEL_SWEEP, jnp.exp2(s_ref[...] - m_new))
        l_ref[...] = alpha * l_ref[...] + l_new
        acc_ref[...] = alpha * acc_ref[...] + pv
        m_ref[...] = m_new

    @pl.when(n_sweep > 0)
    def _():
        mask_tile(0, sc_a_ref[...], sc_a_ref, mt_a_ref)

    def sel_step(kp, carry):
        kt = kp * 2
        score_tile(jnp.minimum(kt + 1, last_tile), sc_b_ref, mt_b_ref)
        consume_tile(kt, sc_a_ref, mt_a_ref)
        score_tile(jnp.minimum(kt + 2, last_tile), sc_a_ref, mt_a_ref)
        consume_tile(jnp.minimum(kt + 1, last_tile), sc_b_ref, mt_b_ref)
        return carry
    lax.fori_loop(0, (n_sweep + 1) // 2, sel_step, 0)
    o_sel = acc_ref[...] / l_ref[...]

    def gate(branch):
        gr = gt_ref[pl.ds(branch * NSA_HEADS + g * NSA_GROUP, NSA_GROUP), :]
        return jnp.concatenate([gr[r:r + 1, :] for r in range(NSA_GROUP)], axis=1)
    o_t = gate(0) * o_cmp + gate(1) * o_sel + gate(2) * o_win
    outs = []
    for r in range(0, NSA_GROUP, 2):
        pair = jnp.concatenate([o_t[:, r * Q_TILE:(r + 1) * Q_TILE],
                                o_t[:, (r + 1) * Q_TILE:(r + 2) * Q_TILE]], axis=0)
        outs.append(pair.T)
    o_ref[0] = jnp.concatenate(outs, axis=1)


def _nsa_prompt(q, gt, kc_sum, vc_sum, ks, vs, kw, vw):
    b, t, _ = q.shape
    assert t % (2 * SEL_SWEEP) == 0 and t >= WINDOW + Q_TILE, t
    nt = t // Q_TILE
    nc = kc_sum.shape[1]
    ns = t // SEL_BLOCK
    rows = NSA_GROUP * Q_TILE
    per_b = lambda a: pl.BlockSpec((1,) + a.shape[1:], lambda bi, g, i: (bi, 0, 0))
    return pl.pallas_call(
        _nsa_prompt_kernel, grid=(b, NSA_KV_HEADS, nt),
        in_specs=[pl.BlockSpec((1, Q_TILE, NSA_WIDTH // NSA_KV_HEADS), lambda bi, g, i: (bi, i, g)),
                  pl.BlockSpec((GATE_ROWS, Q_TILE), lambda bi, g, i: (0, bi * nt + i)),
                  per_b(kc_sum), per_b(vc_sum), per_b(ks), per_b(vs), per_b(kw), per_b(vw)],
        out_specs=pl.BlockSpec((1, Q_TILE, NSA_WIDTH // NSA_KV_HEADS), lambda bi, g, i: (bi, i, g)),
        out_shape=jax.ShapeDtypeStruct((b, t, NSA_WIDTH), F32),
        scratch_shapes=[pltpu.VMEM((t, KV_WIDTH), BF16),
                        pltpu.VMEM((KV_WIDTH, t), BF16),
                        pltpu.VMEM((t, KV_WIDTH), BF16),
                        pltpu.VMEM((KV_WIDTH, t), BF16),
                        pltpu.VMEM((SUBLANES + nc, Q_TILE), F32),
                        pltpu.VMEM((ns, Q_TILE), jnp.int32),
                        pltpu.VMEM((ns, Q_TILE), F32),
                        pltpu.VMEM((ns, Q_TILE), F32),
                        pltpu.VMEM((1, rows), F32),
                        pltpu.VMEM((1, rows), F32),
                        pltpu.VMEM((NSA_HEAD_DIM, rows), F32),
                        pltpu.VMEM((SEL_SWEEP, rows), F32),
                        pltpu.VMEM((SEL_SWEEP, rows), F32),
                        pltpu.VMEM((1, rows), F32),
                        pltpu.VMEM((1, rows), F32)],
        compiler_params=_params("arbitrary", "arbitrary", "arbitrary"),
        name="nsa_prompt")(q, gt, kc_sum, vc_sum, ks, vs, kw, vw)


def _hgrn_prompt_kernel(hq_ref, hf_ref, hi_ref, hg_ref, lb_ref, ng_ref, o_ref, st_out_ref, st_ref):
    c = pl.program_id(1)
    n = HGRN_TILE
    sub = HGRN_CHUNK
    hd = HGRN_HEAD_DIM

    @pl.when(c == 0)
    def _():
        st_ref[...] = jnp.zeros(st_ref.shape, F32)

    pos = lax.broadcasted_iota(jnp.int32, (n, hd), 0) % sub
    rc = lax.broadcasted_iota(jnp.int32, (n, n), 0)
    cc = lax.broadcasted_iota(jnp.int32, (n, n), 1)
    intra = (rc // sub == cc // sub) & (cc <= rc)
    tok_chunk = lax.broadcasted_iota(jnp.int32, (hd, n), 1) // sub

    for h in range(HGRN_HEADS):
        cols = slice(h * hd, (h + 1) * hd)
        lb = lb_ref[:, cols]
        f = lb + (1.0 - lb) * jax.nn.sigmoid(hf_ref[0, :, cols])
        logf = jnp.log(f)
        b = logf
        suf = logf
        sh = 1
        while sh < sub:
            b = b + jnp.where(pos >= sh, pltpu.roll(b, sh, axis=0), 0.0)
            suf = suf + jnp.where(pos + sh < sub, pltpu.roll(suf, n - sh, axis=0), 0.0)
            sh *= 2
        hq = hq_ref[0, :, cols]
        k = 1.0 - f
        q_dec = (hq * jax.nn.sigmoid(hq) * jnp.exp(b)).astype(BF16)
        k_inv = (k * jnp.exp(-b)).astype(BF16)
        k_end = (k * jnp.exp(suf - logf)).astype(BF16)
        v = hi_ref[0, :, cols]

        a = jnp.where(intra, _nt_dot(q_dec, k_inv), 0.0)
        o = _dot(a.astype(BF16), v.astype(BF16))

        vt = v.T
        kv_t = [_dot(jnp.where(tok_chunk == ci, vt, 0.0).astype(BF16), k_end)
                for ci in range(n // sub)]
        states = [st_ref[h]]
        for ci in range(n // sub):
            states.append(states[-1] * jnp.exp(suf[ci * sub:ci * sub + 1, :]) + kv_t[ci])
        st = states[-1]
        st_ref[h] = st
        inter = [_nt_dot(q_dec[ci * sub:(ci + 1) * sub, :], states[ci].astype(BF16))
                 for ci in range(n // sub)]
        o = o + jnp.concatenate(inter, axis=0)
        o = o * lax.rsqrt(jnp.mean(o * o, axis=-1, keepdims=True) + LN_EPS) * ng_ref[...]
        hg = hg_ref[0, :, cols]
        o_ref[0, :, cols] = o * (hg * jax.nn.sigmoid(hg))

    @pl.when(c == pl.num_programs(1) - 1)
    def _():
        for h in range(HGRN_HEADS):
            st_out_ref[0, h] = st_ref[h].T


def _hgrn_prompt(h4, lb, norm_g, b, t):
    nchunk = t // HGRN_TILE
    part = lambda p: pl.BlockSpec((1, HGRN_TILE, HGRN_WIDTH), lambda bi, c: (bi, c, p))
    return pl.pallas_call(
        _hgrn_prompt_kernel, grid=(b, nchunk),
        in_specs=[part(0), part(1), part(2), part(3),
                  pl.BlockSpec((1, HGRN_WIDTH), lambda bi, c: (0, 0)),
                  pl.BlockSpec((1, HGRN_HEAD_DIM), lambda bi, c: (0, 0))],
        out_specs=[pl.BlockSpec((1, HGRN_TILE, HGRN_WIDTH), lambda bi, c: (bi, c, 0)),
                   pl.BlockSpec((1, HGRN_HEADS, HGRN_HEAD_DIM, HGRN_HEAD_DIM), lambda bi, c: (bi, 0, 0, 0))],
        out_shape=[jax.ShapeDtypeStruct((b, t, HGRN_WIDTH), F32),
                   jax.ShapeDtypeStruct((b, HGRN_HEADS, HGRN_HEAD_DIM, HGRN_HEAD_DIM), F32)],
        scratch_shapes=[pltpu.VMEM((HGRN_HEADS, HGRN_HEAD_DIM, HGRN_HEAD_DIM), F32)],
        compiler_params=_params("parallel", "arbitrary"),
        name="hgrn_prompt")(h4, h4, h4, h4, lb, norm_g)


def _layer_norm(y, g, b):
    mu = jnp.mean(y, axis=-1, keepdims=True)
    d = y - mu
    var = jnp.mean(d * d, axis=-1, keepdims=True)
    return d * lax.rsqrt(var + LN_EPS) * g + b


def _finish_kernel(x_ref, oa_ref, ob_ref, gab_ref, wba_ref, wbb_ref, wout_ref, g_ref, b_ref,
                   wr_hi_ref, wr_lo_ref, br_ref, x1_ref, x1b_ref, lt_ref):
    tm = x_ref.shape[0]
    halves = [slice(0, tm // 2), slice(tm // 2, tm)] if tm % (2 * LANES) == 0 else [slice(0, tm)]
    ab = [(_dot(oa_ref[r, :].astype(BF16), wba_ref[...]), _dot(ob_ref[r, :].astype(BF16), wbb_ref[...]))
          for r in halves]
    mixes = [(jax.nn.sigmoid(gab_ref[r, :D_MODEL]) * a + jax.nn.sigmoid(gab_ref[r, D_MODEL:]) * bb).astype(BF16)
             for r, (a, bb) in zip(halves, ab)]
    outs = [_dot(mix, wout_ref[...]) for mix in mixes]
    his, los = [], []
    for r, out in zip(halves, outs):
        x1 = _layer_norm(DN_ALPHA * x_ref[r, :] + out, g_ref[...], b_ref[...])
        x1_ref[r, :] = x1
        hi = x1.astype(BF16)
        x1b_ref[r, :] = hi
        his.append(hi)
        los.append((x1 - hi.astype(F32)).astype(BF16))
    for r, hi, lo in zip(halves, his, los):
        lt_ref[:, r] = (_nt_dot(wr_hi_ref[...], hi) + _nt_dot(wr_hi_ref[...], lo)
                        + _nt_dot(wr_lo_ref[...], hi) + br_ref[...])


def _finish(x, oa, ob, gab, wba, wbb, wout, g1, b1, wr_hi, wr_lo, br, tm):
    n = x.shape[0]
    row = lambda w: pl.BlockSpec((tm, w), lambda i: (i, 0))
    full = lambda a: pl.BlockSpec(a.shape, lambda i: (0,) * a.ndim)
    return pl.pallas_call(
        _finish_kernel, grid=(n // tm,),
        in_specs=[row(D_MODEL), row(NSA_WIDTH), row(HGRN_WIDTH), row(2 * D_MODEL),
                  full(wba), full(wbb), full(wout), full(g1), full(b1), full(wr_hi), full(wr_lo), full(br)],
        out_specs=[row(D_MODEL), row(D_MODEL), pl.BlockSpec((N_EXPERTS, tm), lambda i: (0, i))],
        out_shape=[jax.ShapeDtypeStruct((n, D_MODEL), F32), jax.ShapeDtypeStruct((n, D_MODEL), BF16),
                   jax.ShapeDtypeStruct((N_EXPERTS, n), F32)],
        compiler_params=_params("parallel"), name="merge_ln_router")(
            x, oa, ob, gab, wba, wbb, wout, g1, b1, wr_hi, wr_lo, br)


def _route_kernel(lt_ref, tri_ref, e_ref, w_ref, pos_ref, cnt_ref, carry_ref):
    i = pl.program_id(0)

    @pl.when(i == 0)
    def _():
        carry_ref[...] = jnp.zeros(carry_ref.shape, F32)

    logit = lt_ref[...]
    tn = logit.shape[1]
    eid = lax.broadcasted_iota(jnp.int32, (N_EXPERTS, tn), 0)
    rank = jnp.zeros((N_EXPERTS, tn), F32)
    for ep in range(N_EXPERTS):
        row = logit[ep:ep + 1, :]
        rank = rank + ((row > logit) | ((row == logit) & (ep < eid))).astype(F32)
    sel = rank < float(TOP_K)
    top = jnp.max(logit, axis=0, keepdims=True)
    ex = jnp.where(sel, jnp.exp(logit - top), 0.0)
    wgt = ex / jnp.sum(ex, axis=0, keepdims=True)
    self = sel.astype(F32)
    incl = _dot(self.astype(BF16), tri_ref[...])
    pos = carry_ref[:, 0:1] + incl - self
    carry_ref[...] = carry_ref[...] + jnp.sum(self, axis=1, keepdims=True)
    eid_f = eid.astype(F32)
    for kk in range(TOP_K):
        pick = sel & (rank == float(kk))
        e_ref[kk:kk + 1, :] = jnp.sum(jnp.where(pick, eid_f, 0.0), axis=0, keepdims=True).astype(jnp.int32)
        w_ref[kk:kk + 1, :] = jnp.sum(jnp.where(pick, wgt, 0.0), axis=0, keepdims=True)
        pos_ref[kk:kk + 1, :] = jnp.sum(jnp.where(pick, pos, 0.0), axis=0, keepdims=True).astype(jnp.int32)
    cnt_ref[...] = carry_ref[...]


def _route(logit_t):
    n = logit_t.shape[1]
    tn = ROUTE_TILE if n % ROUTE_TILE == 0 else n
    tri = (lax.broadcasted_iota(jnp.int32, (tn, tn), 0) <= lax.broadcasted_iota(jnp.int32, (tn, tn), 1)).astype(BF16)
    col = lambda r: pl.BlockSpec((r, tn), lambda i: (0, i))
    return pl.pallas_call(
        _route_kernel, grid=(n // tn,),
        in_specs=[col(N_EXPERTS), pl.BlockSpec((tn, tn), lambda i: (0, 0))],
        out_specs=[col(TOP_K), col(TOP_K), col(TOP_K), pl.BlockSpec((N_EXPERTS, LANES), lambda i: (0, 0))],
        out_shape=[jax.ShapeDtypeStruct((TOP_K, n), jnp.int32), jax.ShapeDtypeStruct((TOP_K, n), F32),
                   jax.ShapeDtypeStruct((TOP_K, n), jnp.int32), jax.ShapeDtypeStruct((N_EXPERTS, LANES), F32)],
        scratch_shapes=[pltpu.VMEM((N_EXPERTS, LANES), F32)],
        compiler_params=_params("arbitrary"), name="route")(logit_t, tri)


def _expert_kernel(blk_e_ref, nblk_ref, xb_ref, wgu_ref, bgu_ref, wd_ref, bd_ref, y_ref, wgu_b, wd_b):
    i = pl.program_id(0)
    live = i < nblk_ref[0]

    @pl.when(live & ((i == 0) | (blk_e_ref[i] != blk_e_ref[jnp.maximum(i - 1, 0)])))
    def _():
        wgu_b[...] = wgu_ref[0].astype(BF16)
        wd_b[...] = wd_ref[0].astype(BF16)

    @pl.when(live)
    def _():
        gu = _dot(xb_ref[...], wgu_b[...]) + bgu_ref[0]
        gate = jnp.minimum(gu[:, :D_EXPERT], SWIGLU_LIMIT)
        up = jnp.clip(gu[:, D_EXPERT:], -SWIGLU_LIMIT, SWIGLU_LIMIT)
        h = gate * jax.nn.sigmoid(SWIGLU_ALPHA * gate) * (up + 1.0)
        y_ref[...] = _dot(h.astype(BF16), wd_b[...]) + bd_ref[0]

    @pl.when(i >= nblk_ref[0])
    def _():
        y_ref[...] = jnp.zeros(y_ref.shape, F32)


def _experts(xb, blk_e, nblk, w_gu, b_gu, w_down, b_down):
    m = xb.shape[0]
    grid_spec = pltpu.PrefetchScalarGridSpec(
        num_scalar_prefetch=2, grid=(m // MOE_ROWS,),
        in_specs=[pl.BlockSpec((MOE_ROWS, D_MODEL), lambda i, be, nb: (i, 0)),
                  pl.BlockSpec((1, D_MODEL, 2 * D_EXPERT), lambda i, be, nb: (be[i], 0, 0)),
                  pl.BlockSpec((1, 1, 2 * D_EXPERT), lambda i, be, nb: (be[i], 0, 0)),
                  pl.BlockSpec((1, D_EXPERT, D_MODEL), lambda i, be, nb: (be[i], 0, 0)),
                  pl.BlockSpec((1, 1, D_MODEL), lambda i, be, nb: (be[i], 0, 0))],
        out_specs=pl.BlockSpec((MOE_ROWS, D_MODEL), lambda i, be, nb: (i, 0)),
        scratch_shapes=[pltpu.VMEM((D_MODEL, 2 * D_EXPERT), BF16), pltpu.VMEM((D_EXPERT, D_MODEL), BF16)])
    return pl.pallas_call(
        _expert_kernel, grid_spec=grid_spec,
        out_shape=jax.ShapeDtypeStruct((m, D_MODEL), F32),
        compiler_params=_params("arbitrary"), name="experts")(
            blk_e, nblk, xb, w_gu, b_gu.reshape(N_EXPERTS, 1, 2 * D_EXPERT),
            w_down, b_down.reshape(N_EXPERTS, 1, D_MODEL))


def _final_ln_kernel(x1_ref, w_ref, y0_ref, y1_ref, y2_ref, y3_ref, g_ref, b_ref, o_ref):
    w = w_ref[...]
    y = y0_ref[...] * w[:, 0:1]
    for kk, y_ref in enumerate((y1_ref, y2_ref, y3_ref), start=1):
        y = y + y_ref[...] * w[:, kk:kk + 1]
    o_ref[...] = _layer_norm(DN_ALPHA * x1_ref[...] + y, g_ref[...], b_ref[...])


def _final_ln(x1, w_tok, ys, g2, b2, tm):
    n = x1.shape[0]
    row = pl.BlockSpec((tm, D_MODEL), lambda i: (i, 0))
    vec = pl.BlockSpec((1, D_MODEL), lambda i: (0, 0))
    return pl.pallas_call(
        _final_ln_kernel, grid=(n // tm,),
        in_specs=[row, pl.BlockSpec((tm, TOP_K), lambda i: (i, 0))] + [row] * TOP_K + [vec, vec],
        out_specs=row, out_shape=jax.ShapeDtypeStruct((n, D_MODEL), F32),
        compiler_params=_params("parallel"), name="final_ln")(x1, w_tok, *ys, g2, b2)


def _moe_dispatch(x1b, logit_t):
    n = x1b.shape[0]
    top_e, top_w, top_pos, counts = _route(logit_t)
    counts = counts[:, 0].astype(jnp.int32)
    padded = (counts + MOE_ROWS - 1) // MOE_ROWS * MOE_ROWS
    pad_end = jnp.cumsum(padded)
    pad_start = pad_end - padded
    n_blocks = -(-(n * TOP_K + N_EXPERTS * (MOE_ROWS - 1)) // MOE_ROWS)
    experts = jnp.arange(N_EXPERTS, dtype=jnp.int32)
    start_of = jnp.sum(jnp.where(top_e[..., None] == experts, pad_start, 0), axis=-1)
    dest = start_of + top_pos
    tok = jnp.broadcast_to(jnp.arange(n, dtype=jnp.int32)[None, :], dest.shape)
    slot_tok = jnp.zeros((n_blocks * MOE_ROWS,), jnp.int32).at[dest.reshape(-1)].set(tok.reshape(-1))
    xb = x1b[slot_tok]
    blk_row0 = jnp.arange(n_blocks, dtype=jnp.int32) * MOE_ROWS
    blk_e = jnp.minimum(jnp.sum((pad_end[None, :] <= blk_row0[:, None]).astype(jnp.int32), axis=1),
                        N_EXPERTS - 1)
    nblk = (pad_end[-1:] // MOE_ROWS).astype(jnp.int32)
    return xb, blk_e, nblk, dest, top_w, slot_tok


def _moe_finish(x1, dispatch, w_gu, b_gu, w_down, b_down, g2, b2, tm, after=None):
    xb, blk_e, nblk, dest, top_w, _ = dispatch
    if after is not None:
        nblk, _ = lax.optimization_barrier((nblk, after))
    yb = _experts(xb, blk_e, nblk, w_gu, b_gu, w_down, b_down)
    return _final_ln(x1, top_w.T, [yb[dest[kk]] for kk in range(TOP_K)], g2, b2, tm)


def _page_copy(cache_hbm, table_ref, bi, p, buf_ref, slot, c, sem_ref):
    return pltpu.make_async_copy(cache_hbm.at[table_ref[bi, p]], buf_ref.at[slot, c, p], sem_ref.at[slot, c])


def _fetch_pages(caches, table_ref, buf_ref, sem_ref):
    b = pl.program_id(0)
    n_pages = table_ref.shape[1]
    slot = b % 2

    def start(bi, sl):
        def body(p, carry):
            for c, cache in enumerate(caches):
                _page_copy(cache, table_ref, bi, p, buf_ref, sl, c, sem_ref).start()
            return carry
        lax.fori_loop(0, n_pages, body, 0)

    @pl.when(b == 0)
    def _():
        start(0, 0)

    @pl.when(b + 1 < pl.num_programs(0))
    def _():
        start(b + 1, 1 - slot)

    def wait(p, carry):
        for c, cache in enumerate(caches):
            _page_copy(cache, table_ref, b, p, buf_ref, slot, c, sem_ref).wait()
        return carry
    lax.fori_loop(0, n_pages, wait, 0)
    return slot


def _cmp_pages_kernel(table_ref, kcache, vcache, knew_ref, vnew_ref,
                      w1k_ref, bk_ref, w2k_ref, b2k_ref, w1v_ref, bv_ref, w2v_ref, b2v_ref,
                      ko_ref, vo_ref, buf_ref, rows_ref, sem_ref):
    slot = _fetch_pages((kcache, vcache), table_ref, buf_ref, sem_ref)
    n_pages = table_ref.shape[1]
    past = n_pages * PAGE_SIZE
    n_rows = rows_ref.shape[0]
    j = n_rows // CMP_STRIDE
    first_row = lax.broadcasted_iota(jnp.int32, (n_rows - past, KV_WIDTH), 0) == 0
    plan = ((knew_ref, w1k_ref, bk_ref, w2k_ref, b2k_ref, ko_ref),
            (vnew_ref, w1v_ref, bv_ref, w2v_ref, b2v_ref, vo_ref))
    for c, (new_ref, w1_ref, bias_ref, w2_ref, b2_ref, o_ref) in enumerate(plan):
        def to_rows(p, carry):
            off = pl.multiple_of(p * PAGE_SIZE, PAGE_SIZE)
            rows_ref[pl.ds(off, PAGE_SIZE), :] = buf_ref[slot, c, p].T
            return carry
        lax.fori_loop(0, n_pages, to_rows, 0, unroll=8)
        rows_ref[past:, :] = jnp.where(first_row, new_ref[0], 0.0)
        f = None
        for p in range(0, CMP_STRIDE, 2):
            x = jnp.concatenate([rows_ref[pl.ds(p, j, stride=CMP_STRIDE), :],
                                 rows_ref[pl.ds(p + 1, j, stride=CMP_STRIDE), :]], axis=1).astype(BF16)
            part = _dot(x, w1_ref[p * KV_WIDTH:(p + 2) * KV_WIDTH, :])
            f = part if f is None else f + part
        o_ref[0, 0:j, :] = _compress_tail(f, bias_ref, w2_ref, b2_ref)
        o_ref[0, j:, :] = jnp.zeros((o_ref.shape[1] - j, KV_WIDTH), F32)


def _cmp_pages(table, kcache, vcache, knew, vnew, cw_k, cw_v):
    bs, n_pages = table.shape
    past = n_pages * PAGE_SIZE
    n_rows = past + SUBLANES * CMP_STRIDE
    j = n_rows // CMP_STRIDE
    jp = -(-j // LANES) * LANES
    full = lambda a: pl.BlockSpec(a.shape, lambda b, tbl: (0,) * a.ndim)
    new = pl.BlockSpec((1, 1, KV_WIDTH), lambda b, tbl: (b, 0, 0))
    out = pl.BlockSpec((1, jp, KV_WIDTH), lambda b, tbl: (b, 0, 0))
    grid_spec = pltpu.PrefetchScalarGridSpec(
        num_scalar_prefetch=1, grid=(bs,),
        in_specs=[pl.BlockSpec(memory_space=pl.ANY), pl.BlockSpec(memory_space=pl.ANY), new, new]
        + [full(a) for a in cw_k] + [full(a) for a in cw_v],
        out_specs=[out, out],
        scratch_shapes=[pltpu.VMEM((2, 2, n_pages, KV_WIDTH, PAGE_SIZE), F32),
                        pltpu.VMEM((n_rows, KV_WIDTH), F32), pltpu.SemaphoreType.DMA((2, 2))])
    return pl.pallas_call(
        _cmp_pages_kernel, grid_spec=grid_spec,
        out_shape=[jax.ShapeDtypeStruct((bs, jp, KV_WIDTH), F32)] * 2,
        compiler_params=_params("arbitrary"), name="cmp_pages")(
            table, kcache, vcache, knew.reshape(bs, 1, KV_WIDTH), vnew.reshape(bs, 1, KV_WIDTH), *cw_k, *cw_v)


def _nsa_sample_kernel(table_ref, kcache, vcache, q8_ref, gcol_ref, kc_ref, vc_ref, kwt_ref, vwt_ref,
                       ksn_ref, vsn_ref, kwn_ref, vwn_ref, kwc_ref, vwc_ref,
                       o_ref, kwo_ref, vwo_ref, buf_ref, expand_ref, imp_ref, sem_ref):
    b = pl.program_id(0)
    slot = _fetch_pages((kcache, vcache), table_ref, buf_ref, sem_ref)
    past = table_ref.shape[1] * PAGE_SIZE
    q_pos = past
    sel_len = -(-(past + 1) // SEL_BLOCK) * SEL_BLOCK
    nc = sel_len // CMP_STRIDE - 1
    ns = sel_len // SEL_BLOCK
    ncp = kc_ref.shape[1]
    nsp = -(-ns // SUBLANES) * SUBLANES
    nsq = -(-nsp // LANES) * LANES
    nbp = expand_ref.shape[0]
    rows8 = NSA_HEADS
    cur = q_pos // SEL_BLOCK
    n_top = min(SEL_TOP, ns)

    @pl.when(b == 0)
    def _():
        blk_i = lax.broadcasted_iota(jnp.int32, (nbp, past), 0)
        key_i = lax.broadcasted_iota(jnp.int32, (nbp, past), 1)
        expand_ref[...] = (blk_i == key_i // SEL_BLOCK).astype(BF16)

    q8 = q8_ref[0] * (NSA_HEAD_DIM ** -0.5 * math.log2(math.e))
    q8b = q8.astype(BF16)

    st = _nt_dot(kc_ref[0].astype(BF16), q8b)
    n_idx = lax.broadcasted_iota(jnp.int32, (ncp, rows8), 0)
    st = jnp.where((n_idx * CMP_STRIDE + (CMP_BLOCK - 1) <= q_pos) & (n_idx < nc), st, NEG_INF)
    mc = jnp.max(st, axis=0, keepdims=True)
    mc = jnp.where(mc > NEG_INF, mc, 0.0)
    ec = jnp.exp2(st - mc)
    pt = ec / jnp.maximum(jnp.sum(ec, axis=0, keepdims=True), 1e-30)
    o_cmp_t = _dot(vc_ref[0].T.astype(BF16), pt.astype(BF16))
    o_cmp = jnp.concatenate([o_cmp_t, jnp.zeros((KV_WIDTH, LANES - rows8), F32)], axis=1).T[0:rows8, :]

    imp = jnp.concatenate([jnp.sum(pt[:, g * NSA_GROUP:(g + 1) * NSA_GROUP], axis=1, keepdims=True)
                           for g in range(NSA_KV_HEADS)], axis=1)
    imp_ref[0:SUBLANES, :] = jnp.zeros((SUBLANES, NSA_KV_HEADS), F32)
    imp_ref[SUBLANES:SUBLANES + ncp, :] = imp
    ratio = SEL_BLOCK // CMP_STRIDE
    p_slc = imp_ref[pl.ds(SUBLANES - 1, nsp, stride=ratio), :]
    for d in range(ratio):
        p_slc = p_slc + imp_ref[pl.ds(SUBLANES + d, nsp, stride=ratio), :]
    blk = lax.broadcasted_iota(jnp.int32, (nsp, NSA_KV_HEADS), 0)
    forced = (blk == 0) | (blk > cur - N_LOCAL_SEL)
    score = jnp.where(blk <= cur, p_slc + jnp.where(forced, FORCE_BONUS, 0.0), -FORCE_BONUS)
    low = -2.0 * FORCE_BONUS
    score_pad = jnp.concatenate([score, jnp.full((nsp, LANES - NSA_KV_HEADS), low, F32)], axis=1)
    score_pad = jnp.concatenate([score_pad, jnp.full((nsq - nsp, LANES), low, F32)], axis=0)
    score_rows = score_pad.T
    jp = lax.broadcasted_iota(jnp.int32, (nsp, nsq), 0)
    jj = lax.broadcasted_iota(jnp.int32, (nsp, nsq), 1)
    sel_rows = []
    for g in range(NSA_KV_HEADS):
        col = score[:, g:g + 1]
        row = score_rows[g:g + 1, :]
        beats = (col > row) | ((col == row) & (jp < jj))
        rank = jnp.sum(beats.astype(F32), axis=0, keepdims=True)
        sel_rows.append(((rank < float(n_top)) & (jj[0:1, :] <= cur)).astype(F32))
    row_head = lax.broadcasted_iota(jnp.int32, (rows8, nsq), 0) // NSA_GROUP
    sel8 = jnp.where(row_head == 0, sel_rows[0], sel_rows[1])

    chosen = _dot(sel8[:, 0:nbp].astype(BF16), expand_ref[...])
    n_pages = table_ref.shape[1]
    kt16 = jnp.concatenate([buf_ref[slot, 0, p].astype(BF16) for p in range(n_pages)], axis=1)
    vt16 = jnp.concatenate([buf_ref[slot, 1, p].astype(BF16) for p in range(n_pages)], axis=1)
    s = _dot(q8b, kt16) + jnp.where(chosen > 0.5, 0.0, NEG_INF)
    s_new = jnp.sum(q8 * ksn_ref[0], axis=1, keepdims=True)
    m = jnp.maximum(jnp.max(s, axis=1, keepdims=True), s_new)
    p = jnp.exp2(s - m)
    p_new = jnp.exp2(s_new - m)
    o_sel = ((_nt_dot(p.astype(BF16), vt16) + p_new * vsn_ref[0])
             / (jnp.sum(p, axis=1, keepdims=True) + p_new))

    kwt = kwt_ref[0]
    vwt = vwt_ref[0]
    w = kwt.shape[1]
    lane = lax.broadcasted_iota(jnp.int32, (rows8, w), 1)
    w_pos = past - w + lane
    sw = jnp.where((w_pos > q_pos - WINDOW) & (w_pos >= 0), _dot(q8b, kwt.astype(BF16)), NEG_INF)
    sw_new = jnp.sum(q8 * kwn_ref[0], axis=1, keepdims=True)
    mw = jnp.maximum(jnp.max(sw, axis=1, keepdims=True), sw_new)
    pw = jnp.exp2(sw - mw)
    pw_new = jnp.exp2(sw_new - mw)
    o_win = ((_nt_dot(pw.astype(BF16), vwt.astype(BF16)) + pw_new * vwn_ref[0])
             / (jnp.sum(pw, axis=1, keepdims=True) + pw_new))

    gates = gcol_ref[0]
    o_ref[0] = gates[:, 0:1] * o_cmp + gates[:, 1:2] * o_sel + gates[:, 2:3] * o_win

    last = lax.broadcasted_iota(jnp.int32, (KV_WIDTH, w), 1) == w - 1
    kwo_ref[0] = jnp.where(last, kwc_ref[0], pltpu.roll(kwt, w - 1, axis=1))
    vwo_ref[0] = jnp.where(last, vwc_ref[0], pltpu.roll(vwt, w - 1, axis=1))


def _nsa_sample(table, kcache, vcache, q8, gcol, kc_sum, vc_sum, kwt, vwt, ks_new, vs_new, kw_new, vw_new):
    bs, n_pages = table.shape
    past = n_pages * PAGE_SIZE
    w = kwt.shape[2]
    ncp = kc_sum.shape[1]
    nbp = -(-(past // SEL_BLOCK) // LANES) * LANES
    per_b = lambda a: pl.BlockSpec((1,) + a.shape[1:], lambda b, tbl: (b,) + (0,) * (a.ndim - 1))
    row = lambda a: a.reshape(bs, 1, KV_WIDTH)
    col = lambda a: a.reshape(bs, KV_WIDTH, 1)
    operands = (q8, gcol, kc_sum, vc_sum, kwt, vwt, row(ks_new), row(vs_new), row(kw_new), row(vw_new),
                col(kw_new), col(vw_new))
    grid_spec = pltpu.PrefetchScalarGridSpec(
        num_scalar_prefetch=1, grid=(bs,),
        in_specs=[pl.BlockSpec(memory_space=pl.ANY), pl.BlockSpec(memory_space=pl.ANY)]
        + [per_b(a) for a in operands],
        out_specs=[pl.BlockSpec((1, NSA_HEADS, KV_WIDTH), lambda b, tbl: (b, 0, 0)),
                   pl.BlockSpec((1, KV_WIDTH, w), lambda b, tbl: (b, 0, 0)),
                   pl.BlockSpec((1, KV_WIDTH, w), lambda b, tbl: (b, 0, 0))],
        scratch_shapes=[pltpu.VMEM((2, 2, n_pages, KV_WIDTH, PAGE_SIZE), F32), pltpu.VMEM((nbp, past), BF16),
                        pltpu.VMEM((SUBLANES + ncp, NSA_KV_HEADS), F32), pltpu.SemaphoreType.DMA((2, 2))])
    return pl.pallas_call(
        _nsa_sample_kernel, grid_spec=grid_spec,
        out_shape=[jax.ShapeDtypeStruct((bs, NSA_HEADS, KV_WIDTH), F32),
                   jax.ShapeDtypeStruct((bs, KV_WIDTH, w), F32), jax.ShapeDtypeStruct((bs, KV_WIDTH, w), F32)],
        compiler_params=_params("arbitrary"), name="nsa_sample")(table, kcache, vcache, *operands)


def _hgrn_sample_kernel(h4_ref, s_ref, lb_ref, ng_ref, o_ref, so_ref):
    nb = h4_ref.shape[0]
    hd = HGRN_HEAD_DIM
    hw = HGRN_WIDTH
    hq = h4_ref[:, 0:hw]
    lb = lb_ref[...]
    f = lb + (1.0 - lb) * jax.nn.sigmoid(h4_ref[:, hw:2 * hw])
    k = 1.0 - f
    q = hq * jax.nn.sigmoid(hq)
    hg = h4_ref[:, 3 * hw:4 * hw]
    for h in range(HGRN_HEADS):
        cols = slice(h * hd, (h + 1) * hd)
        stack = jnp.concatenate([f[:, cols], k[:, cols], q[:, cols],
                                 jnp.zeros((LANES - 3 * nb, hd), F32)], axis=0).T
        outs = []
        for bi in range(nb):
            v_row = h4_ref[bi:bi + 1, 2 * hw + h * hd:2 * hw + (h + 1) * hd]
            s_new = stack[:, bi:bi + 1] * s_ref[bi, h] + stack[:, nb + bi:nb + bi + 1] * v_row
            so_ref[bi, h] = s_new
            outs.append(jnp.sum(stack[:, 2 * nb + bi:2 * nb + bi + 1] * s_new, axis=0, keepdims=True))
        o = jnp.concatenate(outs, axis=0)
        o = o * lax.rsqrt(jnp.mean(o * o, axis=-1, keepdims=True) + LN_EPS) * ng_ref[...]
        o_ref[:, cols] = o * (hg[:, cols] * jax.nn.sigmoid(hg[:, cols]))


def _hgrn_sample(h4, s0, lb, norm_g):
    bs = h4.shape[0]
    nb = SUBLANES
    assert bs % nb == 0, bs
    st_spec = pl.BlockSpec((nb, HGRN_HEADS, HGRN_HEAD_DIM, HGRN_HEAD_DIM), lambda i: (i, 0, 0, 0))
    return pl.pallas_call(
        _hgrn_sample_kernel, grid=(bs // nb,),
        in_specs=[pl.BlockSpec((nb, 4 * HGRN_WIDTH), lambda i: (i, 0)), st_spec,
                  pl.BlockSpec((1, HGRN_WIDTH), lambda i: (0, 0)),
                  pl.BlockSpec((1, HGRN_HEAD_DIM), lambda i: (0, 0))],
        out_specs=[pl.BlockSpec((nb, HGRN_WIDTH), lambda i: (i, 0)), st_spec],
        out_shape=[jax.ShapeDtypeStruct((bs, HGRN_WIDTH), F32), jax.ShapeDtypeStruct(s0.shape, F32)],
        compiler_params=_params("parallel"), name="hgrn_sample")(h4, s0, lb, norm_g)


def kernel(x_prompt, x_sample, cache_k_cmp, cache_v_cmp, cache_k_sel, cache_v_sel, cache_k_win, cache_v_win, state_hgrn, page_table, w_in, cmp_pe, cmp_w1, cmp_b1, cmp_w2, cmp_b2, hgrn_gamma, hgrn_norm, w_branch_a, w_branch_b, w_out, ln1_g, ln1_b, w_router, b_router, w_gate_up, b_gate_up, w_down, b_down, ln2_g, ln2_b):
    l = 0
    bp, t, _ = x_prompt.shape
    bs = x_sample.shape[0]
    past = page_table.shape[1] * PAGE_SIZE
    win_keep = cache_k_win.shape[2]

    lower = jnp.cumsum(jax.nn.softmax(hgrn_gamma.astype(F32), axis=0), axis=0)[l][None, :]
    norm_g = hgrn_norm[l][None, :]
    w = w_in[l]
    c_g = NSA_WIDTH + 6 * KV_WIDTH
    n_g = 3 * NSA_HEADS
    w_main = jnp.concatenate([w[:, :c_g], w[:, c_g + n_g:]], axis=1).astype(BF16)
    w_gate_t = jnp.pad(w[:, c_g:c_g + n_g].T, ((0, GATE_ROWS - n_g), (0, 0))).astype(BF16)
    cw = [_compress_weights(cmp_pe[l, n], cmp_w1[l, n], cmp_b1[l, n], cmp_w2[l, n], cmp_b2[l, n]) for n in range(2)]
    wba = w_branch_a[l].astype(BF16)
    wbb = w_branch_b[l].astype(BF16)
    wout = w_out[l].astype(BF16)
    wr_t = w_router[l].T
    wr_hi = wr_t.astype(BF16)
    wr_lo = (wr_t - wr_hi.astype(F32)).astype(BF16)
    br = jnp.broadcast_to(b_router[l].astype(F32)[:, None], (N_EXPERTS, 1))
    g1, b1 = ln1_g[l][None, :], ln1_b[l][None, :]
    g2, b2 = ln2_g[l][None, :], ln2_b[l][None, :]

    def merge_and_dispatch(x2d, o_a, o_b, gab, tm):
        x1, x1b, logit_t = _finish(x2d, o_a, o_b, gab, wba, wbb, wout, g1, b1, wr_hi, wr_lo, br, tm)
        return x1, _moe_dispatch(x1b, logit_t)

    def experts_and_norm(x1, dispatch, tm, after=None):
        return _moe_finish(x1, dispatch, w_gate_up[l], b_gate_up[l], w_down[l], b_down[l], g2, b2, tm, after)

    n = bp * t
    xp = x_prompt.reshape(n, D_MODEL)
    q, kc, vc, ks, vs, kw, vw, h4, gab, gt, *kv_t = _project(xp, w_main, w_gate_t, 256, bp)
    r3 = lambda a: a.reshape(bp, t, a.shape[-1])
    kc_sum = _compress(r3(kc), *cw[0])
    vc_sum = _compress(r3(vc), *cw[1])
    o_nsa = _nsa_prompt(r3(q), gt, kc_sum, vc_sum, r3(ks), r3(vs), r3(kw), r3(vw))
    o_hgrn, p_state = _hgrn_prompt(r3(h4), lower, norm_g, bp, t)
    x1_p, disp_p = merge_and_dispatch(xp, o_nsa.reshape(n, NSA_WIDTH), o_hgrn.reshape(n, HGRN_WIDTH), gab, 256)
    win = min(WINDOW, t)
    t5 = lambda a: jnp.transpose(a.reshape(a.shape[0], NSA_KV_HEADS, NSA_HEAD_DIM, a.shape[2]), (0, 3, 1, 2))[None]
    new_p = (tuple(t5(a) for a in kv_t[:4]) + tuple(t5(a[:, :, t - win:]) for a in kv_t[4:])
             + (p_state[None],))

    page_table, _ = lax.optimization_barrier((page_table, disp_p[-1][:1]))
    xs = x_sample.reshape(bs, D_MODEL)
    q, kc, vc, ks, vs, kw, vw, h4, gab, gt, *_ = _project(xs, w_main, w_gate_t, bs, 1)
    pages = lambda c: jnp.transpose(c[l], (0, 2, 3, 1)).reshape(c.shape[1], KV_WIDTH, PAGE_SIZE)
    band = lambda c: jnp.transpose(c[l], (0, 2, 3, 1)).reshape(bs, KV_WIDTH, win_keep)
    kc_sum, vc_sum = _cmp_pages(page_table, pages(cache_k_cmp), pages(cache_v_cmp), kc, vc, cw[0], cw[1])
    head_eye = jnp.eye(NSA_KV_HEADS, dtype=F32)
    q8 = jnp.einsum('bgrd,gh->bgrhd', q.reshape(bs, NSA_KV_HEADS, NSA_GROUP, NSA_HEAD_DIM),
                    head_eye).reshape(bs, NSA_HEADS, KV_WIDTH)
    gcol = jnp.pad(gt[:3 * NSA_HEADS].T.reshape(bs, 3, NSA_HEADS).transpose(0, 2, 1),
                   ((0, 0), (0, 0), (0, LANES - 3)))
    o8, kw_t, vw_t = _nsa_sample(page_table, pages(cache_k_sel), pages(cache_v_sel), q8, gcol, kc_sum, vc_sum,
                                 band(cache_k_win), band(cache_v_win), ks, vs, kw, vw)
    o_nsa = jnp.einsum('bgrhd,gh->bgrd', o8.reshape(bs, NSA_KV_HEADS, NSA_GROUP, NSA_KV_HEADS, NSA_HEAD_DIM),
                       head_eye).reshape(bs, NSA_WIDTH)
    o_hgrn, s_state = _hgrn_sample(h4, state_hgrn[l], lower, norm_g)
    y_prompt = experts_and_norm(x1_p, disp_p, 256, after=(o8, s_state)).reshape(bp, t, D_MODEL)
    x1_s, disp_s = merge_and_dispatch(xs, o_nsa, o_hgrn, gab, bs)
    y_sample = experts_and_norm(x1_s, disp_s, bs).reshape(bs, 1, D_MODEL)
    s5 = lambda a: a.reshape(1, bs, 1, NSA_KV_HEADS, NSA_HEAD_DIM)
    w5 = lambda a: jnp.transpose(a.reshape(bs, NSA_KV_HEADS, NSA_HEAD_DIM, win_keep), (0, 3, 1, 2))[None]
    new_s = (s5(kc), s5(vc), s5(ks), s5(vs), w5(kw_t), w5(vw_t), s_state[None])

    return (y_prompt, y_sample) + new_p + new_s
```

```python
import functools
import math

import jax
import jax.numpy as jnp
from jax import lax
from jax.experimental import pallas as pl
from jax.experimental.pallas import tpu as pltpu

F32 = jnp.float32
BF16 = jnp.bfloat16

D_MODEL = 1024
PAGE_SIZE = 128
NSA_HEADS = 8
NSA_KV_HEADS = 2
NSA_GROUP = NSA_HEADS // NSA_KV_HEADS
NSA_HEAD_DIM = 64
NSA_WIDTH = NSA_HEADS * NSA_HEAD_DIM
KV_WIDTH = NSA_KV_HEADS * NSA_HEAD_DIM
CMP_BLOCK = 32
CMP_STRIDE = 16
CMP_HIDDEN = 128
SEL_BLOCK = 64
SEL_TOP = 16
N_LOCAL_SEL = 2
FORCE_BONUS = 1.0e4
WINDOW = 512
HGRN_HEADS = 4
HGRN_HEAD_DIM = 128
HGRN_WIDTH = HGRN_HEADS * HGRN_HEAD_DIM
HGRN_CHUNK = 16
N_EXPERTS = 32
TOP_K = 4
D_EXPERT = 1024
SWIGLU_LIMIT = 7.0
SWIGLU_ALPHA = 1.702
DEPTH = 1
DN_ALPHA = (2 * DEPTH) ** 0.25
LN_EPS = 1e-5

LANES = 128
SUBLANES = 8
VMEM_BYTES_V7X = 64 * 1024 * 1024
VMEM_LIMIT = VMEM_BYTES_V7X * 3 // 4

Q_TILE = 128
K_TILE = 128
SEL_SWEEP = 512
HGRN_TILE = 128
MOE_ROWS = 512
ROUTE_TILE = 256
NEG_INF = float("-inf")

_C_Q = 0
_C_KV = _C_Q + NSA_WIDTH
_C_H = _C_KV + 6 * KV_WIDTH
_C_GAB = _C_H + 4 * HGRN_WIDTH
_C_END = _C_GAB + 2 * D_MODEL
GATE_ROWS = 32


def _params(*sem):
    return pltpu.CompilerParams(dimension_semantics=sem, vmem_limit_bytes=VMEM_LIMIT)


def _nt_dot(a, b):
    return lax.dot_general(a, b, (((1,), (1,)), ((), ())), preferred_element_type=F32)


def _dot(a, b):
    return jnp.dot(a, b, preferred_element_type=F32)


def _proj_kernel(x_ref, w_ref, wg_ref, q_ref, kc_ref, vc_ref, ks_ref, vs_ref, kw_ref, vw_ref,
                 h_ref, gab_ref, gt_ref, *kvt_refs):
    x = x_ref[...].astype(BF16)
    q_ref[...] = _dot(x, w_ref[:, _C_Q:_C_KV])
    kv = _dot(x, w_ref[:, _C_KV:_C_H])
    for n, ref in enumerate((kc_ref, vc_ref, ks_ref, vs_ref, kw_ref, vw_ref)):
        ref[...] = kv[:, n * KV_WIDTH:(n + 1) * KV_WIDTH]
    for n, ref in enumerate(kvt_refs):
        for c in range(0, kv.shape[0], LANES):
            ref[0, :, c:c + LANES] = kv[c:c + LANES, n * KV_WIDTH:(n + 1) * KV_WIDTH].T
    h_ref[...] = _dot(x, w_ref[:, _C_H:_C_GAB])
    gab_ref[...] = _dot(x, w_ref[:, _C_GAB:_C_END])
    gt_ref[...] = jax.nn.sigmoid(_nt_dot(wg_ref[...], x))


def _project(x, w_main, w_gate_t, tm, batch):
    n = x.shape[0]
    per_b = n // batch // tm
    row = lambda w: pl.BlockSpec((tm, w), lambda i: (i, 0))
    full = lambda a: pl.BlockSpec(a.shape, lambda i: (0,) * a.ndim)
    out_shape = ([jax.ShapeDtypeStruct((n, NSA_WIDTH), F32)]
                 + [jax.ShapeDtypeStruct((n, KV_WIDTH), F32)] * 6
                 + [jax.ShapeDtypeStruct((n, 4 * HGRN_WIDTH), F32),
                    jax.ShapeDtypeStruct((n, 2 * D_MODEL), F32),
                    jax.ShapeDtypeStruct((GATE_ROWS, n), F32)]
                 + [jax.ShapeDtypeStruct((batch, KV_WIDTH, n // batch), F32)] * 6)
    out_specs = ([row(NSA_WIDTH)] + [row(KV_WIDTH)] * 6 + [row(4 * HGRN_WIDTH), row(2 * D_MODEL),
                 pl.BlockSpec((GATE_ROWS, tm), lambda i: (0, i))]
                 + [pl.BlockSpec((1, KV_WIDTH, tm), lambda i: (i // per_b, 0, i % per_b))] * 6)
    return pl.pallas_call(
        _proj_kernel, grid=(n // tm,),
        in_specs=[row(D_MODEL), full(w_main), full(w_gate_t)],
        out_specs=out_specs, out_shape=out_shape,
        compiler_params=_params("parallel"), name="in_proj")(x, w_main, w_gate_t)


def _gelu_tanh(x):
    return 0.5 * x * (1.0 + jnp.tanh(math.sqrt(2.0 / math.pi) * (x + 0.044715 * (x * x * x))))


def _compress_tail(f, bias_ref, w2_ref, b2_ref):
    j = f.shape[0]
    outs = []
    for g in range(NSA_KV_HEADS):
        base = g * 2 * CMP_HIDDEN
        first = f[:, base:base + CMP_HIDDEN]
        second = f[:, base + CMP_HIDDEN:base + 2 * CMP_HIDDEN]
        nxt = pltpu.roll(second, j - 1, axis=0)
        h = _gelu_tanh(first + nxt + bias_ref[g:g + 1, :])
        outs.append(_dot(h.astype(BF16), w2_ref[...]) + b2_ref[...])
    return jnp.concatenate(outs, axis=1)


def _compress_kernel(c_ref, w1_ref, bias_ref, w2_ref, b2_ref, o_ref):
    c = c_ref[0].astype(BF16)
    o_ref[0] = _compress_tail(_dot(c, w1_ref[...]), bias_ref, w2_ref, b2_ref)


def _compress(rows, w1full, bias, w2, b2):
    b, t, _ = rows.shape
    j = t // CMP_STRIDE
    c = rows.reshape(b, j, CMP_STRIDE * KV_WIDTH)
    full = lambda a: pl.BlockSpec(a.shape, lambda i: (0,) * a.ndim)
    return pl.pallas_call(
        _compress_kernel, grid=(b,),
        in_specs=[pl.BlockSpec((1, j, CMP_STRIDE * KV_WIDTH), lambda i: (i, 0, 0)),
                  full(w1full), full(bias), full(w2), full(b2)],
        out_specs=pl.BlockSpec((1, j, KV_WIDTH), lambda i: (i, 0, 0)),
        out_shape=jax.ShapeDtypeStruct((b, j, KV_WIDTH), F32),
        compiler_params=_params("parallel"), name="compress")(c, w1full, bias, w2, b2)


def _compress_weights(pe, w1, b1, w2, b2):
    hd, hid = NSA_HEAD_DIM, CMP_HIDDEN
    halves = w1.reshape(2, CMP_STRIDE, hd, hid)
    eye = jnp.eye(NSA_KV_HEADS, dtype=w1.dtype)
    w1full = jnp.einsum('apdh,kg->pkdgah', halves, eye).reshape(
        CMP_STRIDE * KV_WIDTH, NSA_KV_HEADS * 2 * hid).astype(BF16)
    bias = jnp.einsum('pd,pdh->h', pe, w1, precision=lax.Precision.HIGHEST) + b1
    bias = jnp.broadcast_to(bias[None, :], (SUBLANES, hid))
    return w1full, bias, w2.astype(BF16), b2.reshape(1, hd)


def _nsa_prompt_kernel(q_ref, gt_ref, kc_ref, vc_ref, ks_ref, vs_ref, kw_ref, vw_ref, o_ref,
                       ks16_ref, vst16_ref, kw16_ref, vwt16_ref, imp_ref, score_ref, selb_ref, seloff_ref,
                       m_ref, l_ref, acc_ref, sc_a_ref, sc_b_ref, mt_a_ref, mt_b_ref):
    g = pl.program_id(1)
    i = pl.program_id(2)
    t0 = pl.multiple_of(i * Q_TILE, Q_TILE)
    rows = NSA_GROUP * Q_TILE
    nc = kc_ref.shape[1]
    ns = score_ref.shape[0]
    t_len = ks_ref.shape[1]
    hd = NSA_HEAD_DIM

    @pl.when((g == 0) & (i == 0))
    def _():
        def cast_step(c, carry):
            r0 = pl.multiple_of(c * K_TILE, K_TILE)
            ks16_ref[pl.ds(r0, K_TILE), :] = ks_ref[0, pl.ds(r0, K_TILE), :].astype(BF16)
            kw16_ref[pl.ds(r0, K_TILE), :] = kw_ref[0, pl.ds(r0, K_TILE), :].astype(BF16)
            vst16_ref[:, pl.ds(r0, K_TILE)] = vs_ref[0, pl.ds(r0, K_TILE), :].T.astype(BF16)
            vwt16_ref[:, pl.ds(r0, K_TILE)] = vw_ref[0, pl.ds(r0, K_TILE), :].T.astype(BF16)
            return carry
        lax.fori_loop(0, t_len // K_TILE, cast_step, 0)

    q = q_ref[0] * (hd ** -0.5 * math.log2(math.e))
    lane_head = lax.broadcasted_iota(jnp.int32, (Q_TILE, KV_WIDTH), 1) // NSA_HEAD_DIM
    parts = []
    for r in range(NSA_GROUP):
        qr = q[:, r * NSA_HEAD_DIM:(r + 1) * NSA_HEAD_DIM]
        parts.append(jnp.where(lane_head == g, jnp.concatenate([qr, qr], axis=1), 0.0))
    qs = jnp.concatenate(parts, axis=0).astype(BF16)

    tok = t0 + lax.broadcasted_iota(jnp.int32, (1, Q_TILE), 1)

    def tile4(bias):
        return jnp.concatenate([bias] * NSA_GROUP, axis=1)

    head_rows = pl.ds(pl.multiple_of(g * hd, hd), hd)

    wk = WINDOW + Q_TILE
    w0 = pl.multiple_of(jnp.maximum(t0 - WINDOW, 0), Q_TILE)
    s = _nt_dot(kc_ref[0].astype(BF16), qs)
    sw = _nt_dot(kw16_ref[pl.ds(w0, wk), :], qs)
    sd = _nt_dot(ks16_ref[pl.ds(t0, Q_TILE), :], qs)
    sc_a_ref[...] = _nt_dot(ks16_ref[0:SEL_SWEEP, :], qs)

    last_ok = jnp.minimum(jnp.right_shift(tok - (CMP_BLOCK - 1), 4), nc - 2)
    n_idx = lax.broadcasted_iota(jnp.int32, (nc, Q_TILE), 0)
    s = s + tile4(jnp.where(n_idx <= last_ok, 0.0, NEG_INF))
    m = jnp.max(s, axis=0, keepdims=True)
    m = jnp.where(m > NEG_INF, m, 0.0)
    e = jnp.exp2(s - m)
    p = e / jnp.maximum(jnp.sum(e, axis=0, keepdims=True), 1e-30)

    key_w = w0 + lax.broadcasted_iota(jnp.int32, (wk, Q_TILE), 0)
    sw = sw + tile4(jnp.where(lax.bitcast_convert_type(tok - key_w, jnp.uint32) < WINDOW, 0.0, NEG_INF))
    pw = jnp.exp2(sw - jnp.max(sw, axis=0, keepdims=True))

    key_d = t0 + lax.broadcasted_iota(jnp.int32, (Q_TILE, Q_TILE), 0)
    sd = sd + tile4(jnp.where(key_d <= tok, 0.0, NEG_INF))
    md = jnp.max(sd, axis=0, keepdims=True)
    pd = jnp.exp2(sd - md)

    o_cmp = _dot(vc_ref[0].T.astype(BF16), p.astype(BF16))
    o_cmp = jnp.where(g == 0, o_cmp[:hd], o_cmp[hd:])
    def pv_and_sum(vt_ref, k0, size, probs):
        vt = jnp.concatenate([vt_ref[head_rows, pl.ds(k0, size)], jnp.ones((2 * SUBLANES, size), BF16)], axis=0)
        r = _dot(vt, probs.astype(BF16))
        return r[:hd], r[hd:hd + 1]

    pv_w, l_w = pv_and_sum(vwt16_ref, w0, wk, pw)
    o_win = pv_w / l_w
    pv_d, l_d = pv_and_sum(vst16_ref, t0, Q_TILE, pd)
    m_ref[...] = md
    l_ref[...] = l_d
    acc_ref[...] = pv_d

    imp = p[:, 0:Q_TILE]
    for r in range(1, NSA_GROUP):
        imp = imp + p[:, r * Q_TILE:(r + 1) * Q_TILE]
    imp_ref[0:SUBLANES, :] = jnp.zeros((SUBLANES, Q_TILE), F32)
    imp_ref[SUBLANES:SUBLANES + nc, :] = imp
    ratio = SEL_BLOCK // CMP_STRIDE
    p_slc = imp_ref[pl.ds(SUBLANES - 1, ns, stride=ratio), :]
    for d in range(ratio):
        p_slc = p_slc + imp_ref[pl.ds(SUBLANES + d, ns, stride=ratio), :]
    blk = lax.broadcasted_iota(jnp.int32, (ns, Q_TILE), 0)
    cur = tok // SEL_BLOCK
    forced = (blk == 0) | (blk > cur - N_LOCAL_SEL)
    score = jnp.where(blk <= cur, p_slc + jnp.where(forced, FORCE_BONUS, 0.0), -FORCE_BONUS)
    key = lax.bitcast_convert_type(score, jnp.int32)
    score_ref[...] = key

    def rank_step(jp, rank):
        row = score_ref[pl.ds(jp, 1), :]
        return rank + jnp.where(row > jnp.where(jp < blk, key - 1, key), 1.0, 0.0)
    n_top = min(SEL_TOP, ns)
    visible = blk <= cur

    @pl.when(t0 + Q_TILE <= n_top * SEL_BLOCK)
    def _():
        selb_ref[...] = jnp.where(visible, 0.0, NEG_INF)

    @pl.when(t0 + Q_TILE > n_top * SEL_BLOCK)
    def _():
        rank = lax.fori_loop(0, ns, rank_step, jnp.zeros((ns, Q_TILE), F32), unroll=8)
        selb_ref[...] = jnp.where((rank < float(n_top)) & visible, 0.0, NEG_INF)

    first_blk = i * (Q_TILE // SEL_BLOCK)
    seloff_ref[...] = jnp.where(blk < first_blk, selb_ref[...], NEG_INF)
    per_tile = SEL_SWEEP // SEL_BLOCK
    n_sweep = (t0 + SEL_SWEEP - 1) // SEL_SWEEP

    last_tile = t_len // SEL_SWEEP - 1

    def mask_tile(kt, raw, s_ref, mx_ref):
        bias = jnp.concatenate(
            [jnp.broadcast_to(seloff_ref[pl.ds(kt * per_tile + j, 1), :], (SEL_BLOCK, Q_TILE))
             for j in range(per_tile)], axis=0)
        sc = raw + tile4(bias)
        s_ref[...] = sc
        mx_ref[...] = jnp.max(sc, axis=0, keepdims=True)

    def score_tile(kt, s_ref, mx_ref):
        k0 = pl.multiple_of(kt * SEL_SWEEP, SEL_SWEEP)
        mask_tile(kt, _nt_dot(ks16_ref[pl.ds(k0, SEL_SWEEP), :], qs), s_ref, mx_ref)

    def consume_tile(kt, s_ref, mx_ref):
        k0 = pl.multiple_of(kt * SEL_SWEEP, SEL_SWEEP)
        m_new = jnp.maximum(m_ref[...], mx_ref[...])
        alpha = jnp.exp2(m_ref[...] - m_new)
        pv, l_new = pv_and_sum(vst16_ref, k0, SEL_SWEEP, jnp.exp2(s_ref[...] - m_new))
        l_ref[...] = alpha * l_ref[...] + l_new
        acc_ref[...] = alpha * acc_ref[...] + pv
        m_ref[...] = m_new

    @pl.when(n_sweep > 0)
    def _():
        mask_tile(0, sc_a_ref[...], sc_a_ref, mt_a_ref)

    def sel_step(kp, carry):
        kt = kp * 2
        score_tile(jnp.minimum(kt + 1, last_tile), sc_b_ref, mt_b_ref)
        consume_tile(kt, sc_a_ref, mt_a_ref)
        score_tile(jnp.minimum(kt + 2, last_tile), sc_a_ref, mt_a_ref)
        consume_tile(jnp.minimum(kt + 1, last_tile), sc_b_ref, mt_b_ref)
        return carry
    lax.fori_loop(0, (n_sweep + 1) // 2, sel_step, 0)
    o_sel = acc_ref[...] / l_ref[...]

    def gate(branch):
        gr = gt_ref[pl.ds(branch * NSA_HEADS + g * NSA_GROUP, NSA_GROUP), :]
        return jnp.concatenate([gr[r:r + 1, :] for r in range(NSA_GROUP)], axis=1)
    o_t = gate(0) * o_cmp + gate(1) * o_sel + gate(2) * o_win
    outs = []
    for r in range(0, NSA_GROUP, 2):
        pair = jnp.concatenate([o_t[:, r * Q_TILE:(r + 1) * Q_TILE],
                                o_t[:, (r + 1) * Q_TILE:(r + 2) * Q_TILE]], axis=0)
        outs.append(pair.T)
    o_ref[0] = jnp.concatenate(outs, axis=1)


def _nsa_prompt(q, gt, kc_sum, vc_sum, ks, vs, kw, vw):
    b, t, _ = q.shape
    assert t % (2 * SEL_SWEEP) == 0 and t >= WINDOW + Q_TILE, t
    nt = t // Q_TILE
    nc = kc_sum.shape[1]
    ns = t // SEL_BLOCK
    rows = NSA_GROUP * Q_TILE
    per_b = lambda a: pl.BlockSpec((1,) + a.shape[1:], lambda bi, g, i: (bi, 0, 0))
    return pl.pallas_call(
        _nsa_prompt_kernel, grid=(b, NSA_KV_HEADS, nt),
        in_specs=[pl.BlockSpec((1, Q_TILE, NSA_WIDTH // NSA_KV_HEADS), lambda bi, g, i: (bi, i, g)),
                  pl.BlockSpec((GATE_ROWS, Q_TILE), lambda bi, g, i: (0, bi * nt + i)),
                  per_b(kc_sum), per_b(vc_sum), per_b(ks), per_b(vs), per_b(kw), per_b(vw)],
        out_specs=pl.BlockSpec((1, Q_TILE, NSA_WIDTH // NSA_KV_HEADS), lambda bi, g, i: (bi, i, g)),
        out_shape=jax.ShapeDtypeStruct((b, t, NSA_WIDTH), F32),
        scratch_shapes=[pltpu.VMEM((t, KV_WIDTH), BF16),
                        pltpu.VMEM((KV_WIDTH, t), BF16),
                        pltpu.VMEM((t, KV_WIDTH), BF16),
                        pltpu.VMEM((KV_WIDTH, t), BF16),
                        pltpu.VMEM((SUBLANES + nc, Q_TILE), F32),
                        pltpu.VMEM((ns, Q_TILE), jnp.int32),
                        pltpu.VMEM((ns, Q_TILE), F32),
                        pltpu.VMEM((ns, Q_TILE), F32),
                        pltpu.VMEM((1, rows), F32),
                        pltpu.VMEM((1, rows), F32),
                        pltpu.VMEM((NSA_HEAD_DIM, rows), F32),
                        pltpu.VMEM((SEL_SWEEP, rows), F32),
                        pltpu.VMEM((SEL_SWEEP, rows), F32),
                        pltpu.VMEM((1, rows), F32),
                        pltpu.VMEM((1, rows), F32)],
        compiler_params=_params("arbitrary", "arbitrary", "arbitrary"),
        name="nsa_prompt")(q, gt, kc_sum, vc_sum, ks, vs, kw, vw)


def _hgrn_prompt_kernel(hq_ref, hf_ref, hi_ref, hg_ref, lb_ref, ng_ref, o_ref, st_out_ref, st_ref):
    c = pl.program_id(1)
    n = HGRN_TILE
    sub = HGRN_CHUNK
    hd = HGRN_HEAD_DIM

    @pl.when(c == 0)
    def _():
        st_ref[...] = jnp.zeros(st_ref.shape, F32)

    pos = lax.broadcasted_iota(jnp.int32, (n, hd), 0) % sub
    rc = lax.broadcasted_iota(jnp.int32, (n, n), 0)
    cc = lax.broadcasted_iota(jnp.int32, (n, n), 1)
    intra = (rc // sub == cc // sub) & (cc <= rc)
    tok_chunk = lax.broadcasted_iota(jnp.int32, (hd, n), 1) // sub

    for h in range(HGRN_HEADS):
        cols = slice(h * hd, (h + 1) * hd)
        lb = lb_ref[:, cols]
        f = lb + (1.0 - lb) * jax.nn.sigmoid(hf_ref[0, :, cols])
        logf = jnp.log(f)
        b = logf
        suf = logf
        sh = 1
        while sh < sub:
            b = b + jnp.where(pos >= sh, pltpu.roll(b, sh, axis=0), 0.0)
            suf = suf + jnp.where(pos + sh < sub, pltpu.roll(suf, n - sh, axis=0), 0.0)
            sh *= 2
        hq = hq_ref[0, :, cols]
        k = 1.0 - f
        q_dec = (hq * jax.nn.sigmoid(hq) * jnp.exp(b)).astype(BF16)
        k_inv = (k * jnp.exp(-b)).astype(BF16)
        k_end = (k * jnp.exp(suf - logf)).astype(BF16)
        v = hi_ref[0, :, cols]

        a = jnp.where(intra, _nt_dot(q_dec, k_inv), 0.0)
        o = _dot(a.astype(BF16), v.astype(BF16))

        vt = v.T
        kv_t = [_dot(jnp.where(tok_chunk == ci, vt, 0.0).astype(BF16), k_end)
                for ci in range(n // sub)]
        states = [st_ref[h]]
        for ci in range(n // sub):
            states.append(states[-1] * jnp.exp(suf[ci * sub:ci * sub + 1, :]) + kv_t[ci])
        st = states[-1]
        st_ref[h] = st
        inter = [_nt_dot(q_dec[ci * sub:(ci + 1) * sub, :], states[ci].astype(BF16))
                 for ci in range(n // sub)]
        o = o + jnp.concatenate(inter, axis=0)
        o = o * lax.rsqrt(jnp.mean(o * o, axis=-1, keepdims=True) + LN_EPS) * ng_ref[...]
        hg = hg_ref[0, :, cols]
        o_ref[0, :, cols] = o * (hg * jax.nn.sigmoid(hg))

    @pl.when(c == pl.num_programs(1) - 1)
    def _():
        for h in range(HGRN_HEADS):
            st_out_ref[0, h] = st_ref[h].T


def _hgrn_prompt(h4, lb, norm_g, b, t):
    nchunk = t // HGRN_TILE
    part = lambda p: pl.BlockSpec((1, HGRN_TILE, HGRN_WIDTH), lambda bi, c: (bi, c, p))
    return pl.pallas_call(
        _hgrn_prompt_kernel, grid=(b, nchunk),
        in_specs=[part(0), part(1), part(2), part(3),
                  pl.BlockSpec((1, HGRN_WIDTH), lambda bi, c: (0, 0)),
                  pl.BlockSpec((1, HGRN_HEAD_DIM), lambda bi, c: (0, 0))],
        out_specs=[pl.BlockSpec((1, HGRN_TILE, HGRN_WIDTH), lambda bi, c: (bi, c, 0)),
                   pl.BlockSpec((1, HGRN_HEADS, HGRN_HEAD_DIM, HGRN_HEAD_DIM), lambda bi, c: (bi, 0, 0, 0))],
        out_shape=[jax.ShapeDtypeStruct((b, t, HGRN_WIDTH), F32),
                   jax.ShapeDtypeStruct((b, HGRN_HEADS, HGRN_HEAD_DIM, HGRN_HEAD_DIM), F32)],
        scratch_shapes=[pltpu.VMEM((HGRN_HEADS, HGRN_HEAD_DIM, HGRN_HEAD_DIM), F32)],
        compiler_params=_params("parallel", "arbitrary"),
        name="hgrn_prompt")(h4, h4, h4, h4, lb, norm_g)


def _layer_norm(y, g, b):
    mu = jnp.mean(y, axis=-1, keepdims=True)
    d = y - mu
    var = jnp.mean(d * d, axis=-1, keepdims=True)
    return d * lax.rsqrt(var + LN_EPS) * g + b


def _finish_kernel(x_ref, oa_ref, ob_ref, gab_ref, wba_ref, wbb_ref, wout_ref, g_ref, b_ref,
                   wr_hi_ref, wr_lo_ref, br_ref, x1_ref, x1b_ref, lt_ref):
    tm = x_ref.shape[0]
    halves = [slice(0, tm // 2), slice(tm // 2, tm)] if tm % (2 * LANES) == 0 else [slice(0, tm)]
    ab = [(_dot(oa_ref[r, :].astype(BF16), wba_ref[...]), _dot(ob_ref[r, :].astype(BF16), wbb_ref[...]))
          for r in halves]
    mixes = [(jax.nn.sigmoid(gab_ref[r, :D_MODEL]) * a + jax.nn.sigmoid(gab_ref[r, D_MODEL:]) * bb).astype(BF16)
             for r, (a, bb) in zip(halves, ab)]
    outs = [_dot(mix, wout_ref[...]) for mix in mixes]
    his, los = [], []
    for r, out in zip(halves, outs):
        x1 = _layer_norm(DN_ALPHA * x_ref[r, :] + out, g_ref[...], b_ref[...])
        x1_ref[r, :] = x1
        hi = x1.astype(BF16)
        x1b_ref[r, :] = hi
        his.append(hi)
        los.append((x1 - hi.astype(F32)).astype(BF16))
    for r, hi, lo in zip(halves, his, los):
        lt_ref[:, r] = (_nt_dot(wr_hi_ref[...], hi) + _nt_dot(wr_hi_ref[...], lo)
                        + _nt_dot(wr_lo_ref[...], hi) + br_ref[...])


def _finish(x, oa, ob, gab, wba, wbb, wout, g1, b1, wr_hi, wr_lo, br, tm):
    n = x.shape[0]
    row = lambda w: pl.BlockSpec((tm, w), lambda i: (i, 0))
    full = lambda a: pl.BlockSpec(a.shape, lambda i: (0,) * a.ndim)
    return pl.pallas_call(
        _finish_kernel, grid=(n // tm,),
        in_specs=[row(D_MODEL), row(NSA_WIDTH), row(HGRN_WIDTH), row(2 * D_MODEL),
                  full(wba), full(wbb), full(wout), full(g1), full(b1), full(wr_hi), full(wr_lo), full(br)],
        out_specs=[row(D_MODEL), row(D_MODEL), pl.BlockSpec((N_EXPERTS, tm), lambda i: (0, i))],
        out_shape=[jax.ShapeDtypeStruct((n, D_MODEL), F32), jax.ShapeDtypeStruct((n, D_MODEL), BF16),
                   jax.ShapeDtypeStruct((N_EXPERTS, n), F32)],
        compiler_params=_params("parallel"), name="merge_ln_router")(
            x, oa, ob, gab, wba, wbb, wout, g1, b1, wr_hi, wr_lo, br)


def _route_kernel(lt_ref, tri_ref, e_ref, w_ref, pos_ref, cnt_ref, carry_ref):
    i = pl.program_id(0)

    @pl.when(i == 0)
    def _():
        carry_ref[...] = jnp.zeros(carry_ref.shape, F32)

    logit = lt_ref[...]
    tn = logit.shape[1]
    eid = lax.broadcasted_iota(jnp.int32, (N_EXPERTS, tn), 0)
    rank = jnp.zeros((N_EXPERTS, tn), F32)
    for ep in range(N_EXPERTS):
        row = logit[ep:ep + 1, :]
        rank = rank + ((row > logit) | ((row == logit) & (ep < eid))).astype(F32)
    sel = rank < float(TOP_K)
    top = jnp.max(logit, axis=0, keepdims=True)
    ex = jnp.where(sel, jnp.exp(logit - top), 0.0)
    wgt = ex / jnp.sum(ex, axis=0, keepdims=True)
    self = sel.astype(F32)
    incl = _dot(self.astype(BF16), tri_ref[...])
    pos = carry_ref[:, 0:1] + incl - self
    carry_ref[...] = carry_ref[...] + jnp.sum(self, axis=1, keepdims=True)
    eid_f = eid.astype(F32)
    for kk in range(TOP_K):
        pick = sel & (rank == float(kk))
        e_ref[kk:kk + 1, :] = jnp.sum(jnp.where(pick, eid_f, 0.0), axis=0, keepdims=True).astype(jnp.int32)
        w_ref[kk:kk + 1, :] = jnp.sum(jnp.where(pick, wgt, 0.0), axis=0, keepdims=True)
        pos_ref[kk:kk + 1, :] = jnp.sum(jnp.where(pick, pos, 0.0), axis=0, keepdims=True).astype(jnp.int32)
    cnt_ref[...] = carry_ref[...]


def _route(logit_t):
    n = logit_t.shape[1]
    tn = ROUTE_TILE if n % ROUTE_TILE == 0 else n
    tri = (lax.broadcasted_iota(jnp.int32, (tn, tn), 0) <= lax.broadcasted_iota(jnp.int32, (tn, tn), 1)).astype(BF16)
    col = lambda r: pl.BlockSpec((r, tn), lambda i: (0, i))
    return pl.pallas_call(
        _route_kernel, grid=(n // tn,),
        in_specs=[col(N_EXPERTS), pl.BlockSpec((tn, tn), lambda i: (0, 0))],
        out_specs=[col(TOP_K), col(TOP_K), col(TOP_K), pl.BlockSpec((N_EXPERTS, LANES), lambda i: (0, 0))],
        out_shape=[jax.ShapeDtypeStruct((TOP_K, n), jnp.int32), jax.ShapeDtypeStruct((TOP_K, n), F32),
                   jax.ShapeDtypeStruct((TOP_K, n), jnp.int32), jax.ShapeDtypeStruct((N_EXPERTS, LANES), F32)],
        scratch_shapes=[pltpu.VMEM((N_EXPERTS, LANES), F32)],
        compiler_params=_params("arbitrary"), name="route")(logit_t, tri)


def _expert_kernel(blk_e_ref, nblk_ref, xb_ref, wgu_ref, bgu_ref, wd_ref, bd_ref, y_ref, wgu_b, wd_b):
    i = pl.program_id(0)
    live = i < nblk_ref[0]

    @pl.when(live & ((i == 0) | (blk_e_ref[i] != blk_e_ref[jnp.maximum(i - 1, 0)])))
    def _():
        wgu_b[...] = wgu_ref[0].astype(BF16)
        wd_b[...] = wd_ref[0].astype(BF16)

    @pl.when(live)
    def _():
        gu = _dot(xb_ref[...], wgu_b[...]) + bgu_ref[0]
        gate = jnp.minimum(gu[:, :D_EXPERT], SWIGLU_LIMIT)
        up = jnp.clip(gu[:, D_EXPERT:], -SWIGLU_LIMIT, SWIGLU_LIMIT)
        h = gate * jax.nn.sigmoid(SWIGLU_ALPHA * gate) * (up + 1.0)
        y_ref[...] = _dot(h.astype(BF16), wd_b[...]) + bd_ref[0]

    @pl.when(i >= nblk_ref[0])
    def _():
        y_ref[...] = jnp.zeros(y_ref.shape, F32)


def _experts(xb, blk_e, nblk, w_gu, b_gu, w_down, b_down):
    m = xb.shape[0]
    grid_spec = pltpu.PrefetchScalarGridSpec(
        num_scalar_prefetch=2, grid=(m // MOE_ROWS,),
        in_specs=[pl.BlockSpec((MOE_ROWS, D_MODEL), lambda i, be, nb: (i, 0)),
                  pl.BlockSpec((1, D_MODEL, 2 * D_EXPERT), lambda i, be, nb: (be[i], 0, 0)),
                  pl.BlockSpec((1, 1, 2 * D_EXPERT), lambda i, be, nb: (be[i], 0, 0)),
                  pl.BlockSpec((1, D_EXPERT, D_MODEL), lambda i, be, nb: (be[i], 0, 0)),
                  pl.BlockSpec((1, 1, D_MODEL), lambda i, be, nb: (be[i], 0, 0))],
        out_specs=pl.BlockSpec((MOE_ROWS, D_MODEL), lambda i, be, nb: (i, 0)),
        scratch_shapes=[pltpu.VMEM((D_MODEL, 2 * D_EXPERT), BF16), pltpu.VMEM((D_EXPERT, D_MODEL), BF16)])
    return pl.pallas_call(
        _expert_kernel, grid_spec=grid_spec,
        out_shape=jax.ShapeDtypeStruct((m, D_MODEL), F32),
        compiler_params=_params("arbitrary"), name="experts")(
            blk_e, nblk, xb, w_gu, b_gu.reshape(N_EXPERTS, 1, 2 * D_EXPERT),
            w_down, b_down.reshape(N_EXPERTS, 1, D_MODEL))


def _final_ln_kernel(x1_ref, w_ref, y0_ref, y1_ref, y2_ref, y3_ref, g_ref, b_ref, o_ref):
    w = w_ref[...]
    y = y0_ref[...] * w[:, 0:1]
    for kk, y_ref in enumerate((y1_ref, y2_ref, y3_ref), start=1):
        y = y + y_ref[...] * w[:, kk:kk + 1]
    o_ref[...] = _layer_norm(DN_ALPHA * x1_ref[...] + y, g_ref[...], b_ref[...])


def _final_ln(x1, w_tok, ys, g2, b2, tm):
    n = x1.shape[0]
    row = pl.BlockSpec((tm, D_MODEL), lambda i: (i, 0))
    vec = pl.BlockSpec((1, D_MODEL), lambda i: (0, 0))
    return pl.pallas_call(
        _final_ln_kernel, grid=(n // tm,),
        in_specs=[row, pl.BlockSpec((tm, TOP_K), lambda i: (i, 0))] + [row] * TOP_K + [vec, vec],
        out_specs=row, out_shape=jax.ShapeDtypeStruct((n, D_MODEL), F32),
        compiler_params=_params("parallel"), name="final_ln")(x1, w_tok, *ys, g2, b2)


def _moe_dispatch(x1b, logit_t):
    n = x1b.shape[0]
    top_e, top_w, top_pos, counts = _route(logit_t)
    counts = counts[:, 0].astype(jnp.int32)
    padded = (counts + MOE_ROWS - 1) // MOE_ROWS * MOE_ROWS
    pad_end = jnp.cumsum(padded)
    pad_start = pad_end - padded
    n_blocks = -(-(n * TOP_K + N_EXPERTS * (MOE_ROWS - 1)) // MOE_ROWS)
    experts = jnp.arange(N_EXPERTS, dtype=jnp.int32)
    start_of = jnp.sum(jnp.where(top_e[..., None] == experts, pad_start, 0), axis=-1)
    dest = start_of + top_pos
    tok = jnp.broadcast_to(jnp.arange(n, dtype=jnp.int32)[None, :], dest.shape)
    slot_tok = jnp.zeros((n_blocks * MOE_ROWS,), jnp.int32).at[dest.reshape(-1)].set(tok.reshape(-1))
    xb = x1b[slot_tok]
    blk_row0 = jnp.arange(n_blocks, dtype=jnp.int32) * MOE_ROWS
    blk_e = jnp.minimum(jnp.sum((pad_end[None, :] <= blk_row0[:, None]).astype(jnp.int32), axis=1),
                        N_EXPERTS - 1)
    nblk = (pad_end[-1:] // MOE_ROWS).astype(jnp.int32)
    return xb, blk_e, nblk, dest, top_w, slot_tok


def _moe_finish(x1, dispatch, w_gu, b_gu, w_down, b_down, g2, b2, tm, after=None):
    xb, blk_e, nblk, dest, top_w, _ = dispatch
    if after is not None:
        nblk, _ = lax.optimization_barrier((nblk, after))
    yb = _experts(xb, blk_e, nblk, w_gu, b_gu, w_down, b_down)
    return _final_ln(x1, top_w.T, [yb[dest[kk]] for kk in range(TOP_K)], g2, b2, tm)


def _page_copy(cache_hbm, table_ref, bi, p, buf_ref, slot, c, sem_ref):
    return pltpu.make_async_copy(cache_hbm.at[table_ref[bi, p]], buf_ref.at[slot, c, p], sem_ref.at[slot, c])


def _fetch_pages(caches, table_ref, buf_ref, sem_ref):
    b = pl.program_id(0)
    n_pages = table_ref.shape[1]
    slot = b % 2

    def start(bi, sl):
        def body(p, carry):
            for c, cache in enumerate(caches):
                _page_copy(cache, table_ref, bi, p, buf_ref, sl, c, sem_ref).start(priority=c % 2)
            return carry
        lax.fori_loop(0, n_pages, body, 0)

    @pl.when(b == 0)
    def _():
        start(0, 0)

    @pl.when(b + 1 < pl.num_programs(0))
    def _():
        start(b + 1, 1 - slot)

    def wait(p, carry):
        for c, cache in enumerate(caches):
            _page_copy(cache, table_ref, b, p, buf_ref, slot, c, sem_ref).wait()
        return carry
    lax.fori_loop(0, n_pages, wait, 0)
    return slot


def _cmp_pages_kernel(table_ref, kcache, vcache, knew_ref, vnew_ref,
                      w1k_ref, bk_ref, w2k_ref, b2k_ref, w1v_ref, bv_ref, w2v_ref, b2v_ref,
                      ko_ref, vo_ref, buf_ref, rows_ref, sem_ref):
    slot = _fetch_pages((kcache, vcache), table_ref, buf_ref, sem_ref)
    n_pages = table_ref.shape[1]
    past = n_pages * PAGE_SIZE
    n_rows = rows_ref.shape[0]
    j = n_rows // CMP_STRIDE
    first_row = lax.broadcasted_iota(jnp.int32, (n_rows - past, KV_WIDTH), 0) == 0
    plan = ((knew_ref, w1k_ref, bk_ref, w2k_ref, b2k_ref, ko_ref),
            (vnew_ref, w1v_ref, bv_ref, w2v_ref, b2v_ref, vo_ref))
    for c, (new_ref, w1_ref, bias_ref, w2_ref, b2_ref, o_ref) in enumerate(plan):
        def to_rows(p, carry):
            off = pl.multiple_of(p * PAGE_SIZE, PAGE_SIZE)
            rows_ref[pl.ds(off, PAGE_SIZE), :] = buf_ref[slot, c, p].T
            return carry
        lax.fori_loop(0, n_pages, to_rows, 0, unroll=8)
        rows_ref[past:, :] = jnp.where(first_row, new_ref[0], 0.0)
        f = None
        for p in range(0, CMP_STRIDE, 2):
            x = jnp.concatenate([rows_ref[pl.ds(p, j, stride=CMP_STRIDE), :],
                                 rows_ref[pl.ds(p + 1, j, stride=CMP_STRIDE), :]], axis=1).astype(BF16)
            part = _dot(x, w1_ref[p * KV_WIDTH:(p + 2) * KV_WIDTH, :])
            f = part if f is None else f + part
        o_ref[0, 0:j, :] = _compress_tail(f, bias_ref, w2_ref, b2_ref)
        o_ref[0, j:, :] = jnp.zeros((o_ref.shape[1] - j, KV_WIDTH), F32)


def _cmp_pages(table, kcache, vcache, knew, vnew, cw_k, cw_v):
    bs, n_pages = table.shape
    past = n_pages * PAGE_SIZE
    n_rows = past + SUBLANES * CMP_STRIDE
    j = n_rows // CMP_STRIDE
    jp = -(-j // LANES) * LANES
    full = lambda a: pl.BlockSpec(a.shape, lambda b, tbl: (0,) * a.ndim)
    new = pl.BlockSpec((1, 1, KV_WIDTH), lambda b, tbl: (b, 0, 0))
    out = pl.BlockSpec((1, jp, KV_WIDTH), lambda b, tbl: (b, 0, 0))
    grid_spec = pltpu.PrefetchScalarGridSpec(
        num_scalar_prefetch=1, grid=(bs,),
        in_specs=[pl.BlockSpec(memory_space=pl.ANY), pl.BlockSpec(memory_space=pl.ANY), new, new]
        + [full(a) for a in cw_k] + [full(a) for a in cw_v],
        out_specs=[out, out],
        scratch_shapes=[pltpu.VMEM((2, 2, n_pages, KV_WIDTH, PAGE_SIZE), F32),
                        pltpu.VMEM((n_rows, KV_WIDTH), F32), pltpu.SemaphoreType.DMA((2, 2))])
    return pl.pallas_call(
        _cmp_pages_kernel, grid_spec=grid_spec,
        out_shape=[jax.ShapeDtypeStruct((bs, jp, KV_WIDTH), F32)] * 2,
        compiler_params=_params("arbitrary"), name="cmp_pages")(
            table, kcache, vcache, knew.reshape(bs, 1, KV_WIDTH), vnew.reshape(bs, 1, KV_WIDTH), *cw_k, *cw_v)


def _nsa_sample_kernel(table_ref, kcache, vcache, q8_ref, gcol_ref, kc_ref, vc_ref, kwt_ref, vwt_ref,
                       ksn_ref, vsn_ref, kwn_ref, vwn_ref, kwc_ref, vwc_ref,
                       o_ref, kwo_ref, vwo_ref, buf_ref, expand_ref, imp_ref, sem_ref):
    b = pl.program_id(0)
    slot = _fetch_pages((kcache, vcache), table_ref, buf_ref, sem_ref)
    past = table_ref.shape[1] * PAGE_SIZE
    q_pos = past
    sel_len = -(-(past + 1) // SEL_BLOCK) * SEL_BLOCK
    nc = sel_len // CMP_STRIDE - 1
    ns = sel_len // SEL_BLOCK
    ncp = kc_ref.shape[1]
    nsp = -(-ns // SUBLANES) * SUBLANES
    nsq = -(-nsp // LANES) * LANES
    nbp = expand_ref.shape[0]
    rows8 = NSA_HEADS
    cur = q_pos // SEL_BLOCK
    n_top = min(SEL_TOP, ns)

    @pl.when(b == 0)
    def _():
        blk_i = lax.broadcasted_iota(jnp.int32, (nbp, past), 0)
        key_i = lax.broadcasted_iota(jnp.int32, (nbp, past), 1)
        expand_ref[...] = (blk_i == key_i // SEL_BLOCK).astype(BF16)

    q8 = q8_ref[0] * (NSA_HEAD_DIM ** -0.5 * math.log2(math.e))
    q8b = q8.astype(BF16)

    st = _nt_dot(kc_ref[0].astype(BF16), q8b)
    n_idx = lax.broadcasted_iota(jnp.int32, (ncp, rows8), 0)
    st = jnp.where((n_idx * CMP_STRIDE + (CMP_BLOCK - 1) <= q_pos) & (n_idx < nc), st, NEG_INF)
    mc = jnp.max(st, axis=0, keepdims=True)
    mc = jnp.where(mc > NEG_INF, mc, 0.0)
    ec = jnp.exp2(st - mc)
    pt = ec / jnp.maximum(jnp.sum(ec, axis=0, keepdims=True), 1e-30)
    o_cmp_t = _dot(vc_ref[0].T.astype(BF16), pt.astype(BF16))
    o_cmp = jnp.concatenate([o_cmp_t, jnp.zeros((KV_WIDTH, LANES - rows8), F32)], axis=1).T[0:rows8, :]

    imp = jnp.concatenate([jnp.sum(pt[:, g * NSA_GROUP:(g + 1) * NSA_GROUP], axis=1, keepdims=True)
                           for g in range(NSA_KV_HEADS)], axis=1)
    imp_ref[0:SUBLANES, :] = jnp.zeros((SUBLANES, NSA_KV_HEADS), F32)
    imp_ref[SUBLANES:SUBLANES + ncp, :] = imp
    ratio = SEL_BLOCK // CMP_STRIDE
    p_slc = imp_ref[pl.ds(SUBLANES - 1, nsp, stride=ratio), :]
    for d in range(ratio):
        p_slc = p_slc + imp_ref[pl.ds(SUBLANES + d, nsp, stride=ratio), :]
    blk = lax.broadcasted_iota(jnp.int32, (nsp, NSA_KV_HEADS), 0)
    forced = (blk == 0) | (blk > cur - N_LOCAL_SEL)
    score = jnp.where(blk <= cur, p_slc + jnp.where(forced, FORCE_BONUS, 0.0), -FORCE_BONUS)
    low = -2.0 * FORCE_BONUS
    score_pad = jnp.concatenate([score, jnp.full((nsp, LANES - NSA_KV_HEADS), low, F32)], axis=1)
    score_pad = jnp.concatenate([score_pad, jnp.full((nsq - nsp, LANES), low, F32)], axis=0)
    score_rows = score_pad.T
    jp = lax.broadcasted_iota(jnp.int32, (nsp, nsq), 0)
    jj = lax.broadcasted_iota(jnp.int32, (nsp, nsq), 1)
    sel_rows = []
    for g in range(NSA_KV_HEADS):
        col = score[:, g:g + 1]
        row = score_rows[g:g + 1, :]
        beats = (col > row) | ((col == row) & (jp < jj))
        rank = jnp.sum(beats.astype(F32), axis=0, keepdims=True)
        sel_rows.append(((rank < float(n_top)) & (jj[0:1, :] <= cur)).astype(F32))
    row_head = lax.broadcasted_iota(jnp.int32, (rows8, nsq), 0) // NSA_GROUP
    sel8 = jnp.where(row_head == 0, sel_rows[0], sel_rows[1])

    chosen = _dot(sel8[:, 0:nbp].astype(BF16), expand_ref[...])
    n_pages = table_ref.shape[1]
    kt16 = jnp.concatenate([buf_ref[slot, 0, p].astype(BF16) for p in range(n_pages)], axis=1)
    vt16 = jnp.concatenate([buf_ref[slot, 1, p].astype(BF16) for p in range(n_pages)], axis=1)
    s = _dot(q8b, kt16) + jnp.where(chosen > 0.5, 0.0, NEG_INF)
    s_new = jnp.sum(q8 * ksn_ref[0], axis=1, keepdims=True)
    m = jnp.maximum(jnp.max(s, axis=1, keepdims=True), s_new)
    p = jnp.exp2(s - m)
    p_new = jnp.exp2(s_new - m)
    o_sel = ((_nt_dot(p.astype(BF16), vt16) + p_new * vsn_ref[0])
             / (jnp.sum(p, axis=1, keepdims=True) + p_new))

    kwt = kwt_ref[0]
    vwt = vwt_ref[0]
    w = kwt.shape[1]
    lane = lax.broadcasted_iota(jnp.int32, (rows8, w), 1)
    w_pos = past - w + lane
    sw = jnp.where((w_pos > q_pos - WINDOW) & (w_pos >= 0), _dot(q8b, kwt.astype(BF16)), NEG_INF)
    sw_new = jnp.sum(q8 * kwn_ref[0], axis=1, keepdims=True)
    mw = jnp.maximum(jnp.max(sw, axis=1, keepdims=True), sw_new)
    pw = jnp.exp2(sw - mw)
    pw_new = jnp.exp2(sw_new - mw)
    o_win = ((_nt_dot(pw.astype(BF16), vwt.astype(BF16)) + pw_new * vwn_ref[0])
             / (jnp.sum(pw, axis=1, keepdims=True) + pw_new))

    gates = gcol_ref[0]
    o_ref[0] = gates[:, 0:1] * o_cmp + gates[:, 1:2] * o_sel + gates[:, 2:3] * o_win

    last = lax.broadcasted_iota(jnp.int32, (KV_WIDTH, w), 1) == w - 1
    kwo_ref[0] = jnp.where(last, kwc_ref[0], pltpu.roll(kwt, w - 1, axis=1))
    vwo_ref[0] = jnp.where(last, vwc_ref[0], pltpu.roll(vwt, w - 1, axis=1))


def _nsa_sample(table, kcache, vcache, q8, gcol, kc_sum, vc_sum, kwt, vwt, ks_new, vs_new, kw_new, vw_new):
    bs, n_pages = table.shape
    past = n_pages * PAGE_SIZE
    w = kwt.shape[2]
    ncp = kc_sum.shape[1]
    nbp = -(-(past // SEL_BLOCK) // LANES) * LANES
    per_b = lambda a: pl.BlockSpec((1,) + a.shape[1:], lambda b, tbl: (b,) + (0,) * (a.ndim - 1))
    row = lambda a: a.reshape(bs, 1, KV_WIDTH)
    col = lambda a: a.reshape(bs, KV_WIDTH, 1)
    operands = (q8, gcol, kc_sum, vc_sum, kwt, vwt, row(ks_new), row(vs_new), row(kw_new), row(vw_new),
                col(kw_new), col(vw_new))
    grid_spec = pltpu.PrefetchScalarGridSpec(
        num_scalar_prefetch=1, grid=(bs,),
        in_specs=[pl.BlockSpec(memory_space=pl.ANY), pl.BlockSpec(memory_space=pl.ANY)]
        + [per_b(a) for a in operands],
        out_specs=[pl.BlockSpec((1, NSA_HEADS, KV_WIDTH), lambda b, tbl: (b, 0, 0)),
                   pl.BlockSpec((1, KV_WIDTH, w), lambda b, tbl: (b, 0, 0)),
                   pl.BlockSpec((1, KV_WIDTH, w), lambda b, tbl: (b, 0, 0))],
        scratch_shapes=[pltpu.VMEM((2, 2, n_pages, KV_WIDTH, PAGE_SIZE), F32), pltpu.VMEM((nbp, past), BF16),
                        pltpu.VMEM((SUBLANES + ncp, NSA_KV_HEADS), F32), pltpu.SemaphoreType.DMA((2, 2))])
    return pl.pallas_call(
        _nsa_sample_kernel, grid_spec=grid_spec,
        out_shape=[jax.ShapeDtypeStruct((bs, NSA_HEADS, KV_WIDTH), F32),
                   jax.ShapeDtypeStruct((bs, KV_WIDTH, w), F32), jax.ShapeDtypeStruct((bs, KV_WIDTH, w), F32)],
        compiler_params=_params("arbitrary"), name="nsa_sample")(table, kcache, vcache, *operands)


def _hgrn_sample_kernel(h4_ref, s_ref, lb_ref, ng_ref, o_ref, so_ref):
    nb = h4_ref.shape[0]
    hd = HGRN_HEAD_DIM
    hw = HGRN_WIDTH
    hq = h4_ref[:, 0:hw]
    lb = lb_ref[...]
    f = lb + (1.0 - lb) * jax.nn.sigmoid(h4_ref[:, hw:2 * hw])
    k = 1.0 - f
    q = hq * jax.nn.sigmoid(hq)
    hg = h4_ref[:, 3 * hw:4 * hw]
    for h in range(HGRN_HEADS):
        cols = slice(h * hd, (h + 1) * hd)
        stack = jnp.concatenate([f[:, cols], k[:, cols], q[:, cols],
                                 jnp.zeros((LANES - 3 * nb, hd), F32)], axis=0).T
        outs = []
        for bi in range(nb):
            v_row = h4_ref[bi:bi + 1, 2 * hw + h * hd:2 * hw + (h + 1) * hd]
            s_new = stack[:, bi:bi + 1] * s_ref[bi, h] + stack[:, nb + bi:nb + bi + 1] * v_row
            so_ref[bi, h] = s_new
            outs.append(jnp.sum(stack[:, 2 * nb + bi:2 * nb + bi + 1] * s_new, axis=0, keepdims=True))
        o = jnp.concatenate(outs, axis=0)
        o = o * lax.rsqrt(jnp.mean(o * o, axis=-1, keepdims=True) + LN_EPS) * ng_ref[...]
        o_ref[:, cols] = o * (hg[:, cols] * jax.nn.sigmoid(hg[:, cols]))


def _hgrn_sample(h4, s0, lb, norm_g):
    bs = h4.shape[0]
    nb = SUBLANES
    assert bs % nb == 0, bs
    st_spec = pl.BlockSpec((nb, HGRN_HEADS, HGRN_HEAD_DIM, HGRN_HEAD_DIM), lambda i: (i, 0, 0, 0))
    return pl.pallas_call(
        _hgrn_sample_kernel, grid=(bs // nb,),
        in_specs=[pl.BlockSpec((nb, 4 * HGRN_WIDTH), lambda i: (i, 0)), st_spec,
                  pl.BlockSpec((1, HGRN_WIDTH), lambda i: (0, 0)),
                  pl.BlockSpec((1, HGRN_HEAD_DIM), lambda i: (0, 0))],
        out_specs=[pl.BlockSpec((nb, HGRN_WIDTH), lambda i: (i, 0)), st_spec],
        out_shape=[jax.ShapeDtypeStruct((bs, HGRN_WIDTH), F32), jax.ShapeDtypeStruct(s0.shape, F32)],
        compiler_params=_params("parallel"), name="hgrn_sample")(h4, s0, lb, norm_g)


def kernel(x_prompt, x_sample, cache_k_cmp, cache_v_cmp, cache_k_sel, cache_v_sel, cache_k_win, cache_v_win, state_hgrn, page_table, w_in, cmp_pe, cmp_w1, cmp_b1, cmp_w2, cmp_b2, hgrn_gamma, hgrn_norm, w_branch_a, w_branch_b, w_out, ln1_g, ln1_b, w_router, b_router, w_gate_up, b_gate_up, w_down, b_down, ln2_g, ln2_b):
    l = 0
    bp, t, _ = x_prompt.shape
    bs = x_sample.shape[0]
    past = page_table.shape[1] * PAGE_SIZE
    win_keep = cache_k_win.shape[2]

    lower = jnp.cumsum(jax.nn.softmax(hgrn_gamma.astype(F32), axis=0), axis=0)[l][None, :]
    norm_g = hgrn_norm[l][None, :]
    w = w_in[l]
    c_g = NSA_WIDTH + 6 * KV_WIDTH
    n_g = 3 * NSA_HEADS
    w_main = jnp.concatenate([w[:, :c_g], w[:, c_g + n_g:]], axis=1).astype(BF16)
    w_gate_t = jnp.pad(w[:, c_g:c_g + n_g].T, ((0, GATE_ROWS - n_g), (0, 0))).astype(BF16)
    cw = [_compress_weights(cmp_pe[l, n], cmp_w1[l, n], cmp_b1[l, n], cmp_w2[l, n], cmp_b2[l, n]) for n in range(2)]
    wba = w_branch_a[l].astype(BF16)
    wbb = w_branch_b[l].astype(BF16)
    wout = w_out[l].astype(BF16)
    wr_t = w_router[l].T
    wr_hi = wr_t.astype(BF16)
    wr_lo = (wr_t - wr_hi.astype(F32)).astype(BF16)
    br = jnp.broadcast_to(b_router[l].astype(F32)[:, None], (N_EXPERTS, 1))
    g1, b1 = ln1_g[l][None, :], ln1_b[l][None, :]
    g2, b2 = ln2_g[l][None, :], ln2_b[l][None, :]

    def merge_and_dispatch(x2d, o_a, o_b, gab, tm):
        x1, x1b, logit_t = _finish(x2d, o_a, o_b, gab, wba, wbb, wout, g1, b1, wr_hi, wr_lo, br, tm)
        return x1, _moe_dispatch(x1b, logit_t)

    def experts_and_norm(x1, dispatch, tm, after=None):
        return _moe_finish(x1, dispatch, w_gate_up[l], b_gate_up[l], w_down[l], b_down[l], g2, b2, tm, after)

    n = bp * t
    xp = x_prompt.reshape(n, D_MODEL)
    q, kc, vc, ks, vs, kw, vw, h4, gab, gt, *kv_t = _project(xp, w_main, w_gate_t, 256, bp)
    r3 = lambda a: a.reshape(bp, t, a.shape[-1])
    kc_sum = _compress(r3(kc), *cw[0])
    vc_sum = _compress(r3(vc), *cw[1])
    o_nsa = _nsa_prompt(r3(q), gt, kc_sum, vc_sum, r3(ks), r3(vs), r3(kw), r3(vw))
    o_hgrn, p_state = _hgrn_prompt(r3(h4), lower, norm_g, bp, t)
    x1_p, disp_p = merge_and_dispatch(xp, o_nsa.reshape(n, NSA_WIDTH), o_hgrn.reshape(n, HGRN_WIDTH), gab, 256)
    win = min(WINDOW, t)
    t5 = lambda a: jnp.transpose(a.reshape(a.shape[0], NSA_KV_HEADS, NSA_HEAD_DIM, a.shape[2]), (0, 3, 1, 2))[None]
    new_p = (tuple(t5(a) for a in kv_t[:4]) + tuple(t5(a[:, :, t - win:]) for a in kv_t[4:])
             + (p_state[None],))

    page_table, _ = lax.optimization_barrier((page_table, disp_p[-1][:1]))
    xs = x_sample.reshape(bs, D_MODEL)
    q, kc, vc, ks, vs, kw, vw, h4, gab, gt, *_ = _project(xs, w_main, w_gate_t, bs, 1)
    pages = lambda c: jnp.transpose(c[l], (0, 2, 3, 1)).reshape(c.shape[1], KV_WIDTH, PAGE_SIZE)
    band = lambda c: jnp.transpose(c[l], (0, 2, 3, 1)).reshape(bs, KV_WIDTH, win_keep)
    kc_sum, vc_sum = _cmp_pages(page_table, pages(cache_k_cmp), pages(cache_v_cmp), kc, vc, cw[0], cw[1])
    head_eye = jnp.eye(NSA_KV_HEADS, dtype=F32)
    q8 = jnp.einsum('bgrd,gh->bgrhd', q.reshape(bs, NSA_KV_HEADS, NSA_GROUP, NSA_HEAD_DIM),
                    head_eye).reshape(bs, NSA_HEADS, KV_WIDTH)
    gcol = jnp.pad(gt[:3 * NSA_HEADS].T.reshape(bs, 3, NSA_HEADS).transpose(0, 2, 1),
                   ((0, 0), (0, 0), (0, LANES - 3)))
    o8, kw_t, vw_t = _nsa_sample(page_table, pages(cache_k_sel), pages(cache_v_sel), q8, gcol, kc_sum, vc_sum,
                                 band(cache_k_win), band(cache_v_win), ks, vs, kw, vw)
    o_nsa = jnp.einsum('bgrhd,gh->bgrd', o8.reshape(bs, NSA_KV_HEADS, NSA_GROUP, NSA_KV_HEADS, NSA_HEAD_DIM),
                       head_eye).reshape(bs, NSA_WIDTH)
    o_hgrn, s_state = _hgrn_sample(h4, state_hgrn[l], lower, norm_g)
    y_prompt = experts_and_norm(x1_p, disp_p, 256, after=(o8, s_state)).reshape(bp, t, D_MODEL)
    x1_s, disp_s = merge_and_dispatch(xs, o_nsa, o_hgrn, gab, bs)
    y_sample = experts_and_norm(x1_s, disp_s, bs).reshape(bs, 1, D_MODEL)
    s5 = lambda a: a.reshape(1, bs, 1, NSA_KV_HEADS, NSA_HEAD_DIM)
    w5 = lambda a: jnp.transpose(a.reshape(bs, NSA_KV_HEADS, NSA_HEAD_DIM, win_keep), (0, 3, 1, 2))[None]
    new_s = (s5(kc), s5(vc), s5(ks), s5(vs), w5(kw_t), w5(vw_t), s_state[None])

    return (y_prompt, y_sample) + new_p + new_s
```
